```python
import math
import jax, jax.numpy as jnp
from jax import lax
import numpy as np

D_MODEL = 2048
BATCH = 8
SEQ = 8192
DEPTH = 4

CHUNK = 64
N_MEM = 256
N_MIXERS = 2
N_A = (DEPTH + 1) // 2
N_B = DEPTH // 2
HEAD_DIM = 128
MEM_W = D_MODEL // 4
MEM_HEADS = MEM_W // HEAD_DIM
TOK_W = D_MODEL - MEM_W
SB_HEADS = TOK_W // HEAD_DIM
SB_BLOCK = 128
S5_GROUP = 16
S5_GROUPS = TOK_W // S5_GROUP
S5_STATE = 64
FFN_DIM = 256 * math.ceil(8 * D_MODEL / 3 / 256)
EPS = 1e-6

kernel_name = "hybrid_stickbreak_s5_macaron_memory_trunk"


def rms_norm(x, g):
    x32 = x.astype(jnp.float32)
    y = x32 * lax.rsqrt(jnp.mean(x32 * x32, axis=-1, keepdims=True) + EPS) * g.astype(jnp.float32)
    return y.astype(x.dtype)


def swiglu_ffn(h, w_gu, w_down):
    g, u = jnp.split(h @ w_gu, 2, axis=-1)
    return (jax.nn.silu(g) * u) @ w_down


def stick_breaking_attention(q, k, v):
    B, L, H, d = q.shape
    q, k, v = (a.transpose(0, 2, 1, 3) for a in (q, k, v))
    scale = 1.0 / math.sqrt(d)
    outs = []
    for qs in range(0, L, SB_BLOCK):
        qe = qs + SB_BLOCK
        z = jnp.einsum("bhqd,bhkd->bhqk", q[:, :, qs:qe], k[:, :, :qe]).astype(jnp.float32) * scale
        t_idx = qs + jnp.arange(SB_BLOCK)[:, None]
        s_idx = jnp.arange(qe)[None, :]
        before = s_idx < t_idx
        log_not = jnp.where(before, jax.nn.log_sigmoid(-z), 0.0)
        between = lax.cumsum(log_not, axis=3, reverse=True) - log_not
        w = jnp.where(before, jnp.exp(jax.nn.log_sigmoid(z) + between), 0.0)
        outs.append(jnp.einsum("bhqk,bhkd->bhqd", w.astype(v.dtype), v[:, :, :qe]))
    o = jnp.concatenate(outs, axis=2)
    return o.transpose(0, 2, 1, 3).reshape(B, L, H * d)


def _ssm_combine(left, right):
    a_l, b_l = left
    a_r, b_r = right
    return (a_r * a_l, a_r * b_l + b_r)


def s5_glu(u, log_dt, a_re, a_im, b_re, b_im, c_re, c_im, d_skip, w_glu):
    B, L, _ = u.shape
    f32 = jnp.float32
    dt = jnp.exp(log_dt.astype(f32))[:, None]
    lam = lax.complex(a_re.astype(f32), a_im.astype(f32))
    a_bar = jnp.exp(lam * dt)
    b = lax.complex(b_re.astype(f32), b_im.astype(f32))
    b_bar = ((a_bar - 1.0) / lam)[..., None] * b
    c = lax.complex(c_re.astype(f32), c_im.astype(f32))
    uc = u.astype(f32).reshape(B, L, S5_GROUPS, S5_GROUP)
    bu = jnp.einsum("gnc,blgc->blgn", b_bar, uc.astype(jnp.complex64))
    a_elems = jnp.broadcast_to(a_bar[None, None], bu.shape)
    _, states = lax.associative_scan(_ssm_combine, (a_elems, bu), axis=1)
    y = jnp.einsum("gcn,blgn->blgc", c, states).real + d_skip.astype(f32).reshape(S5_GROUPS, S5_GROUP) * uc
    y = jax.nn.gelu(y.reshape(B, L, TOK_W))
    y = y * jax.nn.sigmoid(y @ w_glu.astype(f32))
    return y.astype(u.dtype)


def memory_cross_attention(q, mem_h, w_mem_kv, q_gain, k_gain):
    B, L, _ = q.shape
    M = mem_h.shape[1]
    k, v = jnp.split(mem_h @ w_mem_kv, 2, axis=-1)
    q = rms_norm(q.reshape(B, L, MEM_HEADS, HEAD_DIM), q_gain)
    k = rms_norm(k.reshape(B, M, MEM_HEADS, HEAD_DIM), k_gain)
    v = v.reshape(B, M, MEM_HEADS, HEAD_DIM)
    s = jnp.einsum("blhd,bmhd->bhlm", q, k).astype(jnp.float32) / math.sqrt(HEAD_DIM)
    p = jax.nn.softmax(s, axis=-1).astype(v.dtype)
    o = jnp.einsum("bhlm,bmhd->blhd", p, v)
    return o.reshape(B, L, MEM_W)


def _fwd_setup_inputs(seed: int = 0) -> dict:
    key = jax.random.key(seed)
    ks = jax.random.split(key, 32)
    f32 = jnp.float32

    def nrm(k, shape, fan_in):
        return jax.random.normal(k, shape, f32) * fan_in ** -0.5

    def gain(k, shape):
        return 1.0 + 0.02 * jax.random.normal(k, shape, f32)

    G, N, C = S5_GROUPS, S5_STATE, S5_GROUP
    return dict(
        x=jax.random.normal(ks[0], (BATCH, SEQ, D_MODEL), f32),
        mem=jax.random.normal(ks[1], (BATCH, N_MEM, D_MODEL), f32),
        ffn1_norm=gain(ks[2], (DEPTH, D_MODEL)),
        ffn1_w_gu=nrm(ks[3], (DEPTH, D_MODEL, 2 * FFN_DIM), D_MODEL),
        ffn1_w_down=nrm(ks[4], (DEPTH, FFN_DIM, D_MODEL), FFN_DIM),
        mix_norm=gain(ks[5], (DEPTH, D_MODEL)),
        mem_norm=gain(ks[6], (DEPTH, D_MODEL)),
        w_mem_kv=nrm(ks[7], (DEPTH, D_MODEL, 2 * MEM_W), D_MODEL),
        xq_norm=gain(ks[8], (DEPTH, HEAD_DIM)),
        xk_norm=gain(ks[9], (DEPTH, HEAD_DIM)),
        w_out=nrm(ks[10], (DEPTH, TOK_W + MEM_W, D_MODEL), TOK_W + MEM_W),
        ffn2_norm=gain(ks[11], (DEPTH, D_MODEL)),
        ffn2_w_gu=nrm(ks[12], (DEPTH, D_MODEL, 2 * FFN_DIM), D_MODEL),
        ffn2_w_down=nrm(ks[13], (DEPTH, FFN_DIM, D_MODEL), FFN_DIM),
        sb_w_in=nrm(ks[14], (N_A, D_MODEL, 3 * TOK_W + MEM_W), D_MODEL),
        s5_w_in=nrm(ks[15], (N_B, D_MODEL, TOK_W + MEM_W), D_MODEL),
        s5_log_dt=jax.random.uniform(ks[16], (N_B, G), f32, math.log(1e-3), math.log(1e-1)),
        s5_a_re=-0.5 + 0.01 * jax.random.normal(ks[17], (N_B, G, N), f32),
        s5_a_im=jnp.pi * jnp.arange(N, dtype=f32) + 0.01 * jax.random.normal(ks[18], (N_B, G, N), f32),
        s5_b_re=nrm(ks[19], (N_B, G, N, C), 2 * C),
        s5_b_im=nrm(ks[20], (N_B, G, N, C), 2 * C),
        s5_c_re=nrm(ks[21], (N_B, G, C, N), 2 * N),
        s5_c_im=nrm(ks[22], (N_B, G, C, N), 2 * N),
        s5_d=jax.random.normal(ks[23], (N_B, TOK_W), f32),
        s5_w_glu=nrm(ks[24], (N_B, TOK_W, TOK_W), TOK_W),
    )


def _fwd_reference(x, mem, ffn1_norm, ffn1_w_gu, ffn1_w_down, mix_norm, mem_norm, w_mem_kv,
              xq_norm, xk_norm, w_out, ffn2_norm, ffn2_w_gu, ffn2_w_down, sb_w_in,
              s5_w_in, s5_log_dt, s5_a_re, s5_a_im, s5_b_re, s5_b_im, s5_c_re, s5_c_im,
              s5_d, s5_w_glu):
    B, L, _ = x.shape
    for i in range(DEPTH):
        x = x + 0.5 * swiglu_ffn(rms_norm(x, ffn1_norm[i]), ffn1_w_gu[i], ffn1_w_down[i])
        h = rms_norm(x, mix_norm[i])
        j = i // N_MIXERS
        if i % N_MIXERS == 0:
            q, k, v, q_mem = jnp.split(h @ sb_w_in[j], [TOK_W, 2 * TOK_W, 3 * TOK_W], axis=-1)
            shp = (B, L, SB_HEADS, HEAD_DIM)
            tok = stick_breaking_attention(q.reshape(shp), k.reshape(shp), v.reshape(shp))
        else:
            u, q_mem = jnp.split(h @ s5_w_in[j], [TOK_W], axis=-1)
            tok = s5_glu(u, s5_log_dt[j], s5_a_re[j], s5_a_im[j], s5_b_re[j], s5_b_im[j],
                         s5_c_re[j], s5_c_im[j], s5_d[j], s5_w_glu[j])
        mem_h = rms_norm(mem, mem_norm[i])
        cross = memory_cross_attention(q_mem, mem_h, w_mem_kv[i], xq_norm[i], xk_norm[i])
        x = x + jnp.concatenate([tok, cross], axis=-1) @ w_out[i]
        x = x + 0.5 * swiglu_ffn(rms_norm(x, ffn2_norm[i]), ffn2_w_gu[i], ffn2_w_down[i])
    return x


import jax as _jax
import jax.numpy as _jnp

TWIN_FORMAT = 'train_step'
FWD_PARAMS = ['x', 'mem', 'ffn1_norm', 'ffn1_w_gu', 'ffn1_w_down', 'mix_norm', 'mem_norm', 'w_mem_kv', 'xq_norm', 'xk_norm', 'w_out', 'ffn2_norm', 'ffn2_w_gu', 'ffn2_w_down', 'sb_w_in', 's5_w_in', 's5_log_dt', 's5_a_re', 's5_a_im', 's5_b_re', 's5_b_im', 's5_c_re', 's5_c_im', 's5_d', 's5_w_glu']
TWIN_WEIGHTS = ['ffn1_norm', 'ffn1_w_gu', 'ffn1_w_down', 'mix_norm', 'mem_norm', 'w_mem_kv', 'xq_norm', 'xk_norm', 'w_out', 'ffn2_norm', 'ffn2_w_gu', 'ffn2_w_down', 'sb_w_in', 's5_w_in', 's5_log_dt', 's5_a_re', 's5_a_im', 's5_b_re', 's5_b_im', 's5_c_re', 's5_c_im', 's5_d', 's5_w_glu']
TWIN_DIFF_INPUT = 'x'
TWIN_INPUTS = ['x', 'mem', 'ffn1_norm', 'ffn1_w_gu', 'ffn1_w_down', 'mix_norm', 'mem_norm', 'w_mem_kv', 'xq_norm', 'xk_norm', 'w_out', 'ffn2_norm', 'ffn2_w_gu', 'ffn2_w_down', 'sb_w_in', 's5_w_in', 's5_log_dt', 's5_a_re', 's5_a_im', 's5_b_re', 's5_b_im', 's5_c_re', 's5_c_im', 's5_d', 's5_w_glu', 'loss_target', 'm_ffn1_norm', 'm_ffn1_w_gu', 'm_ffn1_w_down', 'm_mix_norm', 'm_mem_norm', 'm_w_mem_kv', 'm_xq_norm', 'm_xk_norm', 'm_w_out', 'm_ffn2_norm', 'm_ffn2_w_gu', 'm_ffn2_w_down', 'm_sb_w_in', 'm_s5_w_in', 'm_s5_log_dt', 'm_s5_a_re', 'm_s5_a_im', 'm_s5_b_re', 'm_s5_b_im', 'm_s5_c_re', 'm_s5_c_im', 'm_s5_d', 'm_s5_w_glu', 'v_ffn1_norm', 'v_ffn1_w_gu', 'v_ffn1_w_down', 'v_mix_norm', 'v_mem_norm', 'v_w_mem_kv', 'v_xq_norm', 'v_xk_norm', 'v_w_out', 'v_ffn2_norm', 'v_ffn2_w_gu', 'v_ffn2_w_down', 'v_sb_w_in', 'v_s5_w_in', 'v_s5_log_dt', 'v_s5_a_re', 'v_s5_a_im', 'v_s5_b_re', 'v_s5_b_im', 'v_s5_c_re', 'v_s5_c_im', 'v_s5_d', 'v_s5_w_glu']
TWIN_OUTPUTS = ['loss', 'grad_x', 'grad_ffn1_norm', 'grad_ffn1_w_gu', 'grad_ffn1_w_down', 'grad_mix_norm', 'grad_mem_norm', 'grad_w_mem_kv', 'grad_xq_norm', 'grad_xk_norm', 'grad_w_out', 'grad_ffn2_norm', 'grad_ffn2_w_gu', 'grad_ffn2_w_down', 'grad_sb_w_in', 'grad_s5_w_in', 'grad_s5_log_dt', 'grad_s5_a_re', 'grad_s5_a_im', 'grad_s5_b_re', 'grad_s5_b_im', 'grad_s5_c_re', 'grad_s5_c_im', 'grad_s5_d', 'grad_s5_w_glu', 'delta_ffn1_norm', 'delta_ffn1_w_gu', 'delta_ffn1_w_down', 'delta_mix_norm', 'delta_mem_norm', 'delta_w_mem_kv', 'delta_xq_norm', 'delta_xk_norm', 'delta_w_out', 'delta_ffn2_norm', 'delta_ffn2_w_gu', 'delta_ffn2_w_down', 'delta_sb_w_in', 'delta_s5_w_in', 'delta_s5_log_dt', 'delta_s5_a_re', 'delta_s5_a_im', 'delta_s5_b_re', 'delta_s5_b_im', 'delta_s5_c_re', 'delta_s5_c_im', 'delta_s5_d', 'delta_s5_w_glu', 'new_m_ffn1_norm', 'new_m_ffn1_w_gu', 'new_m_ffn1_w_down', 'new_m_mix_norm', 'new_m_mem_norm', 'new_m_w_mem_kv', 'new_m_xq_norm', 'new_m_xk_norm', 'new_m_w_out', 'new_m_ffn2_norm', 'new_m_ffn2_w_gu', 'new_m_ffn2_w_down', 'new_m_sb_w_in', 'new_m_s5_w_in', 'new_m_s5_log_dt', 'new_m_s5_a_re', 'new_m_s5_a_im', 'new_m_s5_b_re', 'new_m_s5_b_im', 'new_m_s5_c_re', 'new_m_s5_c_im', 'new_m_s5_d', 'new_m_s5_w_glu', 'new_v_ffn1_norm', 'new_v_ffn1_w_gu', 'new_v_ffn1_w_down', 'new_v_mix_norm', 'new_v_mem_norm', 'new_v_w_mem_kv', 'new_v_xq_norm', 'new_v_xk_norm', 'new_v_w_out', 'new_v_ffn2_norm', 'new_v_ffn2_w_gu', 'new_v_ffn2_w_down', 'new_v_sb_w_in', 'new_v_s5_w_in', 'new_v_s5_log_dt', 'new_v_s5_a_re', 'new_v_s5_a_im', 'new_v_s5_b_re', 'new_v_s5_b_im', 'new_v_s5_c_re', 'new_v_s5_c_im', 'new_v_s5_d', 'new_v_s5_w_glu']
TWIN_LEAF_KINDS = {'loss': 'loss', 'grad_x': 'grad_x', 'grad_ffn1_norm': 'grad_w', 'grad_ffn1_w_gu': 'grad_w', 'grad_ffn1_w_down': 'grad_w', 'grad_mix_norm': 'grad_w', 'grad_mem_norm': 'grad_w', 'grad_w_mem_kv': 'grad_w', 'grad_xq_norm': 'grad_w', 'grad_xk_norm': 'grad_w', 'grad_w_out': 'grad_w', 'grad_ffn2_norm': 'grad_w', 'grad_ffn2_w_gu': 'grad_w', 'grad_ffn2_w_down': 'grad_w', 'grad_sb_w_in': 'grad_w', 'grad_s5_w_in': 'grad_w', 'grad_s5_log_dt': 'grad_w', 'grad_s5_a_re': 'grad_w', 'grad_s5_a_im': 'grad_w', 'grad_s5_b_re': 'grad_w', 'grad_s5_b_im': 'grad_w', 'grad_s5_c_re': 'grad_w', 'grad_s5_c_im': 'grad_w', 'grad_s5_d': 'grad_w', 'grad_s5_w_glu': 'grad_w', 'delta_ffn1_norm': 'delta_w', 'delta_ffn1_w_gu': 'delta_w', 'delta_ffn1_w_down': 'delta_w', 'delta_mix_norm': 'delta_w', 'delta_mem_norm': 'delta_w', 'delta_w_mem_kv': 'delta_w', 'delta_xq_norm': 'delta_w', 'delta_xk_norm': 'delta_w', 'delta_w_out': 'delta_w', 'delta_ffn2_norm': 'delta_w', 'delta_ffn2_w_gu': 'delta_w', 'delta_ffn2_w_down': 'delta_w', 'delta_sb_w_in': 'delta_w', 'delta_s5_w_in': 'delta_w', 'delta_s5_log_dt': 'delta_w', 'delta_s5_a_re': 'delta_w', 'delta_s5_a_im': 'delta_w', 'delta_s5_b_re': 'delta_w', 'delta_s5_b_im': 'delta_w', 'delta_s5_c_re': 'delta_w', 'delta_s5_c_im': 'delta_w', 'delta_s5_d': 'delta_w', 'delta_s5_w_glu': 'delta_w', 'new_m_ffn1_norm': 'new_m', 'new_m_ffn1_w_gu': 'new_m', 'new_m_ffn1_w_down': 'new_m', 'new_m_mix_norm': 'new_m', 'new_m_mem_norm': 'new_m', 'new_m_w_mem_kv': 'new_m', 'new_m_xq_norm': 'new_m', 'new_m_xk_norm': 'new_m', 'new_m_w_out': 'new_m', 'new_m_ffn2_norm': 'new_m', 'new_m_ffn2_w_gu': 'new_m', 'new_m_ffn2_w_down': 'new_m', 'new_m_sb_w_in': 'new_m', 'new_m_s5_w_in': 'new_m', 'new_m_s5_log_dt': 'new_m', 'new_m_s5_a_re': 'new_m', 'new_m_s5_a_im': 'new_m', 'new_m_s5_b_re': 'new_m', 'new_m_s5_b_im': 'new_m', 'new_m_s5_c_re': 'new_m', 'new_m_s5_c_im': 'new_m', 'new_m_s5_d': 'new_m', 'new_m_s5_w_glu': 'new_m', 'new_v_ffn1_norm': 'new_v', 'new_v_ffn1_w_gu': 'new_v', 'new_v_ffn1_w_down': 'new_v', 'new_v_mix_norm': 'new_v', 'new_v_mem_norm': 'new_v', 'new_v_w_mem_kv': 'new_v', 'new_v_xq_norm': 'new_v', 'new_v_xk_norm': 'new_v', 'new_v_w_out': 'new_v', 'new_v_ffn2_norm': 'new_v', 'new_v_ffn2_w_gu': 'new_v', 'new_v_ffn2_w_down': 'new_v', 'new_v_sb_w_in': 'new_v', 'new_v_s5_w_in': 'new_v', 'new_v_s5_log_dt': 'new_v', 'new_v_s5_a_re': 'new_v', 'new_v_s5_a_im': 'new_v', 'new_v_s5_b_re': 'new_v', 'new_v_s5_b_im': 'new_v', 'new_v_s5_c_re': 'new_v', 'new_v_s5_c_im': 'new_v', 'new_v_s5_d': 'new_v', 'new_v_s5_w_glu': 'new_v'}


def _forward(args):
    return _fwd_reference(*[args[k] for k in FWD_PARAMS])


def _output_shape():
    def fwd():
        inp = _fwd_setup_inputs(0)
        return _fwd_reference(*[inp[k] for k in FWD_PARAMS])
    out = _jax.eval_shape(fwd)
    return out.shape, out.dtype

N_MICROBATCH = 1
ADAM_LR = 0.001
ADAM_B1 = 0.9
ADAM_B2 = 0.999
ADAM_EPS = 1e-08
ADAM_WD = 0.01
ADAM_STEP = 10
PER_EXAMPLE_BATCH_AXIS = {'x': 0, 'mem': 0, 'loss_target': 0}
SHARED_INPUTS = []
_WEIGHT_DTYPES = {'ffn1_norm': _jnp.float32, 'ffn1_w_gu': _jnp.float32, 'ffn1_w_down': _jnp.float32, 'mix_norm': _jnp.float32, 'mem_norm': _jnp.float32, 'w_mem_kv': _jnp.float32, 'xq_norm': _jnp.float32, 'xk_norm': _jnp.float32, 'w_out': _jnp.float32, 'ffn2_norm': _jnp.float32, 'ffn2_w_gu': _jnp.float32, 'ffn2_w_down': _jnp.float32, 'sb_w_in': _jnp.float32, 's5_w_in': _jnp.float32, 's5_log_dt': _jnp.float32, 's5_a_re': _jnp.float32, 's5_a_im': _jnp.float32, 's5_b_re': _jnp.float32, 's5_b_im': _jnp.float32, 's5_c_re': _jnp.float32, 's5_c_im': _jnp.float32, 's5_d': _jnp.float32, 's5_w_glu': _jnp.float32}
MOMENT_SCALE = {'ffn1_norm': 6.065897e+00, 'ffn1_w_gu': 8.793468e-02, 'ffn1_w_down': 1.470823e-01, 'mix_norm': 9.773400e+00, 'mem_norm': 1.227356e-01, 'w_mem_kv': 1.317121e-01, 'xq_norm': 1.108743e+00, 'xk_norm': 1.108372e+00, 'w_out': 8.641957e-01, 'ffn2_norm': 6.101217e+00, 'ffn2_w_gu': 8.433685e-02, 'ffn2_w_down': 1.382954e-01, 'sb_w_in': 3.583196e-01, 's5_w_in': 4.565025e-01, 's5_log_dt': 4.902695e+00, 's5_a_re': 2.660042e-02, 's5_a_im': 3.327533e-02, 's5_b_re': 2.838993e-02, 's5_b_im': 2.536854e-02, 's5_c_re': 5.319826e-02, 's5_c_im': 5.153748e-02, 's5_d': 5.173186e+00, 's5_w_glu': 1.059485e+00}


def _to_microbatches(a, axis):
    t = _jnp.moveaxis(a, axis, 0)
    t = t.reshape((N_MICROBATCH, t.shape[0] // N_MICROBATCH) + t.shape[1:])
    return _jnp.moveaxis(t, 1, axis + 1)


def setup_inputs(seed: int = 0) -> dict:
    inp = _fwd_setup_inputs(seed)
    key = _jax.random.fold_in(_jax.random.key(seed), 7919)
    shape, _ = _output_shape()
    out = dict(inp)
    out["loss_target"] = _jax.random.normal(_jax.random.fold_in(key, 0), shape, _jnp.float32)
    for i, name in enumerate(TWIN_WEIGHTS):
        w = inp[name].astype(_jnp.float32)
        if MOMENT_SCALE is None:
            s = _jnp.sqrt(_jnp.mean(_jnp.square(w)) + 1e-30)
        else:
            s = MOMENT_SCALE[name]
        km, kv = _jax.random.split(_jax.random.fold_in(key, i + 1))
        out[name] = w
        out["m_" + name] = s * _jax.random.normal(km, w.shape, _jnp.float32)
        out["v_" + name] = (s * s) * _jax.random.uniform(kv, w.shape, _jnp.float32, 0.5, 1.5)
    if N_MICROBATCH > 1:
        for name, axis in PER_EXAMPLE_BATCH_AXIS.items():
            out[name] = _to_microbatches(out[name], axis)
    return {'x': out['x'], 'mem': out['mem'], 'ffn1_norm': out['ffn1_norm'], 'ffn1_w_gu': out['ffn1_w_gu'], 'ffn1_w_down': out['ffn1_w_down'], 'mix_norm': out['mix_norm'], 'mem_norm': out['mem_norm'], 'w_mem_kv': out['w_mem_kv'], 'xq_norm': out['xq_norm'], 'xk_norm': out['xk_norm'], 'w_out': out['w_out'], 'ffn2_norm': out['ffn2_norm'], 'ffn2_w_gu': out['ffn2_w_gu'], 'ffn2_w_down': out['ffn2_w_down'], 'sb_w_in': out['sb_w_in'], 's5_w_in': out['s5_w_in'], 's5_log_dt': out['s5_log_dt'], 's5_a_re': out['s5_a_re'], 's5_a_im': out['s5_a_im'], 's5_b_re': out['s5_b_re'], 's5_b_im': out['s5_b_im'], 's5_c_re': out['s5_c_re'], 's5_c_im': out['s5_c_im'], 's5_d': out['s5_d'], 's5_w_glu': out['s5_w_glu'], 'loss_target': out['loss_target'], 'm_ffn1_norm': out['m_ffn1_norm'], 'm_ffn1_w_gu': out['m_ffn1_w_gu'], 'm_ffn1_w_down': out['m_ffn1_w_down'], 'm_mix_norm': out['m_mix_norm'], 'm_mem_norm': out['m_mem_norm'], 'm_w_mem_kv': out['m_w_mem_kv'], 'm_xq_norm': out['m_xq_norm'], 'm_xk_norm': out['m_xk_norm'], 'm_w_out': out['m_w_out'], 'm_ffn2_norm': out['m_ffn2_norm'], 'm_ffn2_w_gu': out['m_ffn2_w_gu'], 'm_ffn2_w_down': out['m_ffn2_w_down'], 'm_sb_w_in': out['m_sb_w_in'], 'm_s5_w_in': out['m_s5_w_in'], 'm_s5_log_dt': out['m_s5_log_dt'], 'm_s5_a_re': out['m_s5_a_re'], 'm_s5_a_im': out['m_s5_a_im'], 'm_s5_b_re': out['m_s5_b_re'], 'm_s5_b_im': out['m_s5_b_im'], 'm_s5_c_re': out['m_s5_c_re'], 'm_s5_c_im': out['m_s5_c_im'], 'm_s5_d': out['m_s5_d'], 'm_s5_w_glu': out['m_s5_w_glu'], 'v_ffn1_norm': out['v_ffn1_norm'], 'v_ffn1_w_gu': out['v_ffn1_w_gu'], 'v_ffn1_w_down': out['v_ffn1_w_down'], 'v_mix_norm': out['v_mix_norm'], 'v_mem_norm': out['v_mem_norm'], 'v_w_mem_kv': out['v_w_mem_kv'], 'v_xq_norm': out['v_xq_norm'], 'v_xk_norm': out['v_xk_norm'], 'v_w_out': out['v_w_out'], 'v_ffn2_norm': out['v_ffn2_norm'], 'v_ffn2_w_gu': out['v_ffn2_w_gu'], 'v_ffn2_w_down': out['v_ffn2_w_down'], 'v_sb_w_in': out['v_sb_w_in'], 'v_s5_w_in': out['v_s5_w_in'], 'v_s5_log_dt': out['v_s5_log_dt'], 'v_s5_a_re': out['v_s5_a_re'], 'v_s5_a_im': out['v_s5_a_im'], 'v_s5_b_re': out['v_s5_b_re'], 'v_s5_b_im': out['v_s5_b_im'], 'v_s5_c_re': out['v_s5_c_re'], 'v_s5_c_im': out['v_s5_c_im'], 'v_s5_d': out['v_s5_d'], 'v_s5_w_glu': out['v_s5_w_glu']}


def _loss(weights, diff, rest, loss_target):
    with _jax.named_scope("forward"):
        args = {**rest, TWIN_DIFF_INPUT: diff, **{k: w.astype(_WEIGHT_DTYPES[k]) for k, w in weights.items()}}
        y = _forward(args)
    with _jax.named_scope("loss_head"):
        err = _jnp.square(y.astype(_jnp.float32) - loss_target)
        return 0.5 * _jnp.sum(_jnp.mean(err, axis=-1)) if err.ndim else 0.5 * err


def _adamw(w, g, m, v):
    m = ADAM_B1 * m + (1.0 - ADAM_B1) * g
    v = ADAM_B2 * v + (1.0 - ADAM_B2) * _jnp.square(g)
    m_hat = m / (1.0 - ADAM_B1 ** ADAM_STEP)
    v_hat = v / (1.0 - ADAM_B2 ** ADAM_STEP)
    delta = -ADAM_LR * (m_hat / (_jnp.sqrt(v_hat) + ADAM_EPS) + ADAM_WD * w)
    return delta, m, v


def reference(x, mem, ffn1_norm, ffn1_w_gu, ffn1_w_down, mix_norm, mem_norm, w_mem_kv, xq_norm, xk_norm, w_out, ffn2_norm, ffn2_w_gu, ffn2_w_down, sb_w_in, s5_w_in, s5_log_dt, s5_a_re, s5_a_im, s5_b_re, s5_b_im, s5_c_re, s5_c_im, s5_d, s5_w_glu, loss_target, m_ffn1_norm, m_ffn1_w_gu, m_ffn1_w_down, m_mix_norm, m_mem_norm, m_w_mem_kv, m_xq_norm, m_xk_norm, m_w_out, m_ffn2_norm, m_ffn2_w_gu, m_ffn2_w_down, m_sb_w_in, m_s5_w_in, m_s5_log_dt, m_s5_a_re, m_s5_a_im, m_s5_b_re, m_s5_b_im, m_s5_c_re, m_s5_c_im, m_s5_d, m_s5_w_glu, v_ffn1_norm, v_ffn1_w_gu, v_ffn1_w_down, v_mix_norm, v_mem_norm, v_w_mem_kv, v_xq_norm, v_xk_norm, v_w_out, v_ffn2_norm, v_ffn2_w_gu, v_ffn2_w_down, v_sb_w_in, v_s5_w_in, v_s5_log_dt, v_s5_a_re, v_s5_a_im, v_s5_b_re, v_s5_b_im, v_s5_c_re, v_s5_c_im, v_s5_d, v_s5_w_glu):
    given = dict(x=x, mem=mem, ffn1_norm=ffn1_norm, ffn1_w_gu=ffn1_w_gu, ffn1_w_down=ffn1_w_down, mix_norm=mix_norm, mem_norm=mem_norm, w_mem_kv=w_mem_kv, xq_norm=xq_norm, xk_norm=xk_norm, w_out=w_out, ffn2_norm=ffn2_norm, ffn2_w_gu=ffn2_w_gu, ffn2_w_down=ffn2_w_down, sb_w_in=sb_w_in, s5_w_in=s5_w_in, s5_log_dt=s5_log_dt, s5_a_re=s5_a_re, s5_a_im=s5_a_im, s5_b_re=s5_b_re, s5_b_im=s5_b_im, s5_c_re=s5_c_re, s5_c_im=s5_c_im, s5_d=s5_d, s5_w_glu=s5_w_glu, loss_target=loss_target, m_ffn1_norm=m_ffn1_norm, m_ffn1_w_gu=m_ffn1_w_gu, m_ffn1_w_down=m_ffn1_w_down, m_mix_norm=m_mix_norm, m_mem_norm=m_mem_norm, m_w_mem_kv=m_w_mem_kv, m_xq_norm=m_xq_norm, m_xk_norm=m_xk_norm, m_w_out=m_w_out, m_ffn2_norm=m_ffn2_norm, m_ffn2_w_gu=m_ffn2_w_gu, m_ffn2_w_down=m_ffn2_w_down, m_sb_w_in=m_sb_w_in, m_s5_w_in=m_s5_w_in, m_s5_log_dt=m_s5_log_dt, m_s5_a_re=m_s5_a_re, m_s5_a_im=m_s5_a_im, m_s5_b_re=m_s5_b_re, m_s5_b_im=m_s5_b_im, m_s5_c_re=m_s5_c_re, m_s5_c_im=m_s5_c_im, m_s5_d=m_s5_d, m_s5_w_glu=m_s5_w_glu, v_ffn1_norm=v_ffn1_norm, v_ffn1_w_gu=v_ffn1_w_gu, v_ffn1_w_down=v_ffn1_w_down, v_mix_norm=v_mix_norm, v_mem_norm=v_mem_norm, v_w_mem_kv=v_w_mem_kv, v_xq_norm=v_xq_norm, v_xk_norm=v_xk_norm, v_w_out=v_w_out, v_ffn2_norm=v_ffn2_norm, v_ffn2_w_gu=v_ffn2_w_gu, v_ffn2_w_down=v_ffn2_w_down, v_sb_w_in=v_sb_w_in, v_s5_w_in=v_s5_w_in, v_s5_log_dt=v_s5_log_dt, v_s5_a_re=v_s5_a_re, v_s5_a_im=v_s5_a_im, v_s5_b_re=v_s5_b_re, v_s5_b_im=v_s5_b_im, v_s5_c_re=v_s5_c_re, v_s5_c_im=v_s5_c_im, v_s5_d=v_s5_d, v_s5_w_glu=v_s5_w_glu)
    weights = {n: given[n] for n in TWIN_WEIGHTS}
    shared = {n: given[n] for n in SHARED_INPUTS}
    per_example = {n: given[n] for n in ['x', 'mem']}
    grad_fn = _jax.value_and_grad(_loss, argnums=(0, 1))

    def one_microbatch(ex, loss_target):
        ex = dict(ex)
        diff = ex.pop(TWIN_DIFF_INPUT)
        return grad_fn(weights, diff, {**shared, **ex}, loss_target)

    if N_MICROBATCH == 1:
        loss, (grad_w, grad_x) = one_microbatch(per_example, given["loss_target"])
    else:
        def body(carry, xs):
            loss_sum, grad_sum = carry
            l_k, (gw_k, gx_k) = one_microbatch(xs[0], xs[1])
            with _jax.named_scope("update"):
                return (loss_sum + l_k, _jax.tree.map(_jnp.add, grad_sum, gw_k)), gx_k

        init = (_jnp.zeros((), _jnp.float32), _jax.tree.map(_jnp.zeros_like, weights))
        (loss, grad_w), grad_x = _jax.lax.scan(body, init, (per_example, given["loss_target"]))
    with _jax.named_scope("update"):
        delta_w, new_m, new_v = {}, {}, {}
        for n in TWIN_WEIGHTS:
            delta_w[n], new_m[n], new_v[n] = _adamw(weights[n], grad_w[n], given["m_" + n], given["v_" + n])
    return (loss, grad_x, *[grad_w[n] for n in TWIN_WEIGHTS], *[delta_w[n] for n in TWIN_WEIGHTS],
            *[new_m[n] for n in TWIN_WEIGHTS], *[new_v[n] for n in TWIN_WEIGHTS])
```

```python
import math

import jax
import jax.numpy as jnp
from jax import lax
from jax.experimental import pallas as pl
from jax.experimental.pallas import tpu as pltpu

F32 = jnp.float32
BF16 = jnp.bfloat16

HEAD_DIM = 128
S5_GROUP = 16
S5_STATE = 64
EPS = 1e-6
ADAM_LR = 0.001
ADAM_B1 = 0.9
ADAM_B2 = 0.999
ADAM_EPS = 1e-08
ADAM_WD = 0.01
ADAM_STEP = 10

LANES = 128
V7X_VMEM_LIMIT = 56 * 1024 * 1024
N_CHIPS = 4
MESH = pl.DeviceIdType.MESH


def _params(sem):
    return pltpu.CompilerParams(dimension_semantics=sem, vmem_limit_bytes=V7X_VMEM_LIMIT)


def _tile(n, want, mult=LANES):
    if n <= want:
        return n
    t = (want // mult) * mult
    while t > mult and n % t:
        t -= mult
    assert n % t == 0, (n, want, mult)
    return t


_DIMS = {
    "nn": (((1,), (0,)), ((), ())),
    "nt": (((1,), (1,)), ((), ())),
    "tn": (((0,), (0,)), ((), ())),
}


def _matmul(name, mode, a, b, *, grid, a_spec, b_spec, out_shape, out_spec, acc_shape,
            extras=(), extra_specs=(), epilogue=None):
    nk = grid[2]
    ne = len(extras)
    multi = isinstance(out_shape, (tuple, list))
    outs = tuple(out_shape) if multi else (out_shape,)
    ospecs = tuple(out_spec) if multi else (out_spec,)
    no = len(outs)

    def body(a_ref, b_ref, *rest):
        ex = rest[:ne]
        out_refs = rest[ne:ne + no]
        part = lax.dot_general(a_ref[...].astype(BF16), b_ref[...].astype(BF16), _DIMS[mode],
                               preferred_element_type=F32)

        def finish(acc):
            if epilogue is None:
                out_refs[0][...] = acc.astype(out_refs[0].dtype)
            else:
                epilogue(acc, ex, out_refs)

        if nk == 1:
            finish(part)
        else:
            acc_ref = rest[-1]
            k = pl.program_id(2)

            @pl.when(k == 0)
            def _():
                acc_ref[...] = part

            @pl.when(k > 0)
            def _():
                acc_ref[...] += part

            @pl.when(k == nk - 1)
            def _():
                finish(acc_ref[...])

    res = pl.pallas_call(
        body, name=name, grid=grid,
        in_specs=[a_spec, b_spec, *extra_specs],
        out_specs=ospecs if multi else ospecs[0],
        out_shape=outs if multi else outs[0],
        scratch_shapes=[pltpu.VMEM(acc_shape, F32)] if nk > 1 else [],
        compiler_params=_params(("parallel", "parallel", "arbitrary")),
    )(a, b, *extras)
    return res


def _sds(shape, dtype):
    return jax.ShapeDtypeStruct(tuple(shape), dtype)


def _rmsnorm_fwd(name, x, gain):
    m, d = x.shape
    tm = _tile(m, 512, 8)

    def body(x_ref, g_ref, o_ref):
        xv = x_ref[...]
        r = lax.rsqrt(jnp.mean(xv * xv, axis=-1, keepdims=True) + EPS)
        o_ref[...] = (xv * r * g_ref[...]).astype(o_ref.dtype)

    return pl.pallas_call(
        body, name=name, grid=(m // tm,),
        in_specs=[pl.BlockSpec((tm, d), lambda i: (i, 0)), pl.BlockSpec((1, d), lambda i: (0, 0))],
        out_specs=pl.BlockSpec((tm, d), lambda i: (i, 0)),
        out_shape=_sds((m, d), BF16),
        compiler_params=_params(("parallel",)),
    )(x, gain)


def _rmsnorm_bwd(name, x, dh, gain, dres=None):
    m, d = x.shape
    tm = _tile(m, 256, 8)
    nsteps = m // tm
    with_dx = dres is not None

    def body(*refs):
        if with_dx:
            x_ref, dh_ref, g_ref, dres_ref, dx_ref, dxb_ref, dg_ref, acc_ref = refs
        else:
            x_ref, dh_ref, g_ref, dg_ref, acc_ref = refs
        i = pl.program_id(0)
        xv = x_ref[...]
        r = lax.rsqrt(jnp.mean(xv * xv, axis=-1, keepdims=True) + EPS)
        xh = xv * r
        dhv = dh_ref[...].astype(F32)
        contrib = (dhv * xh).reshape(tm // 8, 8, d).sum(axis=0)

        @pl.when(i == 0)
        def _():
            acc_ref[...] = contrib

        @pl.when(i > 0)
        def _():
            acc_ref[...] += contrib

        @pl.when(i == nsteps - 1)
        def _():
            dg_ref[...] = jnp.sum(acc_ref[...], axis=0, keepdims=True)

        if with_dx:
            dxh = dhv * g_ref[...]
            dx = r * (dxh - xh * jnp.mean(dxh * xh, axis=-1, keepdims=True)) + dres_ref[...]
            dx_ref[...] = dx
            dxb_ref[...] = dx.astype(BF16)

    row = pl.BlockSpec((tm, d), lambda i: (i, 0))
    vec = pl.BlockSpec((1, d), lambda i: (0, 0))
    if with_dx:
        return pl.pallas_call(
            body, name=name, grid=(nsteps,),
            in_specs=[row, row, vec, row], out_specs=(row, row, vec),
            out_shape=(_sds((m, d), F32), _sds((m, d), BF16), _sds((1, d), F32)),
            scratch_shapes=[pltpu.VMEM((8, d), F32)],
            compiler_params=_params(("arbitrary",)),
        )(x, dh, gain, dres)
    return pl.pallas_call(
        body, name=name, grid=(nsteps,),
        in_specs=[row, row, vec], out_specs=vec,
        out_shape=_sds((1, d), F32),
        scratch_shapes=[pltpu.VMEM((8, d), F32)],
        compiler_params=_params(("arbitrary",)),
    )(x, dh, gain)


def _ffn_fwd(tag, x, gain, wgu, wdown):
    m, d = x.shape
    ns = wgu.shape[2]
    f = 2 * ns
    h = _rmsnorm_fwd(f"{tag}_norm", x, gain)
    tm = _tile(m, 1024, 8)
    tn = _tile(ns, 256)
    nb = ns // tn

    def gu_epilogue_body(h_ref, wg_ref, wu_ref, act_ref, gu_ref):
        hv = h_ref[...]
        g = jnp.dot(hv, wg_ref[...], preferred_element_type=F32)
        u = jnp.dot(hv, wu_ref[...], preferred_element_type=F32)
        act_ref[...] = (g * jax.nn.sigmoid(g) * u).astype(BF16)
        gu_ref[0] = g.astype(BF16)
        gu_ref[1] = u.astype(BF16)

    act, gu = pl.pallas_call(
        gu_epilogue_body, name=f"{tag}_gu", grid=(m // tm, 2 * nb),
        in_specs=[pl.BlockSpec((tm, d), lambda i, j: (i, 0)),
                  pl.BlockSpec((None, d, tn), lambda i, j: (j // nb, 0, j % nb)),
                  pl.BlockSpec((None, d, tn), lambda i, j: (2 + j // nb, 0, j % nb))],
        out_specs=(pl.BlockSpec((tm, tn), lambda i, j: (i, j)),
                   pl.BlockSpec((2, tm, tn), lambda i, j: (0, i, j))),
        out_shape=(_sds((m, f), BF16), _sds((2, m, f), BF16)),
        compiler_params=_params(("parallel", "parallel")),
    )(h, wgu, wgu)

    tk = _tile(f, 1408)
    tnd = _tile(d, 512)

    def down_epilogue(acc, ex, outs):
        outs[0][...] = ex[0][...] + 0.5 * acc

    y = _matmul(
        f"{tag}_down", "nn", act, wdown, grid=(m // tm, d // tnd, f // tk),
        a_spec=pl.BlockSpec((tm, tk), lambda i, j, k: (i, k)),
        b_spec=pl.BlockSpec((tk, tnd), lambda i, j, k: (k, j)),
        out_shape=_sds((m, d), F32), out_spec=pl.BlockSpec((tm, tnd), lambda i, j, k: (i, j)),
        acc_shape=(tm, tnd), extras=(x,), extra_specs=(pl.BlockSpec((tm, tnd), lambda i, j, k: (i, j)),),
        epilogue=down_epilogue)
    return y, (x, h, act, gu)


def _ffn_bwd(tag, saved, gain, wgu, wdown, dy, dyb):
    x, h, act, gu = saved
    m, d = x.shape
    ns = wgu.shape[2]
    f = 2 * ns
    tm = _tile(m, 1024, 8)
    tn = _tile(ns, 256)
    nb = ns // tn

    def dact_epilogue(acc, ex, outs):
        g = ex[0][0].astype(F32)
        u = ex[0][1].astype(F32)
        da = 0.5 * acc
        s = jax.nn.sigmoid(g)
        outs[0][0] = (da * u * s * (1.0 + g * (1.0 - s))).astype(BF16)
        outs[0][1] = (da * g * s).astype(BF16)

    dgu = _matmul(
        f"{tag}_dact", "nt", dyb, wdown, grid=(m // tm, 2 * nb, 1),
        a_spec=pl.BlockSpec((tm, d), lambda i, j, k: (i, 0)),
        b_spec=pl.BlockSpec((tn, d), lambda i, j, k: (j, 0)),
        out_shape=_sds((2, m, f), BF16), out_spec=pl.BlockSpec((2, tm, tn), lambda i, j, k: (0, i, j)),
        acc_shape=(tm, tn), extras=(gu,), extra_specs=(pl.BlockSpec((2, tm, tn), lambda i, j, k: (0, i, j)),),
        epilogue=dact_epilogue)

    tkm = _tile(m, 512, 8)
    tf = _tile(f, 1408)
    tnd = _tile(d, 1024)

    def half_epilogue(acc, ex, outs):
        outs[0][...] = 0.5 * acc

    dwdown = _matmul(
        f"{tag}_dwdown", "tn", act, dyb, grid=(f // tf, d // tnd, m // tkm),
        a_spec=pl.BlockSpec((tkm, tf), lambda i, j, k: (k, i)),
        b_spec=pl.BlockSpec((tkm, tnd), lambda i, j, k: (k, j)),
        out_shape=_sds((f, d), F32), out_spec=pl.BlockSpec((tf, tnd), lambda i, j, k: (i, j)),
        acc_shape=(tf, tnd), epilogue=half_epilogue)

    td = _tile(d, 512)
    dwgu = _matmul(
        f"{tag}_dwgu", "tn", h, dgu, grid=(d // td, 4, m // tkm),
        a_spec=pl.BlockSpec((tkm, td), lambda i, j, k: (k, i)),
        b_spec=pl.BlockSpec((None, tkm, ns), lambda i, j, k: (j // 2, k, j % 2)),
        out_shape=_sds((4, d, ns), F32), out_spec=pl.BlockSpec((None, td, ns), lambda i, j, k: (j, i, 0)),
        acc_shape=(td, ns))

    tmh = _tile(m, 512, 8)
    nbh = f // tn
    dh = _matmul(
        f"{tag}_dh", "nt", dgu, wgu, grid=(m // tmh, 1, 2 * nbh),
        a_spec=pl.BlockSpec((None, tmh, tn), lambda i, j, k: (k // nbh, i, k % nbh)),
        b_spec=pl.BlockSpec((None, d, tn), lambda i, j, k: (k // nb, 0, k % nb)),
        out_shape=_sds((m, d), F32), out_spec=pl.BlockSpec((tmh, d), lambda i, j, k: (i, 0)),
        acc_shape=(tmh, d))

    dx, dxb, dgain = _rmsnorm_bwd(f"{tag}_dnorm", x, dh, gain, dy)
    return dx, dxb, dgain, dwgu, dwdown


SB_KEY_BLOCK = 128
_NT = (((1,), (1,)), ((), ()))
_TN = (((0,), (0,)), ((), ()))


def _split3(x):
    hi = x.astype(BF16)
    r1 = x - hi.astype(F32)
    mid = r1.astype(BF16)
    lo = (r1 - mid.astype(F32)).astype(BF16)
    return hi, mid, lo


def _dot3(parts, mat):
    return (jnp.dot(parts[0], mat, preferred_element_type=F32) + jnp.dot(parts[1], mat, preferred_element_type=F32)
            + jnp.dot(parts[2], mat, preferred_element_type=F32))


def _sb_logits(q, k, scale):
    z = lax.dot_general(q, k, _NT, preferred_element_type=F32) * scale
    lp = jnp.minimum(z, 0.0) - jnp.log1p(jnp.exp(-jnp.abs(z)))
    return z, lp, lp - z


def _sb_fwd(name, qkv, n_heads):
    length = qkv.shape[0]
    kb = SB_KEY_BLOCK
    tq = _tile(length, 256, kb)
    r = tq // kb
    scale = 1.0 / math.sqrt(HEAD_DIM)

    def body(q_ref, k_ref, v_ref, o_ref, tot_ref):
        qi = pl.program_id(1)
        q = q_ref[...].astype(BF16)
        row = qi * tq + lax.broadcasted_iota(jnp.int32, (tq, kb), 0)
        col = lax.broadcasted_iota(jnp.int32, (tq, kb), 1)
        jj = lax.broadcasted_iota(jnp.int32, (kb, kb), 0)
        ss = lax.broadcasted_iota(jnp.int32, (kb, kb), 1)
        later = (jj > ss).astype(BF16)
        ones = jnp.ones((kb, kb), BF16)

        def block(kbi, carry, masked):
            c, acc = carry
            ks = pl.multiple_of(kbi * kb, kb)
            k = k_ref[pl.ds(ks, kb), :].astype(BF16)
            v = v_ref[pl.ds(ks, kb), :].astype(BF16)
            _, lp, ln = _sb_logits(q, k, scale)
            if masked:
                valid = (ks + col) < row
                ln = jnp.where(valid, ln, 0.0)
            parts = _split3(ln)
            w = jnp.exp(lp + c + _dot3(parts, later))
            if masked:
                w = jnp.where(valid, w, 0.0)
            acc = acc + jnp.dot(w.astype(BF16), v, preferred_element_type=F32)
            return c + _dot3(parts, ones), acc

        carry = (jnp.zeros((tq, kb), F32), jnp.zeros((tq, HEAD_DIM), F32))
        for dgl in range(r - 1, -1, -1):
            carry = block(qi * r + dgl, carry, True)
        carry = lax.fori_loop(0, qi * r, lambda i, cr: block(qi * r - 1 - i, cr, False), carry)
        o_ref[...] = carry[1]
        tot_ref[...] = carry[0]

    h = n_heads
    qblk = pl.BlockSpec((tq, HEAD_DIM), lambda hh, i: (i, hh))
    return pl.pallas_call(
        body, name=name, grid=(h, length // tq),
        in_specs=[qblk,
                  pl.BlockSpec((length, HEAD_DIM), lambda hh, i: (0, h + hh)),
                  pl.BlockSpec((length, HEAD_DIM), lambda hh, i: (0, 2 * h + hh))],
        out_specs=(qblk, qblk),
        out_shape=(_sds((length, h * HEAD_DIM), F32), _sds((length, h * HEAD_DIM), F32)),
        compiler_params=_params(("parallel", "arbitrary")),
    )(qkv, qkv, qkv)


def _sb_bwd(name, qkv, tot, do, n_heads):
    length = qkv.shape[0]
    kb = SB_KEY_BLOCK
    tq = _tile(length, 256, kb)
    r = tq // kb
    scale = 1.0 / math.sqrt(HEAD_DIM)

    def body(q_ref, k_ref, v_ref, tot_ref, do_ref, dq_ref, dk_ref, dv_ref):
        qi = pl.program_id(1)

        @pl.when(qi == 0)
        def _():
            dk_ref[...] = jnp.zeros_like(dk_ref)
            dv_ref[...] = jnp.zeros_like(dv_ref)

        q = q_ref[...].astype(BF16)
        dob = do_ref[...].astype(BF16)
        tot = tot_ref[...]
        row = qi * tq + lax.broadcasted_iota(jnp.int32, (tq, kb), 0)
        col = lax.broadcasted_iota(jnp.int32, (tq, kb), 1)
        jj = lax.broadcasted_iota(jnp.int32, (kb, kb), 0)
        ss = lax.broadcasted_iota(jnp.int32, (kb, kb), 1)
        upto = (jj <= ss).astype(BF16)
        before = (jj < ss).astype(BF16)
        ones = jnp.ones((kb, kb), BF16)

        def block(kbi, carry, masked):
            pl_, pe, dq = carry
            ks = pl.multiple_of(kbi * kb, kb)
            k = k_ref[pl.ds(ks, kb), :].astype(BF16)
            v = v_ref[pl.ds(ks, kb), :].astype(BF16)
            _, lp, ln_raw = _sb_logits(q, k, scale)
            ln = ln_raw
            if masked:
                valid = (ks + col) < row
                ln = jnp.where(valid, ln_raw, 0.0)
            parts = _split3(ln)
            w = jnp.exp(lp + (tot - (pl_ + _dot3(parts, upto))))
            if masked:
                w = jnp.where(valid, w, 0.0)
            e = w * lax.dot_general(dob, v, _NT, preferred_element_type=F32)
            dv_ref[pl.ds(ks, kb), :] += lax.dot_general(w.astype(BF16), dob, _TN, preferred_element_type=F32)
            eparts = _split3(e)
            dz = e * jnp.exp(ln_raw) - jnp.exp(lp) * (pe + _dot3(eparts, before))
            if masked:
                dz = jnp.where(valid, dz, 0.0)
            dzb = (dz * scale).astype(BF16)
            dq = dq + jnp.dot(dzb, k, preferred_element_type=F32)
            dk_ref[pl.ds(ks, kb), :] += lax.dot_general(dzb, q, _TN, preferred_element_type=F32)
            return pl_ + _dot3(parts, ones), pe + _dot3(eparts, ones), dq

        carry = (jnp.zeros((tq, kb), F32), jnp.zeros((tq, kb), F32), jnp.zeros((tq, HEAD_DIM), F32))
        carry = lax.fori_loop(0, qi * r, lambda i, cr: block(i, cr, False), carry)
        for dgl in range(r):
            carry = block(qi * r + dgl, carry, True)
        dq_ref[...] = carry[2]

    h = n_heads
    qblk = pl.BlockSpec((tq, HEAD_DIM), lambda hh, i: (i, hh))
    full = pl.BlockSpec((length, HEAD_DIM), lambda hh, i: (0, hh))
    out = _sds((length, h * HEAD_DIM), F32)
    return pl.pallas_call(
        body, name=name, grid=(h, length // tq),
        in_specs=[qblk,
                  pl.BlockSpec((length, HEAD_DIM), lambda hh, i: (0, h + hh)),
                  pl.BlockSpec((length, HEAD_DIM), lambda hh, i: (0, 2 * h + hh)),
                  qblk, qblk],
        out_specs=(qblk, full, full), out_shape=(out, out, out),
        compiler_params=_params(("parallel", "arbitrary")),
    )(qkv, qkv, qkv, tot, do)


def _head_rms(xh):
    r = lax.rsqrt(jnp.mean(xh * xh, axis=-1, keepdims=True) + EPS)
    return xh * r, r


def _mem_fwd(name, qsrc, qcol0, kv, gq, gk):
    length = qsrc.shape[0]
    mm, mw2 = kv.shape
    mw = mw2 // 2
    nh = mw // HEAD_DIM
    tq = _tile(length, 512, 8)
    inv = 1.0 / math.sqrt(HEAD_DIM)
    assert qcol0 % mw == 0

    def body(q_ref, kv_ref, gq_ref, gk_ref, o_ref):
        for hh in range(nh):
            sl = slice(hh * HEAD_DIM, (hh + 1) * HEAD_DIM)
            qn = _head_rms(q_ref[:, sl])[0] * gq_ref[...]
            kn = _head_rms(kv_ref[:, sl])[0] * gk_ref[...]
            vh = kv_ref[:, mw + hh * HEAD_DIM:mw + (hh + 1) * HEAD_DIM].astype(BF16)
            s = lax.dot_general(qn.astype(BF16), kn.astype(BF16), _NT, preferred_element_type=F32) * inv
            p = jnp.exp(s - jnp.max(s, axis=-1, keepdims=True))
            p = p / jnp.sum(p, axis=-1, keepdims=True)
            o_ref[:, sl] = jnp.dot(p.astype(BF16), vh, preferred_element_type=F32)

    vec = pl.BlockSpec((1, HEAD_DIM), lambda i: (0, 0))
    return pl.pallas_call(
        body, name=name, grid=(length // tq,),
        in_specs=[pl.BlockSpec((tq, mw), lambda i: (i, qcol0 // mw)),
                  pl.BlockSpec((mm, mw2), lambda i: (0, 0)), vec, vec],
        out_specs=pl.BlockSpec((tq, mw), lambda i: (i, 0)),
        out_shape=_sds((length, mw), F32),
        compiler_params=_params(("parallel",)),
    )(qsrc, kv, gq, gk)


def _mem_bwd(name, qsrc, qcol0, kv, gq, gk, dsrc, docol0):
    length = qsrc.shape[0]
    mm, mw2 = kv.shape
    mw = mw2 // 2
    nh = mw // HEAD_DIM
    tq = _tile(length, 512, 8)
    nsteps = length // tq
    inv = 1.0 / math.sqrt(HEAD_DIM)

    def body(q_ref, kv_ref, gq_ref, gk_ref, do_ref, dq_ref, dkv_ref, dgq_ref, dgk_ref):
        i = pl.program_id(0)

        @pl.when(i == 0)
        def _():
            dkv_ref[...] = jnp.zeros_like(dkv_ref)
            dgq_ref[...] = jnp.zeros_like(dgq_ref)

        gqv = gq_ref[...]
        gkv = gk_ref[...]
        for hh in range(nh):
            sl = slice(hh * HEAD_DIM, (hh + 1) * HEAD_DIM)
            slv = slice(mw + hh * HEAD_DIM, mw + (hh + 1) * HEAD_DIM)
            qhat, rq = _head_rms(q_ref[:, sl])
            qn = (qhat * gqv).astype(BF16)
            kn = (_head_rms(kv_ref[:, sl])[0] * gkv).astype(BF16)
            vh = kv_ref[:, slv].astype(BF16)
            dob = do_ref[:, sl].astype(BF16)
            s = lax.dot_general(qn, kn, _NT, preferred_element_type=F32) * inv
            p = jnp.exp(s - jnp.max(s, axis=-1, keepdims=True))
            p = p / jnp.sum(p, axis=-1, keepdims=True)
            dp = lax.dot_general(dob, vh, _NT, preferred_element_type=F32)
            ds = (p * (dp - jnp.sum(dp * p, axis=-1, keepdims=True)) * inv).astype(BF16)
            dqn = jnp.dot(ds, kn, preferred_element_type=F32)
            dkv_ref[:, sl] += lax.dot_general(ds, qn, _TN, preferred_element_type=F32)
            dkv_ref[:, slv] += lax.dot_general(p.astype(BF16), dob, _TN, preferred_element_type=F32)
            dgq_ref[...] += jnp.sum(dqn * qhat, axis=0, keepdims=True)
            dqh = dqn * gqv
            dq_ref[:, sl] = rq * (dqh - qhat * jnp.mean(dqh * qhat, axis=-1, keepdims=True))

        @pl.when(i == nsteps - 1)
        def _():
            dgk = jnp.zeros((1, HEAD_DIM), F32)
            for hh in range(nh):
                sl = slice(hh * HEAD_DIM, (hh + 1) * HEAD_DIM)
                khat, rk = _head_rms(kv_ref[:, sl])
                dkn = dkv_ref[:, sl]
                dgk = dgk + jnp.sum(dkn * khat, axis=0, keepdims=True)
                dkh = dkn * gkv
                dkv_ref[:, sl] = rk * (dkh - khat * jnp.mean(dkh * khat, axis=-1, keepdims=True))
            dgk_ref[...] = dgk

    vec = pl.BlockSpec((1, HEAD_DIM), lambda i: (0, 0))
    kvs = pl.BlockSpec((mm, mw2), lambda i: (0, 0))
    blk = pl.BlockSpec((tq, mw), lambda i: (i, 0))
    return pl.pallas_call(
        body, name=name, grid=(nsteps,),
        in_specs=[pl.BlockSpec((tq, mw), lambda i: (i, qcol0 // mw)), kvs, vec, vec,
                  pl.BlockSpec((tq, mw), lambda i: (i, docol0 // mw))],
        out_specs=(blk, kvs, vec, vec),
        out_shape=(_sds((length, mw), F32), _sds((mm, mw2), F32), _sds((1, HEAD_DIM), F32), _sds((1, HEAD_DIM), F32)),
        compiler_params=_params(("arbitrary",)),
    )(qsrc, kv, gq, gk, dsrc)


S5_TILE_GROUPS = LANES // S5_GROUP
S5_TILE_STATES = S5_TILE_GROUPS * S5_STATE
S5_SCAN_ROWS = 64
_HI = lax.Precision.HIGHEST


def _dotf(a, b, dims=None):
    if dims is None:
        return jnp.dot(a, b, precision=_HI, preferred_element_type=F32)
    return lax.dot_general(a, b, dims, precision=_HI, preferred_element_type=F32)


def _s5_prep(log_dt, a_re, a_im, b_re, b_im, c_re, c_im):
    g, n = a_re.shape
    nt = g // S5_TILE_GROUPS
    dt = jnp.exp(log_dt)[:, None]
    mag = jnp.exp(a_re * dt)
    ab_re = mag * jnp.cos(a_im * dt)
    ab_im = mag * jnp.sin(a_im * dt)
    den = a_re * a_re + a_im * a_im
    num_re = ab_re - 1.0
    co_re = (num_re * a_re + ab_im * a_im) / den
    co_im = (ab_im * a_re - num_re * a_im) / den
    bb_re = co_re[..., None] * b_re - co_im[..., None] * b_im
    bb_im = co_re[..., None] * b_im + co_im[..., None] * b_re
    eye = jnp.eye(S5_TILE_GROUPS, dtype=F32)

    def blk_b(bb):
        t = bb.reshape(nt, S5_TILE_GROUPS, n, S5_GROUP).transpose(0, 1, 3, 2)
        return jnp.einsum("jgcn,gh->jgchn", t, eye).reshape(nt, LANES, S5_TILE_STATES)

    def blk_c(cc):
        t = cc.reshape(nt, S5_TILE_GROUPS, S5_GROUP, n).transpose(0, 1, 3, 2)
        return jnp.einsum("jgnc,gh->jgnhc", t, eye).reshape(nt, S5_TILE_STATES, LANES)

    return (ab_re.reshape(1, g * n), ab_im.reshape(1, g * n), blk_b(bb_re), blk_b(bb_im), blk_c(c_re), blk_c(c_im))


def _s5_bu(name, usrc, bblk_re, bblk_im):
    length = usrc.shape[0]
    nt = bblk_re.shape[0]
    tm = _tile(length, 512, 8)

    def body(u_ref, br_ref, bi_ref, or_ref, oi_ref):
        u = u_ref[...]
        or_ref[...] = _dotf(u, br_ref[...])
        oi_ref[...] = _dotf(u, bi_ref[...])

    bspec = pl.BlockSpec((None, LANES, S5_TILE_STATES), lambda i, j: (j, 0, 0))
    ospec = pl.BlockSpec((tm, S5_TILE_STATES), lambda i, j: (i, j))
    out = _sds((length, nt * S5_TILE_STATES), F32)
    return pl.pallas_call(
        body, name=name, grid=(length // tm, nt),
        in_specs=[pl.BlockSpec((tm, LANES), lambda i, j: (i, j)), bspec, bspec],
        out_specs=(ospec, ospec), out_shape=(out, out),
        compiler_params=_params(("parallel", "parallel")),
    )(usrc, bblk_re, bblk_im)


def _scan_rows(hr, hi, ar, ai, reverse):
    t = hr.shape[0]
    rows = lax.broadcasted_iota(jnp.int32, hr.shape, 0)
    d = 1
    while d < t:
        if reverse:
            sr = jnp.where(rows < t - d, pltpu.roll(hr, t - d, 0), 0.0)
            si = jnp.where(rows < t - d, pltpu.roll(hi, t - d, 0), 0.0)
        else:
            sr = jnp.where(rows >= d, pltpu.roll(hr, d, 0), 0.0)
            si = jnp.where(rows >= d, pltpu.roll(hi, d, 0), 0.0)
        hr, hi = hr + ar * sr - ai * si, hi + ar * si + ai * sr
        ar, ai = ar * ar - ai * ai, 2.0 * ar * ai
        d *= 2
    return hr, hi


def _s5_scan(name, x_re, x_im, a_re, a_im, reverse=False, h=None, bu=None):
    length, width = x_re.shape
    t = _tile(length, S5_SCAN_ROWS, 8)
    nsteps = length // t
    with_sum = h is not None

    def body(*refs):
        if with_sum:
            (xr_ref, xi_ref, ar_ref, ai_ref, hr_ref, hi_ref, br_ref, bi_ref,
             or_ref, oi_ref, sr_ref, si_ref, cr, ci, pr, pi, accr, acci) = refs
        else:
            xr_ref, xi_ref, ar_ref, ai_ref, or_ref, oi_ref, cr, ci, pr, pi = refs
        step = pl.program_id(0)
        ar = ar_ref[...]
        ai = ai_ref[...]
        edge = 0 if reverse else t - 1
        last = t - 1 if reverse else 0

        @pl.when(step == 0)
        def _():
            cr[...] = jnp.zeros_like(cr)
            ci[...] = jnp.zeros_like(ci)
            rows = lax.broadcasted_iota(jnp.int32, (t, width), 0)
            seed_r = jnp.where(rows == last, ar, 0.0)
            seed_i = jnp.where(rows == last, ai, 0.0)
            p_r, p_i = _scan_rows(seed_r, seed_i, ar, ai, reverse)
            pr[...] = p_r
            pi[...] = p_i
            if with_sum:
                accr[...] = jnp.zeros_like(accr)
                acci[...] = jnp.zeros_like(acci)

        hr, hi = _scan_rows(xr_ref[...], xi_ref[...], ar, ai, reverse)
        c_r = cr[...]
        c_i = ci[...]
        p_r = pr[...]
        p_i = pi[...]
        hr = hr + p_r * c_r - p_i * c_i
        hi = hi + p_r * c_i + p_i * c_r
        or_ref[...] = hr
        oi_ref[...] = hi
        cr[...] = hr[edge:edge + 1, :]
        ci[...] = hi[edge:edge + 1, :]
        if with_sum:
            wr = hr_ref[...] - br_ref[...]
            wi = hi_ref[...] - bi_ref[...]
            accr[...] += (wr * hr + wi * hi).reshape(t // 8, 8, width).sum(axis=0)
            acci[...] += (wr * hi - wi * hr).reshape(t // 8, 8, width).sum(axis=0)

            @pl.when(step == nsteps - 1)
            def _():
                sr_ref[...] = jnp.sum(accr[...], axis=0, keepdims=True)
                si_ref[...] = jnp.sum(acci[...], axis=0, keepdims=True)

    if reverse:
        blk = pl.BlockSpec((t, width), lambda s: (nsteps - 1 - s, 0))
    else:
        blk = pl.BlockSpec((t, width), lambda s: (s, 0))
    vec = pl.BlockSpec((1, width), lambda s: (0, 0))
    full = _sds((length, width), F32)
    row = _sds((1, width), F32)
    scratch = [pltpu.VMEM((1, width), F32), pltpu.VMEM((1, width), F32),
               pltpu.VMEM((t, width), F32), pltpu.VMEM((t, width), F32)]
    if with_sum:
        return pl.pallas_call(
            body, name=name, grid=(nsteps,),
            in_specs=[blk, blk, vec, vec, blk, blk, blk, blk],
            out_specs=(blk, blk, vec, vec), out_shape=(full, full, row, row),
            scratch_shapes=scratch + [pltpu.VMEM((8, width), F32), pltpu.VMEM((8, width), F32)],
            compiler_params=_params(("arbitrary",)),
        )(x_re, x_im, a_re, a_im, h[0], h[1], bu[0], bu[1])
    return pl.pallas_call(
        body, name=name, grid=(nsteps,),
        in_specs=[blk, blk, vec, vec], out_specs=(blk, blk), out_shape=(full, full),
        scratch_shapes=scratch,
        compiler_params=_params(("arbitrary",)),
    )(x_re, x_im, a_re, a_im)


_GELU_C = math.sqrt(2.0 / math.pi)


def _gelu(y):
    return 0.5 * y * (1.0 + jnp.tanh(_GELU_C * (y + 0.044715 * y * y * y)))


def _gelu_grad(y):
    th = jnp.tanh(_GELU_C * (y + 0.044715 * y * y * y))
    return 0.5 * (1.0 + th) + 0.5 * y * (1.0 - th * th) * _GELU_C * (1.0 + 3 * 0.044715 * y * y)


def _s5_out(name, h_re, h_im, cblk_re, cblk_im, usrc, dskip):
    length = h_re.shape[0]
    nt = cblk_re.shape[0]
    tm = _tile(length, 512, 8)

    def body(hr_ref, hi_ref, cr_ref, ci_ref, u_ref, d_ref, y_ref, y2_ref):
        y = _dotf(hr_ref[...], cr_ref[...]) - _dotf(hi_ref[...], ci_ref[...]) + d_ref[...] * u_ref[...]
        y_ref[...] = y
        y2_ref[...] = _gelu(y)

    hspec = pl.BlockSpec((tm, S5_TILE_STATES), lambda i, j: (i, j))
    cspec = pl.BlockSpec((None, S5_TILE_STATES, LANES), lambda i, j: (j, 0, 0))
    uspec = pl.BlockSpec((tm, LANES), lambda i, j: (i, j))
    out = _sds((length, nt * LANES), F32)
    return pl.pallas_call(
        body, name=name, grid=(length // tm, nt),
        in_specs=[hspec, hspec, cspec, cspec, uspec, pl.BlockSpec((1, LANES), lambda i, j: (0, j))],
        out_specs=(uspec, uspec), out_shape=(out, out),
        compiler_params=_params(("parallel", "parallel")),
    )(h_re, h_im, cblk_re, cblk_im, usrc, dskip)


def _s5_fwd(tag, usrc, prep, dskip, wglu):
    ab_re, ab_im, bb_re, bb_im, cb_re, cb_im = prep
    length = usrc.shape[0]
    tw = wglu.shape[0]
    bu = _s5_bu(f"{tag}_bu", usrc, bb_re, bb_im)
    hs = _s5_scan(f"{tag}_scan", bu[0], bu[1], ab_re, ab_im)
    y, y2 = _s5_out(f"{tag}_out", hs[0], hs[1], cb_re, cb_im, usrc, dskip)
    tm = _tile(length, 1024, 8)
    tn = _tile(tw, 512)

    def glu_epilogue(acc, ex, outs):
        outs[0][...] = acc
        outs[1][...] = ex[0][...] * jax.nn.sigmoid(acc)

    ospec = pl.BlockSpec((tm, tn), lambda i, j, k: (i, j))
    gl, tok = _matmul(
        f"{tag}_glu", "nn", y2, wglu, grid=(length // tm, tw // tn, 1),
        a_spec=pl.BlockSpec((tm, tw), lambda i, j, k: (i, 0)),
        b_spec=pl.BlockSpec((tw, tn), lambda i, j, k: (0, j)),
        out_shape=(_sds((length, tw), F32), _sds((length, tw), F32)), out_spec=(ospec, ospec),
        acc_shape=(tm, tn), extras=(y2,), extra_specs=(ospec,), epilogue=glu_epilogue)
    return tok, (bu, hs, y, y2, gl)


def _s5_bwd(tag, usrc, prep, dskip, wglu, saved, dsrc):
    ab_re, ab_im, bb_re, bb_im, cb_re, cb_im = prep
    bu, hs, y, y2, gl = saved
    length = usrc.shape[0]
    tw = wglu.shape[0]
    nt = bb_re.shape[0]

    tme = _tile(length, 512, 8)

    def gate_body(dt_ref, y2_ref, gl_ref, dgl_ref, dy2_ref):
        s = jax.nn.sigmoid(gl_ref[...])
        dt = dt_ref[...]
        dgl_ref[...] = (dt * y2_ref[...] * s * (1.0 - s)).astype(BF16)
        dy2_ref[...] = dt * s

    espec = pl.BlockSpec((tme, tw), lambda i: (i, 0))
    dgl, dy2a = pl.pallas_call(
        gate_body, name=f"{tag}_dgate", grid=(length // tme,),
        in_specs=[espec, espec, espec], out_specs=(espec, espec),
        out_shape=(_sds((length, tw), BF16), _sds((length, tw), F32)),
        compiler_params=_params(("parallel",)),
    )(dsrc, y2, gl)

    tkm = _tile(length, 512, 8)
    dwglu = _matmul(
        f"{tag}_dwglu", "tn", y2, dgl, grid=(1, 1, length // tkm),
        a_spec=pl.BlockSpec((tkm, tw), lambda i, j, k: (k, 0)),
        b_spec=pl.BlockSpec((tkm, tw), lambda i, j, k: (k, 0)),
        out_shape=_sds((tw, tw), F32), out_spec=pl.BlockSpec((tw, tw), lambda i, j, k: (0, 0)),
        acc_shape=(tw, tw))

    tm = _tile(length, 1024, 8)
    tn = _tile(tw, 512)

    def dy_epilogue(acc, ex, outs):
        outs[0][...] = (ex[0][...] + acc) * _gelu_grad(ex[1][...])

    ospec = pl.BlockSpec((tm, tn), lambda i, j, k: (i, j))
    dy = _matmul(
        f"{tag}_dy", "nt", dgl, wglu, grid=(length // tm, tw // tn, 1),
        a_spec=pl.BlockSpec((tm, tw), lambda i, j, k: (i, 0)),
        b_spec=pl.BlockSpec((tn, tw), lambda i, j, k: (j, 0)),
        out_shape=_sds((length, tw), F32), out_spec=ospec, acc_shape=(tm, tn),
        extras=(dy2a, y), extra_specs=(ospec, ospec), epilogue=dy_epilogue)

    tmh = _tile(length, 512, 8)

    def dh_body(dy_ref, cr_ref, ci_ref, gr_ref, gi_ref):
        dyv = dy_ref[...]
        gr_ref[...] = _dotf(dyv, cr_ref[...], _NT)
        gi_ref[...] = -_dotf(dyv, ci_ref[...], _NT)

    hspec = pl.BlockSpec((tmh, S5_TILE_STATES), lambda i, j: (i, j))
    cspec = pl.BlockSpec((None, S5_TILE_STATES, LANES), lambda i, j: (j, 0, 0))
    uspec = pl.BlockSpec((tmh, LANES), lambda i, j: (i, j))
    wide = _sds((length, nt * S5_TILE_STATES), F32)
    g_re, g_im = pl.pallas_call(
        dh_body, name=f"{tag}_dh", grid=(length // tmh, nt),
        in_specs=[uspec, cspec, cspec], out_specs=(hspec, hspec), out_shape=(wide, wide),
        compiler_params=_params(("parallel", "parallel")),
    )(dy, cb_re, cb_im)

    lam_re, lam_im, s_re, s_im = _s5_scan(f"{tag}_rscan", g_re, g_im, ab_re, -ab_im, reverse=True, h=hs, bu=bu)
    den = ab_re * ab_re + ab_im * ab_im
    da_re = (ab_re * s_re - ab_im * s_im) / den
    da_im = (ab_re * s_im + ab_im * s_re) / den

    def du_body(lr_ref, li_ref, br_ref, bi_ref, dy_ref, d_ref, du_ref):
        du_ref[...] = (_dotf(lr_ref[...], br_ref[...], _NT) + _dotf(li_ref[...], bi_ref[...], _NT)
                       + dy_ref[...] * d_ref[...])

    bspec = pl.BlockSpec((None, LANES, S5_TILE_STATES), lambda i, j: (j, 0, 0))
    dvec = pl.BlockSpec((1, LANES), lambda i, j: (0, j))
    du = pl.pallas_call(
        du_body, name=f"{tag}_du", grid=(length // tmh, nt),
        in_specs=[hspec, hspec, bspec, bspec, uspec, dvec], out_specs=uspec,
        out_shape=_sds((length, tw), F32),
        compiler_params=_params(("parallel", "parallel")),
    )(lam_re, lam_im, bb_re, bb_im, dy, dskip)

    nk = length // tkm

    def dpar_body(u_ref, dy_ref, hr_ref, hi_ref, lr_ref, li_ref, dbr_ref, dbi_ref, dcr_ref, dci_ref, dd_ref):
        k = pl.program_id(1)

        @pl.when(k == 0)
        def _():
            for ref in (dbr_ref, dbi_ref, dcr_ref, dci_ref, dd_ref):
                ref[...] = jnp.zeros_like(ref)

        u = u_ref[...]
        dyv = dy_ref[...]
        dbr_ref[...] += _dotf(u, lr_ref[...], _TN)
        dbi_ref[...] += _dotf(u, li_ref[...], _TN)
        dcr_ref[...] += _dotf(hr_ref[...], dyv, _TN)
        dci_ref[...] -= _dotf(hi_ref[...], dyv, _TN)
        dd_ref[...] += jnp.sum(dyv * u, axis=0, keepdims=True)

    kspec_u = pl.BlockSpec((tkm, LANES), lambda j, k: (k, j))
    kspec_h = pl.BlockSpec((tkm, S5_TILE_STATES), lambda j, k: (k, j))
    ob = pl.BlockSpec((None, LANES, S5_TILE_STATES), lambda j, k: (j, 0, 0))
    oc = pl.BlockSpec((None, S5_TILE_STATES, LANES), lambda j, k: (j, 0, 0))
    dbr, dbi, dcr, dci, dd = pl.pallas_call(
        dpar_body, name=f"{tag}_dpar", grid=(nt, nk),
        in_specs=[kspec_u, kspec_u, kspec_h, kspec_h, kspec_h, kspec_h],
        out_specs=(ob, ob, oc, oc, pl.BlockSpec((1, LANES), lambda j, k: (0, j))),
        out_shape=(_sds(bb_re.shape, F32), _sds(bb_re.shape, F32), _sds(cb_re.shape, F32), _sds(cb_re.shape, F32),
                   _sds((1, tw), F32)),
        compiler_params=_params(("parallel", "arbitrary")),
    )(usrc, dy, hs[0], hs[1], lam_re, lam_im)
    return du, (da_re, da_im, dbr, dbi, dcr, dci), dd, dwglu


def _mm_nn(name, a, b, out_dtype=F32, residual=None):
    m, kk = a.shape
    tm = _tile(m, 1024, 8)
    tk = _tile(kk, 2048)
    if b.ndim == 3:
        nsh, _, ns = b.shape
        tn = _tile(ns, 256)
        nb = ns // tn
        n = nsh * ns
        b_spec = pl.BlockSpec((None, tk, tn), lambda i, j, k: (j // nb, k, j % nb))
    else:
        n = b.shape[1]
        tn = _tile(n, 512)
        b_spec = pl.BlockSpec((tk, tn), lambda i, j, k: (k, j))
    ospec = pl.BlockSpec((tm, tn), lambda i, j, k: (i, j))
    extras, especs, epi = (), (), None
    if residual is not None:
        extras, especs = (residual,), (ospec,)

        def epi(acc, ex, outs):
            outs[0][...] = (ex[0][...] + acc).astype(outs[0].dtype)

    return _matmul(name, "nn", a, b, grid=(m // tm, n // tn, kk // tk),
                   a_spec=pl.BlockSpec((tm, tk), lambda i, j, k: (i, k)), b_spec=b_spec,
                   out_shape=_sds((m, n), out_dtype), out_spec=ospec, acc_shape=(tm, tn),
                   extras=extras, extra_specs=especs, epilogue=epi)


def _mm_nt(name, a, b):
    m, n = a.shape
    tm = _tile(m, 512, 8)
    if b.ndim == 3:
        nsh, kk, ns = b.shape
        tn = _tile(ns, 256)
        nb = ns // tn
        b_spec = pl.BlockSpec((None, kk, tn), lambda i, j, k: (k // nb, 0, k % nb))
    else:
        kk = b.shape[0]
        tn = _tile(n, 512)
        b_spec = pl.BlockSpec((kk, tn), lambda i, j, k: (0, k))
    return _matmul(name, "nt", a, b, grid=(m // tm, 1, n // tn),
                   a_spec=pl.BlockSpec((tm, tn), lambda i, j, k: (i, k)), b_spec=b_spec,
                   out_shape=_sds((m, kk), F32), out_spec=pl.BlockSpec((tm, kk), lambda i, j, k: (i, 0)),
                   acc_shape=(tm, kk))


def _mm_tn(name, a, b, shards=None):
    m, kk = a.shape
    n = b.shape[1]
    tkm = _tile(m, 512, 8)
    tk = _tile(kk, 512)
    if shards:
        ns = n // shards
        return _matmul(name, "tn", a, b, grid=(kk // tk, shards, m // tkm),
                       a_spec=pl.BlockSpec((tkm, tk), lambda i, j, k: (k, i)),
                       b_spec=pl.BlockSpec((tkm, ns), lambda i, j, k: (k, j)),
                       out_shape=_sds((shards, kk, ns), F32),
                       out_spec=pl.BlockSpec((None, tk, ns), lambda i, j, k: (j, i, 0)), acc_shape=(tk, ns))
    tn = _tile(n, 2048)
    return _matmul(name, "tn", a, b, grid=(kk // tk, n // tn, m // tkm),
                   a_spec=pl.BlockSpec((tkm, tk), lambda i, j, k: (k, i)),
                   b_spec=pl.BlockSpec((tkm, tn), lambda i, j, k: (k, j)),
                   out_shape=_sds((kk, n), F32), out_spec=pl.BlockSpec((tk, tn), lambda i, j, k: (i, j)),
                   acc_shape=(tk, tn))


def _loss_head(name, y, target):
    m, d = y.shape
    tm = _tile(m, 256, 8)
    nsteps = m // tm

    def body(y_ref, t_ref, loss_ref, dy_ref, dyb_ref, acc_ref):
        i = pl.program_id(0)
        diff = y_ref[...] - t_ref[...]
        dy = diff * (1.0 / d)
        dy_ref[...] = dy
        dyb_ref[...] = dy.astype(BF16)
        sq = (diff * diff).reshape(tm // 8, 8, d).sum(axis=0)

        @pl.when(i == 0)
        def _():
            acc_ref[...] = sq

        @pl.when(i > 0)
        def _():
            acc_ref[...] += sq

        @pl.when(i == nsteps - 1)
        def _():
            loss_ref[...] = jnp.full(loss_ref.shape, jnp.sum(acc_ref[...]) * (0.5 / d), F32)

    row = pl.BlockSpec((tm, d), lambda i: (i, 0))
    return pl.pallas_call(
        body, name=name, grid=(nsteps,),
        in_specs=[row, row], out_specs=(pl.BlockSpec((8, LANES), lambda i: (0, 0)), row, row),
        out_shape=(_sds((8, LANES), F32), _sds((m, d), F32), _sds((m, d), BF16)),
        scratch_shapes=[pltpu.VMEM((8, d), F32)],
        compiler_params=_params(("arbitrary",)),
    )(y, target)


def _adamw(name, w, g, m, v):
    shape = w.shape
    total = math.prod(shape)
    if shape[-1] % LANES and total % LANES == 0:
        view = (total // LANES, LANES)
    else:
        view = (total // shape[-1], shape[-1])
    rows, cols = view
    tr = rows
    if rows * cols * 4 > (1 << 20) and rows % 8 == 0:
        tr = _tile(rows, max(8, ((1 << 20) // (cols * 4)) // 8 * 8), 8)
    c1 = 1.0 / (1.0 - ADAM_B1 ** ADAM_STEP)
    c2 = 1.0 / (1.0 - ADAM_B2 ** ADAM_STEP)

    def body(w_ref, g_ref, m_ref, v_ref, d_ref, nm_ref, nv_ref):
        gv = g_ref[...]
        nm = ADAM_B1 * m_ref[...] + (1.0 - ADAM_B1) * gv
        nv = ADAM_B2 * v_ref[...] + (1.0 - ADAM_B2) * gv * gv
        nm_ref[...] = nm
        nv_ref[...] = nv
        d_ref[...] = -ADAM_LR * ((nm * c1) / (jnp.sqrt(nv * c2) + ADAM_EPS) + ADAM_WD * w_ref[...])

    spec = pl.BlockSpec((tr, cols), lambda i: (i, 0))
    out = _sds(view, F32)
    res = pl.pallas_call(
        body, name=name, grid=(rows // tr,),
        in_specs=[spec] * 4, out_specs=(spec,) * 3, out_shape=(out,) * 3,
        compiler_params=_params(("parallel",)),
    )(w.reshape(view), g.reshape(view), m.reshape(view), v.reshape(view))
    return tuple(r.reshape(shape) for r in res)


def _sum_leading(name, x, out_dtype=F32):
    k, rows, cols = x.shape
    tr = _tile(rows, max(8, ((1 << 20) // (cols * 4)) // 8 * 8), 8)

    def body(x_ref, o_ref):
        acc = x_ref[0].astype(F32)
        for j in range(1, k):
            acc = acc + x_ref[j].astype(F32)
        o_ref[...] = acc.astype(out_dtype)

    return pl.pallas_call(
        body, name=name, grid=(rows // tr,),
        in_specs=[pl.BlockSpec((k, tr, cols), lambda i: (0, i, 0))],
        out_specs=pl.BlockSpec((tr, cols), lambda i: (i, 0)), out_shape=_sds((rows, cols), out_dtype),
        compiler_params=_params(("parallel",)),
    )(x)


def _add_pair(name, a, b, out_dtype):
    k, rows, cols = a.shape
    tr = _tile(rows, max(8, ((1 << 20) // (cols * 4)) // 8 * 8), 8)

    def body(a_ref, b_ref, o_ref):
        o_ref[...] = (a_ref[...] + b_ref[...]).astype(out_dtype)

    spec = pl.BlockSpec((None, tr, cols), lambda s, i: (s, i, 0))
    return pl.pallas_call(
        body, name=name, grid=(k, rows // tr), in_specs=[spec, spec], out_specs=spec,
        out_shape=_sds(a.shape, out_dtype), compiler_params=_params(("parallel", "parallel")),
    )(a, b)


_ANY = pl.BlockSpec(memory_space=pl.ANY)


def _place():
    x, y, c = lax.axis_index("x"), lax.axis_index("y"), lax.axis_index("c")
    return x, y, c, [(1 - x, y), (x, 1 - y), (1 - x, 1 - y)]


def _remote(src, dst, send_sems, recv_sems, k, to):
    return pltpu.make_async_remote_copy(src_ref=src, dst_ref=dst, send_sem=send_sems.at[k], recv_sem=recv_sems.at[k],
                                        device_id=to, device_id_type=MESH)


def _gather_chips(name, flat):
    def body(in_ref, out_ref, send_sems, recv_sems, local_sem):
        x, y, c, chips = _place()
        sibling = (x, y, 1 - c)

        def slab(chip, half):
            return out_ref.at[2 * chip[0] + chip[1], half]

        mine = pltpu.make_async_copy(in_ref, out_ref.at[2 * x + y], local_sem)
        mine.start()
        first = [_remote(in_ref.at[c], slab((x, y), c), send_sems, recv_sems, j, (*chip, c))
                 for j, chip in enumerate(chips)]
        for cp in first:
            cp.start()
        passed = [_remote(slab(chip, c), slab(chip, c), send_sems, recv_sems, 3 + j, sibling)
                  for j, chip in enumerate(chips)]
        for j, chip in enumerate(chips):
            _remote(in_ref.at[c], slab(chip, c), send_sems, recv_sems, j, (x, y, c)).wait_recv()
            passed[j].start()
        for j, chip in enumerate(chips):
            _remote(in_ref.at[c], slab(chip, 1 - c), send_sems, recv_sems, 3 + j, (x, y, c)).wait_recv()
        for cp in first + passed:
            cp.wait_send()
        mine.wait()

    return pl.pallas_call(
        body, name=name, in_specs=[_ANY], out_specs=_ANY,
        out_shape=_sds((N_CHIPS,) + flat.shape, flat.dtype),
        scratch_shapes=[pltpu.SemaphoreType.DMA((6,)), pltpu.SemaphoreType.DMA((6,)), pltpu.SemaphoreType.DMA],
    )(flat)


def _to_sibling(name, src):
    def body(in_ref, out_ref, send_sems, recv_sems):
        x, y, c, _ = _place()
        cp = _remote(in_ref, out_ref, send_sems, recv_sems, 0, (x, y, 1 - c))
        cp.start()
        cp.wait()

    return pl.pallas_call(
        body, name=name, in_specs=[_ANY], out_specs=_ANY, out_shape=_sds(src.shape, src.dtype),
        scratch_shapes=[pltpu.SemaphoreType.DMA((1,)), pltpu.SemaphoreType.DMA((1,))],
    )(src)


def _scatter_chips(name, part):
    def body(in_ref, out_ref, send_sems, recv_sems, local_sem):
        x, y, c, chips = _place()
        me = 2 * x + y
        mine = pltpu.make_async_copy(in_ref.at[me], out_ref.at[me], local_sem)
        mine.start()
        sends = [_remote(in_ref.at[2 * chip[0] + chip[1]], out_ref.at[me], send_sems, recv_sems, j, (*chip, c))
                 for j, chip in enumerate(chips)]
        for cp in sends:
            cp.start()
        for j, chip in enumerate(chips):
            _remote(in_ref.at[me], out_ref.at[2 * chip[0] + chip[1]], send_sems, recv_sems, j, (x, y, c)).wait_recv()
        for cp in sends:
            cp.wait_send()
        mine.wait()

    return pl.pallas_call(
        body, name=name, in_specs=[_ANY], out_specs=_ANY, out_shape=_sds(part.shape, part.dtype),
        scratch_shapes=[pltpu.SemaphoreType.DMA((3,)), pltpu.SemaphoreType.DMA((3,)), pltpu.SemaphoreType.DMA],
    )(part)


def _join_halves(name, half):
    def body(in_ref, out_ref, send_sems, recv_sems, local_sem):
        x, y, c, _ = _place()
        mine = pltpu.make_async_copy(in_ref, out_ref.at[c], local_sem)
        mine.start()
        cp = _remote(in_ref, out_ref.at[c], send_sems, recv_sems, 0, (x, y, 1 - c))
        cp.start()
        _remote(in_ref, out_ref.at[1 - c], send_sems, recv_sems, 0, (x, y, c)).wait_recv()
        cp.wait_send()
        mine.wait()

    return pl.pallas_call(
        body, name=name, in_specs=[_ANY], out_specs=_ANY, out_shape=_sds((2,) + half.shape, half.dtype),
        scratch_shapes=[pltpu.SemaphoreType.DMA((1,)), pltpu.SemaphoreType.DMA((1,)), pltpu.SemaphoreType.DMA],
    )(half)


def _gather_all(name, block):
    rows, cols = block.shape

    def body(x_ref, out_ref, send_sems, recv_sems, local_sem):
        x, y, c, chips = _place()
        me, sibling = (x, y, c), (x, y, 1 - c)

        def at(px, py, pc):
            return out_ref.at[4 * px + 2 * py + pc]

        def copy(k, blk, to, src=None):
            return _remote(at(*blk) if src is None else src, at(*blk), send_sems, recv_sems, k, to)

        mine = pltpu.make_async_copy(x_ref, at(*me), local_sem)
        mine.start()
        first = [copy(0, me, sibling, src=x_ref)]
        first += [copy(1 + j, me, (*chip, c), src=x_ref) for j, chip in enumerate(chips)]
        for cp in first:
            cp.start()
        passed = [copy(4 + j, (*chip, c), sibling) for j, chip in enumerate(chips)]
        for j, chip in enumerate(chips):
            copy(1 + j, (*chip, c), me).wait_recv()
            passed[j].start()
        copy(0, sibling, me).wait_recv()
        for j, chip in enumerate(chips):
            copy(4 + j, (*chip, 1 - c), me).wait_recv()
        for cp in first + passed:
            cp.wait_send()
        mine.wait()

    return pl.pallas_call(
        body, name=name,
        in_specs=[pl.BlockSpec(memory_space=pltpu.VMEM)], out_specs=pl.BlockSpec(memory_space=pltpu.VMEM),
        out_shape=_sds((8, rows, cols), block.dtype),
        scratch_shapes=[pltpu.SemaphoreType.DMA((7,)), pltpu.SemaphoreType.DMA((7,)), pltpu.SemaphoreType.DMA],
        compiler_params=pltpu.CompilerParams(vmem_limit_bytes=V7X_VMEM_LIMIT),
    )(block)


WEIGHTS = ("ffn1_norm", "ffn1_w_gu", "ffn1_w_down", "mix_norm", "mem_norm", "w_mem_kv", "xq_norm", "xk_norm", "w_out",
           "ffn2_norm", "ffn2_w_gu", "ffn2_w_down", "sb_w_in", "s5_w_in", "s5_log_dt", "s5_a_re", "s5_a_im", "s5_b_re",
           "s5_b_im", "s5_c_re", "s5_c_im", "s5_d", "s5_w_glu")
BIG = ("ffn1_w_gu", "ffn1_w_down", "w_mem_kv", "w_out", "ffn2_w_gu", "ffn2_w_down", "sb_w_in", "s5_w_in", "s5_w_glu")
COLUMN_SHARDED = ("ffn1_w_gu", "ffn2_w_gu", "sb_w_in")
SMALL = tuple(n for n in WEIGHTS if n not in BIG)
FLAT_COLS = 512
FLAT_GRANULE = FLAT_COLS * 1024


def _pack_halves(parts):
    cols = []
    for arr, axis in parts:
        n = arr.shape[axis]
        lead = arr.shape[:axis]
        cols.append(arr.reshape(lead + (2, (n // 2) * math.prod(arr.shape[axis + 1:]))))
    size = sum(col.shape[-1] for col in cols)
    pad = -size % FLAT_GRANULE
    if pad:
        cols.append(jnp.zeros(cols[0].shape[:-1] + (pad,), cols[0].dtype))
    return jnp.concatenate(cols, axis=-1)


def _layer_weights(full, i):
    j = i // 2
    w = {}
    for name in BIG:
        idx = j if name in ("sb_w_in", "s5_w_in", "s5_w_glu") else i
        if (name == "sb_w_in" and i % 2) or (name in ("s5_w_in", "s5_w_glu") and i % 2 == 0):
            continue
        piece = full[name][:, idx]
        w[name] = piece if name in COLUMN_SHARDED else piece.reshape((-1,) + piece.shape[2:])
    return w


def _row(v):
    return v.reshape(1, -1)


def _layer_fwd(i, x, mem, w, p, s5prep):
    tag = f"l{i}"
    j = i // 2
    x1, sv_ffn1 = _ffn_fwd(f"{tag}_ffn1", x, _row(p["ffn1_norm"][i]), w["ffn1_w_gu"], w["ffn1_w_down"])
    h = _rmsnorm_fwd(f"{tag}_mixnorm", x1, _row(p["mix_norm"][i]))
    if i % 2 == 0:
        proj = _mm_nn(f"{tag}_inproj", h, w["sb_w_in"])
        n_heads = (proj.shape[1] * 3 // 10) // HEAD_DIM
        tok, mix_saved = _sb_fwd(f"{tag}_sb", proj, n_heads)
        tok_w = n_heads * HEAD_DIM
        qcol0 = 3 * tok_w
    else:
        proj = _mm_nn(f"{tag}_inproj", h, w["s5_w_in"])
        tok, mix_saved = _s5_fwd(f"{tag}_s5", proj, s5prep[j], _row(p["s5_d"][j]), w["s5_w_glu"])
        tok_w = tok.shape[1]
        qcol0 = tok_w
    mem_h = _rmsnorm_fwd(f"{tag}_memnorm", mem, _row(p["mem_norm"][i]))
    kv = _mm_nn(f"{tag}_memkv", mem_h, w["w_mem_kv"])
    gq, gk = _row(p["xq_norm"][i]), _row(p["xk_norm"][i])
    cross = _mem_fwd(f"{tag}_mem", proj, qcol0, kv, gq, gk)
    cat = jnp.concatenate([tok, cross], axis=1).astype(BF16)
    x2 = _mm_nn(f"{tag}_outproj", cat, w["w_out"], residual=x1)
    x3, sv_ffn2 = _ffn_fwd(f"{tag}_ffn2", x2, _row(p["ffn2_norm"][i]), w["ffn2_w_gu"], w["ffn2_w_down"])
    saved = dict(ffn1=sv_ffn1, x1=x1, h=h, proj=proj, mix=mix_saved, mem_h=mem_h, kv=kv, cat=cat, ffn2=sv_ffn2,
                 tok_w=tok_w, qcol0=qcol0)
    return x3, saved


def _layer_bwd(i, sv, mem, w, p, s5prep, dx3, dx3b):
    tag = f"l{i}b"
    j = i // 2
    g = {}
    dx2, dx2b, g["ffn2_norm"], g["ffn2_w_gu"], g["ffn2_w_down"] = _ffn_bwd(
        f"{tag}_ffn2", sv["ffn2"], _row(p["ffn2_norm"][i]), w["ffn2_w_gu"], w["ffn2_w_down"], dx3, dx3b)
    dcat = _mm_nt(f"{tag}_dcat", dx2b, w["w_out"])
    g["w_out"] = _mm_tn(f"{tag}_dwout", sv["cat"], dx2b)
    gq, gk = _row(p["xq_norm"][i]), _row(p["xk_norm"][i])
    tok_w, qcol0 = sv["tok_w"], sv["qcol0"]
    dqm, dkv, g["xq_norm"], g["xk_norm"] = _mem_bwd(f"{tag}_mem", sv["proj"], qcol0, sv["kv"], gq, gk, dcat, tok_w)
    dkvb = dkv.astype(BF16)
    g["w_mem_kv"] = _mm_tn(f"{tag}_dwkv", sv["mem_h"], dkvb)
    dmem_h = _mm_nt(f"{tag}_dmemh", dkvb, w["w_mem_kv"])
    g["mem_norm"] = _rmsnorm_bwd(f"{tag}_dmemnorm", mem, dmem_h, _row(p["mem_norm"][i]))
    if i % 2 == 0:
        dq, dk, dv = _sb_bwd(f"{tag}_sb", sv["proj"], sv["mix"], dcat, tok_w // HEAD_DIM)
        dproj = jnp.concatenate([dq, dk, dv, dqm], axis=1).astype(BF16)
        g["sb_w_in"] = _mm_tn(f"{tag}_dwin", sv["h"], dproj, shards=N_CHIPS)
        dh = _mm_nt(f"{tag}_dh", dproj, w["sb_w_in"])
    else:
        du, g["s5_prep"], g["s5_d"], g["s5_w_glu"] = _s5_bwd(
            f"{tag}_s5", sv["proj"], s5prep[j], _row(p["s5_d"][j]), w["s5_w_glu"], sv["mix"], dcat)
        dproj = jnp.concatenate([du, dqm], axis=1).astype(BF16)
        g["s5_w_in"] = _mm_tn(f"{tag}_dwin", sv["h"], dproj)
        dh = _mm_nt(f"{tag}_dh", dproj, w["s5_w_in"])
    dx1, dx1b, g["mix_norm"] = _rmsnorm_bwd(f"{tag}_dmixnorm", sv["x1"], dh, _row(p["mix_norm"][i]), dx2)
    dx, dxb, g["ffn1_norm"], g["ffn1_w_gu"], g["ffn1_w_down"] = _ffn_bwd(
        f"{tag}_ffn1", sv["ffn1"], _row(p["ffn1_norm"][i]), w["ffn1_w_gu"], w["ffn1_w_down"], dx1, dx1b)
    return dx, dxb, g


def _local_step(x, mem, target, full, p):
    depth = p["ffn1_norm"].shape[0]
    n_s5 = depth // 2
    s5_names = ("s5_log_dt", "s5_a_re", "s5_a_im", "s5_b_re", "s5_b_im", "s5_c_re", "s5_c_im")
    s5prep, s5vjp = [], []
    for j in range(n_s5):
        out, vjp = jax.vjp(_s5_prep, *[p[n][j] for n in s5_names])
        s5prep.append(out)
        s5vjp.append(vjp)
    ws = [_layer_weights(full, i) for i in range(depth)]
    saved = []
    for i in range(depth):
        x, sv = _layer_fwd(i, x, mem, ws[i], p, s5prep)
        saved.append(sv)
    loss, dx, dxb = _loss_head("loss", x, target)
    per_layer = [None] * depth
    for i in reversed(range(depth)):
        dx, dxb, per_layer[i] = _layer_bwd(i, saved[i], mem, ws[i], p, s5prep, dx, dxb)
    grads = {}
    for name in ("ffn1_norm", "mix_norm", "mem_norm", "xq_norm", "xk_norm", "ffn2_norm"):
        grads[name] = jnp.concatenate([per_layer[i][name] for i in range(depth)], axis=0)
    for name in ("ffn1_w_gu", "ffn1_w_down", "w_mem_kv", "w_out", "ffn2_w_gu", "ffn2_w_down"):
        grads[name] = [per_layer[i][name] for i in range(depth)]
    grads["sb_w_in"] = [per_layer[i]["sb_w_in"] for i in range(0, depth, 2)]
    grads["s5_w_in"] = [per_layer[i]["s5_w_in"] for i in range(1, depth, 2)]
    grads["s5_w_glu"] = [per_layer[i]["s5_w_glu"] for i in range(1, depth, 2)]
    grads["s5_d"] = jnp.concatenate([per_layer[i]["s5_d"] for i in range(1, depth, 2)], axis=0)
    s5g = [s5vjp[j](tuple(per_layer[2 * j + 1]["s5_prep"])) for j in range(n_s5)]
    for k, name in enumerate(s5_names):
        grads[name] = jnp.stack([s5g[j][k] for j in range(n_s5)], axis=0)
    for name in BIG:
        grads[name] = jnp.stack([gl if name in COLUMN_SHARDED else gl.reshape((N_CHIPS, -1) + gl.shape[1:])
                                 for gl in grads[name]], axis=0)
    return loss, dx, grads


def _flat_rows(v, cols):
    return v.reshape(v.shape[:-1] + (v.shape[-1] // cols, cols))


def _gather_weights(shards):
    flat = _pack_halves([(shards[n].astype(BF16), 0) for n in BIG])
    got = _gather_chips("gather_weights", _flat_rows(flat, FLAT_COLS))
    got = got.reshape(N_CHIPS, 2, -1)
    full, off = {}, 0
    for n in BIG:
        shp = shards[n].shape
        size = math.prod(shp) // 2
        full[n] = got[:, :, off:off + size].reshape((N_CHIPS,) + shp)
        off += size
    return full


def _reduce_big(grads, shapes):
    c = lax.axis_index("c")
    flat = _pack_halves([(jnp.swapaxes(grads[n], 0, 1), 1) for n in BIG])
    mine = _flat_rows(jnp.where(c == 0, flat[:, 0], flat[:, 1]), FLAT_COLS)
    other = _flat_rows(jnp.where(c == 0, flat[:, 1], flat[:, 0]), FLAT_COLS)
    from_sibling = _to_sibling("reduce_to_sibling", other)
    chip_part = _add_pair("reduce_add_cores", mine, from_sibling, BF16)
    slabs = _scatter_chips("reduce_scatter_chips", chip_part)
    half = _sum_leading("reduce_add_chips", slabs)
    both = _join_halves("reduce_join_halves", half).reshape(2, -1)
    out, off = {}, 0
    for n in BIG:
        shp = shapes[n]
        size = math.prod(shp) // 2
        out[n] = both[:, off:off + size].reshape(shp)
        off += size
    return out


def _reduce_small(grads, shapes):
    flat = jnp.concatenate([grads[n].reshape(-1) for n in SMALL])
    total = flat.shape[0]
    rows = -(-total // (512 * LANES)) * 512
    block = jnp.pad(flat, (0, rows * LANES - total)).reshape(rows, LANES)
    summed = _sum_leading("reduce_small_sum", _gather_all("reduce_small_gather", block)).reshape(-1)
    out, off = {}, 0
    for n in SMALL:
        size = math.prod(shapes[n])
        out[n] = summed[off:off + size].reshape(shapes[n])
        off += size
    return out


def kernel(x, mem, ffn1_norm, ffn1_w_gu, ffn1_w_down, mix_norm, mem_norm, w_mem_kv, xq_norm, xk_norm, w_out, ffn2_norm, ffn2_w_gu, ffn2_w_down, sb_w_in, s5_w_in, s5_log_dt, s5_a_re, s5_a_im, s5_b_re, s5_b_im, s5_c_re, s5_c_im, s5_d, s5_w_glu, loss_target, m_ffn1_norm, m_ffn1_w_gu, m_ffn1_w_down, m_mix_norm, m_mem_norm, m_w_mem_kv, m_xq_norm, m_xk_norm, m_w_out, m_ffn2_norm, m_ffn2_w_gu, m_ffn2_w_down, m_sb_w_in, m_s5_w_in, m_s5_log_dt, m_s5_a_re, m_s5_a_im, m_s5_b_re, m_s5_b_im, m_s5_c_re, m_s5_c_im, m_s5_d, m_s5_w_glu, v_ffn1_norm, v_ffn1_w_gu, v_ffn1_w_down, v_mix_norm, v_mem_norm, v_w_mem_kv, v_xq_norm, v_xk_norm, v_w_out, v_ffn2_norm, v_ffn2_w_gu, v_ffn2_w_down, v_sb_w_in, v_s5_w_in, v_s5_log_dt, v_s5_a_re, v_s5_a_im, v_s5_b_re, v_s5_b_im, v_s5_c_re, v_s5_c_im, v_s5_d, v_s5_w_glu):
    given = dict(locals())
    wts = {n: given[n] for n in WEIGHTS}
    chip = 2 * lax.axis_index("x") + lax.axis_index("y")

    d_sh = s5_d.shape
    d_rows = -(-math.prod(d_sh) // (8 * LANES)) * 8
    d_block = jnp.pad(s5_d.reshape(-1), (0, d_rows * LANES - math.prod(d_sh))).reshape(d_rows, LANES)
    d_all = _gather_all("gather_s5_d", d_block).reshape(8, -1)[:, :math.prod(d_sh)]
    d_full = jnp.concatenate([d_all[2 * s].reshape(d_sh) for s in range(N_CHIPS)], axis=1)

    full = _gather_weights({n: wts[n] for n in BIG})
    p = {n: wts[n] for n in SMALL}
    p["s5_d"] = d_full
    loss, grad_x, grads = _local_step(x[0], mem[0], loss_target[0], full, p)

    shapes = {n: wts[n].shape for n in WEIGHTS}
    small_shapes = dict(shapes)
    small_shapes["s5_d"] = d_full.shape
    gsum = _reduce_big(grads, shapes)
    gsmall = _reduce_small(grads, small_shapes)
    gsmall["s5_d"] = lax.dynamic_slice_in_dim(gsmall["s5_d"], chip * d_sh[1], d_sh[1], axis=1)
    gsum.update(gsmall)

    deltas, new_m, new_v = {}, {}, {}
    for n in WEIGHTS:
        deltas[n], new_m[n], new_v[n] = _adamw(f"adamw_{n}", wts[n], gsum[n], given["m_" + n], given["v_" + n])
    total_loss = lax.psum(loss[0, 0], ("x", "y", "c"))
    return (total_loss, grad_x[None], *[gsum[n] for n in WEIGHTS], *[deltas[n] for n in WEIGHTS],
            *[new_m[n] for n in WEIGHTS], *[new_v[n] for n in WEIGHTS])
```

```python
import math
from typing import NamedTuple

import jax
import jax.numpy as jnp
from jax import lax
from jax.experimental import pallas as pl
from jax.experimental.pallas import tpu as pltpu

F32 = jnp.float32
BF16 = jnp.bfloat16

HEAD_DIM = 128
S5_GROUP = 16
S5_STATE = 64
EPS = 1e-6
ADAM_LR = 0.001
ADAM_B1 = 0.9
ADAM_B2 = 0.999
ADAM_EPS = 1e-08
ADAM_WD = 0.01
ADAM_STEP = 10

LANES = 128
V7X_VMEM_LIMIT = 56 * 1024 * 1024
N_CHIPS = 4
MESH = pl.DeviceIdType.MESH
_NT = (((1,), (1,)), ((), ()))
_TN = (((0,), (0,)), ((), ()))


def _params(sem):
    return pltpu.CompilerParams(dimension_semantics=sem, vmem_limit_bytes=V7X_VMEM_LIMIT)


def _tile(n, want, mult=LANES):
    if n <= want:
        return n
    t = (want // mult) * mult
    while t > mult and n % t:
        t -= mult
    assert n % t == 0, (n, want, mult)
    return t


def _row_tile(rows, cols, itemsize=4, target=1 << 20):
    return _tile(rows, max(16, (target // (cols * itemsize)) // 16 * 16), 16)


def _sds(shape, dtype):
    return jax.ShapeDtypeStruct(tuple(shape), dtype)


class _W(NamedTuple):
    arr: jax.Array
    layer: int
    cols: bool


def _wspec(w, tr, tc, rb, cb):
    _, _, r, c = w.arr.shape
    layer = w.layer
    assert r % tr == 0 and c % tc == 0, (r, c, tr, tc)
    if w.cols:
        nb = c // tc
        return pl.BlockSpec((None, None, tr, tc), lambda *g: (cb(*g) // nb, layer, rb(*g), cb(*g) % nb))
    nb = r // tr
    return pl.BlockSpec((None, None, tr, tc), lambda *g: (rb(*g) // nb, layer, rb(*g) % nb, cb(*g)))


_DIMS = {"nn": (((1,), (0,)), ((), ())), "nt": _NT, "tn": _TN}


def _matmul(name, mode, a, b, *, grid, a_spec, b_spec, out_shape, out_spec, acc_shape,
            extras=(), extra_specs=(), epilogue=None):
    nk = grid[2]
    ne = len(extras)
    multi = isinstance(out_shape, (tuple, list))
    outs = tuple(out_shape) if multi else (out_shape,)
    ospecs = tuple(out_spec) if multi else (out_spec,)
    no = len(outs)

    def body(a_ref, b_ref, *rest):
        ex = rest[:ne]
        out_refs = rest[ne:ne + no]
        part = lax.dot_general(a_ref[...].astype(BF16), b_ref[...].astype(BF16), _DIMS[mode],
                               preferred_element_type=F32)

        def finish(acc):
            if epilogue is None:
                out_refs[0][...] = acc.astype(out_refs[0].dtype)
            else:
                epilogue(acc, ex, out_refs)

        if nk == 1:
            finish(part)
        else:
            acc_ref = rest[-1]
            k = pl.program_id(2)

            @pl.when(k == 0)
            def _():
                acc_ref[...] = part

            @pl.when(k > 0)
            def _():
                acc_ref[...] += part

            @pl.when(k == nk - 1)
            def _():
                finish(acc_ref[...])

    return pl.pallas_call(
        body, name=name, grid=grid,
        in_specs=[a_spec, b_spec, *extra_specs],
        out_specs=ospecs if multi else ospecs[0],
        out_shape=outs if multi else outs[0],
        scratch_shapes=[pltpu.VMEM(acc_shape, F32)] if nk > 1 else [],
        compiler_params=_params(("parallel", "parallel", "arbitrary")),
    )(a, b, *extras)


def _mm_nn(name, a, w, out_dtype=F32, extras=(), epilogue=None, n_out=1):
    m, kk = a.shape
    _, _, r, c = w.arr.shape
    tm = _tile(m, 1024, 8)
    tk = _tile(r, 2048)
    if w.cols:
        n, tn = N_CHIPS * c, _tile(c, 256)
    else:
        n, tn = c, _tile(c, 512)
    ospec = pl.BlockSpec((tm, tn), lambda i, j, k: (i, j))
    out_shape = _sds((m, n), out_dtype)
    return _matmul(name, "nn", a, w.arr, grid=(m // tm, n // tn, kk // tk),
                   a_spec=pl.BlockSpec((tm, tk), lambda i, j, k: (i, k)),
                   b_spec=_wspec(w, tk, tn, lambda i, j, k: k, lambda i, j, k: j),
                   out_shape=out_shape if n_out == 1 else (out_shape,) * n_out,
                   out_spec=ospec if n_out == 1 else (ospec,) * n_out, acc_shape=(tm, tn),
                   extras=extras, extra_specs=(ospec,) * len(extras), epilogue=epilogue)


def _add_residual(acc, ex, outs):
    outs[0][...] = ex[0][...] + acc


def _mm_nt(name, a, w, extras=(), epilogue=None):
    m, n = a.shape
    _, _, r, c = w.arr.shape
    tm = _tile(m, 512, 8)
    if w.cols:
        kk, tr = r, r
        tc = c if r * c * 2 <= (12 << 20) else _tile(c, 1408)
    else:
        kk, tr, tc = N_CHIPS * r, _tile(r, 512), _tile(c, 2048)
    ospec = pl.BlockSpec((tm, tr), lambda i, j, k: (i, j))
    return _matmul(name, "nt", a, w.arr, grid=(m // tm, kk // tr, n // tc),
                   a_spec=pl.BlockSpec((tm, tc), lambda i, j, k: (i, k)),
                   b_spec=_wspec(w, tr, tc, lambda i, j, k: j, lambda i, j, k: k),
                   out_shape=_sds((m, kk), F32), out_spec=ospec, acc_shape=(tm, tr),
                   extras=extras, extra_specs=(ospec,) * len(extras), epilogue=epilogue)


def _mm_tn(name, a, b, shards=None):
    m, kk = a.shape
    n = b.shape[1]
    tkm = _tile(m, 512, 8)
    tk = _tile(kk, 512)
    if shards:
        ns = n // shards
        return _matmul(name, "tn", a, b, grid=(kk // tk, shards, m // tkm),
                       a_spec=pl.BlockSpec((tkm, tk), lambda i, j, k: (k, i)),
                       b_spec=pl.BlockSpec((tkm, ns), lambda i, j, k: (k, j)),
                       out_shape=_sds((shards, kk, ns), F32),
                       out_spec=pl.BlockSpec((None, tk, ns), lambda i, j, k: (j, i, 0)), acc_shape=(tk, ns))
    tn = _tile(n, 2048)
    return _matmul(name, "tn", a, b, grid=(kk // tk, n // tn, m // tkm),
                   a_spec=pl.BlockSpec((tkm, tk), lambda i, j, k: (k, i)),
                   b_spec=pl.BlockSpec((tkm, tn), lambda i, j, k: (k, j)),
                   out_shape=_sds((kk, n), F32), out_spec=pl.BlockSpec((tk, tn), lambda i, j, k: (i, j)),
                   acc_shape=(tk, tn))


def _rmsnorm_fwd(name, x, gain):
    m, d = x.shape
    tm = _tile(m, 512, 8)

    def body(x_ref, g_ref, o_ref):
        xv = x_ref[...]
        r = lax.rsqrt(jnp.mean(xv * xv, axis=-1, keepdims=True) + EPS)
        o_ref[...] = (xv * r * g_ref[...]).astype(o_ref.dtype)

    return pl.pallas_call(
        body, name=name, grid=(m // tm,),
        in_specs=[pl.BlockSpec((tm, d), lambda i: (i, 0)), pl.BlockSpec((1, d), lambda i: (0, 0))],
        out_specs=pl.BlockSpec((tm, d), lambda i: (i, 0)),
        out_shape=_sds((m, d), BF16),
        compiler_params=_params(("parallel",)),
    )(x, gain)


def _rmsnorm_bwd(name, x, dh, gain, dres=None):
    m, d = x.shape
    tm = _tile(m, 256, 8)
    nsteps = m // tm
    with_dx = dres is not None

    def body(*refs):
        if with_dx:
            x_ref, dh_ref, g_ref, dres_ref, dx_ref, dxb_ref, dg_ref, acc_ref = refs
        else:
            x_ref, dh_ref, g_ref, dg_ref, acc_ref = refs
        i = pl.program_id(0)
        xv = x_ref[...]
        r = lax.rsqrt(jnp.mean(xv * xv, axis=-1, keepdims=True) + EPS)
        xh = xv * r
        dhv = dh_ref[...].astype(F32)
        contrib = (dhv * xh).reshape(tm // 8, 8, d).sum(axis=0)

        @pl.when(i == 0)
        def _():
            acc_ref[...] = contrib

        @pl.when(i > 0)
        def _():
            acc_ref[...] += contrib

        @pl.when(i == nsteps - 1)
        def _():
            dg_ref[...] = jnp.sum(acc_ref[...], axis=0, keepdims=True)

        if with_dx:
            dxh = dhv * g_ref[...]
            dx = r * (dxh - xh * jnp.mean(dxh * xh, axis=-1, keepdims=True)) + dres_ref[...]
            dx_ref[...] = dx
            dxb_ref[...] = dx.astype(BF16)

    row = pl.BlockSpec((tm, d), lambda i: (i, 0))
    vec = pl.BlockSpec((1, d), lambda i: (0, 0))
    if with_dx:
        return pl.pallas_call(
            body, name=name, grid=(nsteps,),
            in_specs=[row, row, vec, row], out_specs=(row, row, vec),
            out_shape=(_sds((m, d), F32), _sds((m, d), BF16), _sds((1, d), F32)),
            scratch_shapes=[pltpu.VMEM((8, d), F32)],
            compiler_params=_params(("arbitrary",)),
        )(x, dh, gain, dres)
    return pl.pallas_call(
        body, name=name, grid=(nsteps,),
        in_specs=[row, row, vec], out_specs=vec,
        out_shape=_sds((1, d), F32),
        scratch_shapes=[pltpu.VMEM((8, d), F32)],
        compiler_params=_params(("arbitrary",)),
    )(x, dh, gain)


def _ffn_fwd(tag, x, gain, wgu, wdown):
    m, d = x.shape
    ns = wgu.arr.shape[3]
    f = 2 * ns
    h = _rmsnorm_fwd(f"{tag}_norm", x, gain)
    tm = _tile(m, 1024, 8)
    tn = _tile(ns, 256)
    nb = ns // tn

    def gu_body(h_ref, wg_ref, wu_ref, act_ref, gu_ref):
        hv = h_ref[...]
        g = jnp.dot(hv, wg_ref[...], preferred_element_type=F32)
        u = jnp.dot(hv, wu_ref[...], preferred_element_type=F32)
        act_ref[...] = (g * jax.nn.sigmoid(g) * u).astype(BF16)
        gu_ref[0] = g.astype(BF16)
        gu_ref[1] = u.astype(BF16)

    act, gu = pl.pallas_call(
        gu_body, name=f"{tag}_gu", grid=(m // tm, 2 * nb),
        in_specs=[pl.BlockSpec((tm, d), lambda i, j: (i, 0)),
                  _wspec(wgu, d, tn, lambda i, j: 0, lambda i, j: j),
                  _wspec(wgu, d, tn, lambda i, j: 0, lambda i, j: 2 * nb + j)],
        out_specs=(pl.BlockSpec((tm, tn), lambda i, j: (i, j)),
                   pl.BlockSpec((2, tm, tn), lambda i, j: (0, i, j))),
        out_shape=(_sds((m, f), BF16), _sds((2, m, f), BF16)),
        compiler_params=_params(("parallel", "parallel")),
    )(h, wgu.arr, wgu.arr)

    tk = wdown.arr.shape[2]
    tnd = _tile(d, 512)

    def down_epilogue(acc, ex, outs):
        outs[0][...] = ex[0][...] + 0.5 * acc

    ospec = pl.BlockSpec((tm, tnd), lambda i, j, k: (i, j))
    y = _matmul(
        f"{tag}_down", "nn", act, wdown.arr, grid=(m // tm, d // tnd, f // tk),
        a_spec=pl.BlockSpec((tm, tk), lambda i, j, k: (i, k)),
        b_spec=_wspec(wdown, tk, tnd, lambda i, j, k: k, lambda i, j, k: j),
        out_shape=_sds((m, d), F32), out_spec=ospec, acc_shape=(tm, tnd),
        extras=(x,), extra_specs=(ospec,), epilogue=down_epilogue)
    return y, (x, h, act, gu)


def _ffn_bwd(tag, saved, gain, wgu, wdown, dy, dyb):
    x, h, act, gu = saved
    m, d = x.shape
    ns = wgu.arr.shape[3]
    f = 2 * ns

    def dact_epilogue(acc, ex, outs):
        g = ex[0][0].astype(F32)
        u = ex[0][1].astype(F32)
        da = 0.5 * acc
        s = jax.nn.sigmoid(g)
        outs[0][0] = (da * u * s * (1.0 + g * (1.0 - s))).astype(BF16)
        outs[0][1] = (da * g * s).astype(BF16)

    tma = _tile(m, 512, 8)
    tna = wdown.arr.shape[2]
    gspec = pl.BlockSpec((2, tma, tna), lambda i, j, k: (0, i, j))
    dgu = _matmul(
        f"{tag}_dact", "nt", dyb, wdown.arr, grid=(m // tma, f // tna, 1),
        a_spec=pl.BlockSpec((tma, d), lambda i, j, k: (i, 0)),
        b_spec=_wspec(wdown, tna, d, lambda i, j, k: j, lambda i, j, k: 0),
        out_shape=_sds((2, m, f), BF16), out_spec=gspec, acc_shape=(tma, tna),
        extras=(gu,), extra_specs=(gspec,), epilogue=dact_epilogue)

    tkm = _tile(m, 512, 8)
    tf = _tile(f, 1408)
    tnd = _tile(d, 1024)

    def half_epilogue(acc, ex, outs):
        outs[0][...] = 0.5 * acc

    dwdown = _matmul(
        f"{tag}_dwdown", "tn", act, dyb, grid=(f // tf, d // tnd, m // tkm),
        a_spec=pl.BlockSpec((tkm, tf), lambda i, j, k: (k, i)),
        b_spec=pl.BlockSpec((tkm, tnd), lambda i, j, k: (k, j)),
        out_shape=_sds((f, d), F32), out_spec=pl.BlockSpec((tf, tnd), lambda i, j, k: (i, j)),
        acc_shape=(tf, tnd), epilogue=half_epilogue)

    td = _tile(d, 512)
    dwgu = _matmul(
        f"{tag}_dwgu", "tn", h, dgu, grid=(d // td, 4, m // tkm),
        a_spec=pl.BlockSpec((tkm, td), lambda i, j, k: (k, i)),
        b_spec=pl.BlockSpec((None, tkm, ns), lambda i, j, k: (j // 2, k, j % 2)),
        out_shape=_sds((4, d, ns), F32), out_spec=pl.BlockSpec((None, td, ns), lambda i, j, k: (j, i, 0)),
        acc_shape=(td, ns))

    tmh = _tile(m, 512, 8)
    dh = _matmul(
        f"{tag}_dh", "nt", dgu, wgu.arr, grid=(m // tmh, 1, 4),
        a_spec=pl.BlockSpec((None, tmh, ns), lambda i, j, k: (k // 2, i, k % 2)),
        b_spec=_wspec(wgu, d, ns, lambda i, j, k: 0, lambda i, j, k: k),
        out_shape=_sds((m, d), F32), out_spec=pl.BlockSpec((tmh, d), lambda i, j, k: (i, 0)),
        acc_shape=(tmh, d))

    dx, dxb, dgain = _rmsnorm_bwd(f"{tag}_dnorm", x, dh, gain, dy)
    return dx, dxb, dgain, dwgu, dwdown


SB_KEY_BLOCK = 256


def _split2(x):
    hi = x.astype(BF16)
    return hi, (x - hi.astype(F32)).astype(BF16)


def _dot2(parts, mat):
    return jnp.dot(parts[0], mat, preferred_element_type=F32) + jnp.dot(parts[1], mat, preferred_element_type=F32)


def _sb_logits(q, k, scale):
    z = lax.dot_general(q, k, _NT, preferred_element_type=F32) * scale
    lp = jnp.minimum(z, 0.0) - jnp.log1p(jnp.exp(-jnp.abs(z)))
    return lp, lp - z


def _sb_fwd(name, qkv, n_heads):
    length = qkv.shape[0]
    kb = _tile(length, SB_KEY_BLOCK, LANES)
    tq = kb
    scale = 1.0 / math.sqrt(HEAD_DIM)

    def body(q_ref, k_ref, v_ref, o_ref, tot_ref):
        qi = pl.program_id(1)
        q = q_ref[...].astype(BF16)
        row = lax.broadcasted_iota(jnp.int32, (tq, kb), 0)
        col = lax.broadcasted_iota(jnp.int32, (tq, kb), 1)
        later = (lax.broadcasted_iota(jnp.int32, (kb, kb), 0) > lax.broadcasted_iota(jnp.int32, (kb, kb), 1)).astype(BF16)

        def block(kbi, carry, masked):
            c, acc = carry
            ks = pl.multiple_of(kbi * kb, kb)
            k = k_ref[pl.ds(ks, kb), :].astype(BF16)
            v = v_ref[pl.ds(ks, kb), :].astype(BF16)
            lp, ln = _sb_logits(q, k, scale)
            if masked:
                valid = col < row
                ln = jnp.where(valid, ln, 0.0)
            w = jnp.exp(lp + (c + _dot2(_split2(ln), later)))
            if masked:
                w = jnp.where(valid, w, 0.0)
            acc = acc + jnp.dot(w.astype(BF16), v, preferred_element_type=F32)
            return c + jnp.sum(ln, axis=1, keepdims=True), acc

        carry = block(qi, (jnp.zeros((tq, 1), F32), jnp.zeros((tq, HEAD_DIM), F32)), True)
        carry = lax.fori_loop(0, qi, lambda i, cr: block(qi - 1 - i, cr, False), carry)
        o_ref[...] = carry[1]
        tot_ref[...] = jnp.broadcast_to(carry[0], (tq, HEAD_DIM))

    h = n_heads
    qblk = pl.BlockSpec((tq, HEAD_DIM), lambda hh, i: (i, hh))
    return pl.pallas_call(
        body, name=name, grid=(h, length // tq),
        in_specs=[qblk,
                  pl.BlockSpec((length, HEAD_DIM), lambda hh, i: (0, h + hh)),
                  pl.BlockSpec((length, HEAD_DIM), lambda hh, i: (0, 2 * h + hh))],
        out_specs=(qblk, qblk),
        out_shape=(_sds((length, h * HEAD_DIM), F32), _sds((length, h * HEAD_DIM), F32)),
        compiler_params=_params(("parallel", "arbitrary")),
    )(qkv, qkv, qkv)


def _sb_bwd(name, qkv, tot, do, n_heads):
    length = qkv.shape[0]
    kb = _tile(length, SB_KEY_BLOCK, LANES)
    tq = kb
    scale = 1.0 / math.sqrt(HEAD_DIM)

    def body(q_ref, k_ref, v_ref, tot_ref, do_ref, dq_ref, dk_ref, dv_ref):
        qi = pl.program_id(1)

        @pl.when(qi == 0)
        def _():
            dk_ref[...] = jnp.zeros_like(dk_ref)
            dv_ref[...] = jnp.zeros_like(dv_ref)

        q = q_ref[...].astype(BF16)
        dob = do_ref[...].astype(BF16)
        tot = tot_ref[:, 0:1]
        row = lax.broadcasted_iota(jnp.int32, (tq, kb), 0)
        col = lax.broadcasted_iota(jnp.int32, (tq, kb), 1)
        jj = lax.broadcasted_iota(jnp.int32, (kb, kb), 0)
        ss = lax.broadcasted_iota(jnp.int32, (kb, kb), 1)
        later = (jj > ss).astype(BF16)
        before = (jj < ss).astype(BF16)

        def block(kbi, carry, masked):
            pl_, pe, dq = carry
            ks = pl.multiple_of(kbi * kb, kb)
            k = k_ref[pl.ds(ks, kb), :].astype(BF16)
            v = v_ref[pl.ds(ks, kb), :].astype(BF16)
            lp, ln_raw = _sb_logits(q, k, scale)
            ln = ln_raw
            if masked:
                valid = col < row
                ln = jnp.where(valid, ln_raw, 0.0)
            tb = jnp.sum(ln, axis=1, keepdims=True)
            w = jnp.exp(lp + ((tot - pl_ - tb) + _dot2(_split2(ln), later)))
            if masked:
                w = jnp.where(valid, w, 0.0)
            e = w * lax.dot_general(dob, v, _NT, preferred_element_type=F32)
            dv_ref[pl.ds(ks, kb), :] += lax.dot_general(w.astype(BF16), dob, _TN, preferred_element_type=F32)
            dz = e * jnp.exp(ln_raw) - jnp.exp(lp) * (pe + _dot2(_split2(e), before))
            if masked:
                dz = jnp.where(valid, dz, 0.0)
            dzb = (dz * scale).astype(BF16)
            dq = dq + jnp.dot(dzb, k, preferred_element_type=F32)
            dk_ref[pl.ds(ks, kb), :] += lax.dot_general(dzb, q, _TN, preferred_element_type=F32)
            return pl_ + tb, pe + jnp.sum(e, axis=1, keepdims=True), dq

        carry = (jnp.zeros((tq, 1), F32), jnp.zeros((tq, 1), F32), jnp.zeros((tq, HEAD_DIM), F32))
        carry = lax.fori_loop(0, qi, lambda i, cr: block(i, cr, False), carry)
        carry = block(qi, carry, True)
        dq_ref[...] = carry[2]

    h = n_heads
    qblk = pl.BlockSpec((tq, HEAD_DIM), lambda hh, i: (i, hh))
    full = pl.BlockSpec((length, HEAD_DIM), lambda hh, i: (0, hh))
    out = _sds((length, h * HEAD_DIM), F32)
    return pl.pallas_call(
        body, name=name, grid=(h, length // tq),
        in_specs=[qblk,
                  pl.BlockSpec((length, HEAD_DIM), lambda hh, i: (0, h + hh)),
                  pl.BlockSpec((length, HEAD_DIM), lambda hh, i: (0, 2 * h + hh)),
                  qblk, qblk],
        out_specs=(qblk, full, full), out_shape=(out, out, out),
        compiler_params=_params(("parallel", "arbitrary")),
    )(qkv, qkv, qkv, tot, do)


def _head_rms(xh):
    r = lax.rsqrt(jnp.mean(xh * xh, axis=-1, keepdims=True) + EPS)
    return xh * r, r


def _mem_fwd(name, qsrc, qcol0, kv, gq, gk):
    length = qsrc.shape[0]
    mm, mw2 = kv.shape
    mw = mw2 // 2
    nh = mw // HEAD_DIM
    tq = _tile(length, 512, 8)
    inv = 1.0 / math.sqrt(HEAD_DIM)
    assert qcol0 % mw == 0

    def body(q_ref, kv_ref, gq_ref, gk_ref, o_ref):
        for hh in range(nh):
            sl = slice(hh * HEAD_DIM, (hh + 1) * HEAD_DIM)
            qn = _head_rms(q_ref[:, sl])[0] * gq_ref[...]
            kn = _head_rms(kv_ref[:, sl])[0] * gk_ref[...]
            vh = kv_ref[:, mw + hh * HEAD_DIM:mw + (hh + 1) * HEAD_DIM].astype(BF16)
            s = lax.dot_general(qn.astype(BF16), kn.astype(BF16), _NT, preferred_element_type=F32) * inv
            p = jnp.exp(s - jnp.max(s, axis=-1, keepdims=True))
            p = p / jnp.sum(p, axis=-1, keepdims=True)
            o_ref[:, sl] = jnp.dot(p.astype(BF16), vh, preferred_element_type=F32)

    vec = pl.BlockSpec((1, HEAD_DIM), lambda i: (0, 0))
    return pl.pallas_call(
        body, name=name, grid=(length // tq,),
        in_specs=[pl.BlockSpec((tq, mw), lambda i: (i, qcol0 // mw)),
                  pl.BlockSpec((mm, mw2), lambda i: (0, 0)), vec, vec],
        out_specs=pl.BlockSpec((tq, mw), lambda i: (i, 0)),
        out_shape=_sds((length, mw), F32),
        compiler_params=_params(("parallel",)),
    )(qsrc, kv, gq, gk)


def _mem_bwd(name, qsrc, qcol0, kv, gq, gk, dsrc, docol0):
    length = qsrc.shape[0]
    mm, mw2 = kv.shape
    mw = mw2 // 2
    nh = mw // HEAD_DIM
    tq = _tile(length, 512, 8)
    nsteps = length // tq
    inv = 1.0 / math.sqrt(HEAD_DIM)

    def body(q_ref, kv_ref, gq_ref, gk_ref, do_ref, dq_ref, dkv_ref, dgq_ref, dgk_ref):
        i = pl.program_id(0)

        @pl.when(i == 0)
        def _():
            dkv_ref[...] = jnp.zeros_like(dkv_ref)
            dgq_ref[...] = jnp.zeros_like(dgq_ref)

        gqv = gq_ref[...]
        gkv = gk_ref[...]
        for hh in range(nh):
            sl = slice(hh * HEAD_DIM, (hh + 1) * HEAD_DIM)
            slv = slice(mw + hh * HEAD_DIM, mw + (hh + 1) * HEAD_DIM)
            qhat, rq = _head_rms(q_ref[:, sl])
            qn = (qhat * gqv).astype(BF16)
            kn = (_head_rms(kv_ref[:, sl])[0] * gkv).astype(BF16)
            vh = kv_ref[:, slv].astype(BF16)
            dob = do_ref[:, sl].astype(BF16)
            s = lax.dot_general(qn, kn, _NT, preferred_element_type=F32) * inv
            p = jnp.exp(s - jnp.max(s, axis=-1, keepdims=True))
            p = p / jnp.sum(p, axis=-1, keepdims=True)
            dp = lax.dot_general(dob, vh, _NT, preferred_element_type=F32)
            ds = (p * (dp - jnp.sum(dp * p, axis=-1, keepdims=True)) * inv).astype(BF16)
            dqn = jnp.dot(ds, kn, preferred_element_type=F32)
            dkv_ref[:, sl] += lax.dot_general(ds, qn, _TN, preferred_element_type=F32)
            dkv_ref[:, slv] += lax.dot_general(p.astype(BF16), dob, _TN, preferred_element_type=F32)
            dgq_ref[...] += jnp.sum(dqn * qhat, axis=0, keepdims=True)
            dqh = dqn * gqv
            dq_ref[:, sl] = rq * (dqh - qhat * jnp.mean(dqh * qhat, axis=-1, keepdims=True))

        @pl.when(i == nsteps - 1)
        def _():
            dgk = jnp.zeros((1, HEAD_DIM), F32)
            for hh in range(nh):
                sl = slice(hh * HEAD_DIM, (hh + 1) * HEAD_DIM)
                khat, rk = _head_rms(kv_ref[:, sl])
                dkn = dkv_ref[:, sl]
                dgk = dgk + jnp.sum(dkn * khat, axis=0, keepdims=True)
                dkh = dkn * gkv
                dkv_ref[:, sl] = rk * (dkh - khat * jnp.mean(dkh * khat, axis=-1, keepdims=True))
            dgk_ref[...] = dgk

    vec = pl.BlockSpec((1, HEAD_DIM), lambda i: (0, 0))
    kvs = pl.BlockSpec((mm, mw2), lambda i: (0, 0))
    blk = pl.BlockSpec((tq, mw), lambda i: (i, 0))
    return pl.pallas_call(
        body, name=name, grid=(nsteps,),
        in_specs=[pl.BlockSpec((tq, mw), lambda i: (i, qcol0 // mw)), kvs, vec, vec,
                  pl.BlockSpec((tq, mw), lambda i: (i, docol0 // mw))],
        out_specs=(blk, kvs, vec, vec),
        out_shape=(_sds((length, mw), F32), _sds((mm, mw2), F32), _sds((1, HEAD_DIM), F32), _sds((1, HEAD_DIM), F32)),
        compiler_params=_params(("arbitrary",)),
    )(qsrc, kv, gq, gk, dsrc)


S5_TILE_GROUPS = LANES // S5_GROUP
S5_TILE_STATES = S5_TILE_GROUPS * S5_STATE
S5_SCAN_ROWS = 64
_HI = lax.Precision.HIGHEST


def _dotf(a, b, dims=None):
    if dims is None:
        return jnp.dot(a, b, precision=_HI, preferred_element_type=F32)
    return lax.dot_general(a, b, dims, precision=_HI, preferred_element_type=F32)


def _s5_prep(log_dt, a_re, a_im, b_re, b_im, c_re, c_im):
    g, n = a_re.shape
    nt = g // S5_TILE_GROUPS
    dt = jnp.exp(log_dt)[:, None]
    mag = jnp.exp(a_re * dt)
    ab_re = mag * jnp.cos(a_im * dt)
    ab_im = mag * jnp.sin(a_im * dt)
    den = a_re * a_re + a_im * a_im
    num_re = ab_re - 1.0
    co_re = (num_re * a_re + ab_im * a_im) / den
    co_im = (ab_im * a_re - num_re * a_im) / den
    bb_re = co_re[..., None] * b_re - co_im[..., None] * b_im
    bb_im = co_re[..., None] * b_im + co_im[..., None] * b_re
    eye = jnp.eye(S5_TILE_GROUPS, dtype=F32)

    def blk_b(bb):
        t = bb.reshape(nt, S5_TILE_GROUPS, n, S5_GROUP).transpose(0, 1, 3, 2)
        return jnp.einsum("jgcn,gh->jgchn", t, eye).reshape(nt, LANES, S5_TILE_STATES)

    def blk_c(cc):
        t = cc.reshape(nt, S5_TILE_GROUPS, S5_GROUP, n).transpose(0, 1, 3, 2)
        return jnp.einsum("jgnc,gh->jgnhc", t, eye).reshape(nt, S5_TILE_STATES, LANES)

    return (ab_re.reshape(1, g * n), ab_im.reshape(1, g * n), blk_b(bb_re), blk_b(bb_im), blk_c(c_re), blk_c(c_im))


def _s5_bu(name, usrc, bblk_re, bblk_im):
    length = usrc.shape[0]
    nt = bblk_re.shape[0]
    tm = _tile(length, 512, 8)

    def body(u_ref, br_ref, bi_ref, or_ref, oi_ref):
        u = u_ref[...]
        or_ref[...] = _dotf(u, br_ref[...])
        oi_ref[...] = _dotf(u, bi_ref[...])

    bspec = pl.BlockSpec((None, LANES, S5_TILE_STATES), lambda i, j: (j, 0, 0))
    ospec = pl.BlockSpec((tm, S5_TILE_STATES), lambda i, j: (i, j))
    out = _sds((length, nt * S5_TILE_STATES), F32)
    return pl.pallas_call(
        body, name=name, grid=(length // tm, nt),
        in_specs=[pl.BlockSpec((tm, LANES), lambda i, j: (i, j)), bspec, bspec],
        out_specs=(ospec, ospec), out_shape=(out, out),
        compiler_params=_params(("parallel", "parallel")),
    )(usrc, bblk_re, bblk_im)


def _scan_rows(hr, hi, ar, ai, reverse):
    t = hr.shape[0]
    rows = lax.broadcasted_iota(jnp.int32, hr.shape, 0)
    d = 1
    while d < t:
        if reverse:
            sr = jnp.where(rows < t - d, pltpu.roll(hr, t - d, 0), 0.0)
            si = jnp.where(rows < t - d, pltpu.roll(hi, t - d, 0), 0.0)
        else:
            sr = jnp.where(rows >= d, pltpu.roll(hr, d, 0), 0.0)
            si = jnp.where(rows >= d, pltpu.roll(hi, d, 0), 0.0)
        hr, hi = hr + ar * sr - ai * si, hi + ar * si + ai * sr
        ar, ai = ar * ar - ai * ai, 2.0 * ar * ai
        d *= 2
    return hr, hi


def _s5_scan(name, x_re, x_im, a_re, a_im, reverse=False, h=None, bu=None):
    length, width = x_re.shape
    t = _tile(length, S5_SCAN_ROWS, 8)
    nsteps = length // t
    with_sum = h is not None

    def body(*refs):
        if with_sum:
            (xr_ref, xi_ref, ar_ref, ai_ref, hr_ref, hi_ref, br_ref, bi_ref,
             or_ref, oi_ref, sr_ref, si_ref, cr, ci, pr, pi, accr, acci) = refs
        else:
            xr_ref, xi_ref, ar_ref, ai_ref, or_ref, oi_ref, cr, ci, pr, pi = refs
        step = pl.program_id(0)
        ar = ar_ref[...]
        ai = ai_ref[...]
        edge = 0 if reverse else t - 1
        last = t - 1 if reverse else 0

        @pl.when(step == 0)
        def _():
            cr[...] = jnp.zeros_like(cr)
            ci[...] = jnp.zeros_like(ci)
            rows = lax.broadcasted_iota(jnp.int32, (t, width), 0)
            seed_r = jnp.where(rows == last, ar, 0.0)
            seed_i = jnp.where(rows == last, ai, 0.0)
            p_r, p_i = _scan_rows(seed_r, seed_i, ar, ai, reverse)
            pr[...] = p_r
            pi[...] = p_i
            if with_sum:
                accr[...] = jnp.zeros_like(accr)
                acci[...] = jnp.zeros_like(acci)

        hr, hi = _scan_rows(xr_ref[...], xi_ref[...], ar, ai, reverse)
        c_r = cr[...]
        c_i = ci[...]
        p_r = pr[...]
        p_i = pi[...]
        hr = hr + p_r * c_r - p_i * c_i
        hi = hi + p_r * c_i + p_i * c_r
        or_ref[...] = hr
        oi_ref[...] = hi
        cr[...] = hr[edge:edge + 1, :]
        ci[...] = hi[edge:edge + 1, :]
        if with_sum:
            wr = hr_ref[...] - br_ref[...]
            wi = hi_ref[...] - bi_ref[...]
            accr[...] += (wr * hr + wi * hi).reshape(t // 8, 8, width).sum(axis=0)
            acci[...] += (wr * hi - wi * hr).reshape(t // 8, 8, width).sum(axis=0)

            @pl.when(step == nsteps - 1)
            def _():
                sr_ref[...] = jnp.sum(accr[...], axis=0, keepdims=True)
                si_ref[...] = jnp.sum(acci[...], axis=0, keepdims=True)

    if reverse:
        blk = pl.BlockSpec((t, width), lambda s: (nsteps - 1 - s, 0))
    else:
        blk = pl.BlockSpec((t, width), lambda s: (s, 0))
    vec = pl.BlockSpec((1, width), lambda s: (0, 0))
    full = _sds((length, width), F32)
    row = _sds((1, width), F32)
    scratch = [pltpu.VMEM((1, width), F32), pltpu.VMEM((1, width), F32),
               pltpu.VMEM((t, width), F32), pltpu.VMEM((t, width), F32)]
    if with_sum:
        return pl.pallas_call(
            body, name=name, grid=(nsteps,),
            in_specs=[blk, blk, vec, vec, blk, blk, blk, blk],
            out_specs=(blk, blk, vec, vec), out_shape=(full, full, row, row),
            scratch_shapes=scratch + [pltpu.VMEM((8, width), F32), pltpu.VMEM((8, width), F32)],
            compiler_params=_params(("arbitrary",)),
        )(x_re, x_im, a_re, a_im, h[0], h[1], bu[0], bu[1])
    return pl.pallas_call(
        body, name=name, grid=(nsteps,),
        in_specs=[blk, blk, vec, vec], out_specs=(blk, blk), out_shape=(full, full),
        scratch_shapes=scratch,
        compiler_params=_params(("arbitrary",)),
    )(x_re, x_im, a_re, a_im)


_GELU_C = math.sqrt(2.0 / math.pi)


def _gelu(y):
    return 0.5 * y * (1.0 + jnp.tanh(_GELU_C * (y + 0.044715 * y * y * y)))


def _gelu_grad(y):
    th = jnp.tanh(_GELU_C * (y + 0.044715 * y * y * y))
    return 0.5 * (1.0 + th) + 0.5 * y * (1.0 - th * th) * _GELU_C * (1.0 + 3 * 0.044715 * y * y)


def _s5_out(name, h_re, h_im, cblk_re, cblk_im, usrc, dskip):
    length = h_re.shape[0]
    nt = cblk_re.shape[0]
    tm = _tile(length, 512, 8)

    def body(hr_ref, hi_ref, cr_ref, ci_ref, u_ref, d_ref, y_ref, y2_ref):
        y = _dotf(hr_ref[...], cr_ref[...]) - _dotf(hi_ref[...], ci_ref[...]) + d_ref[...] * u_ref[...]
        y_ref[...] = y
        y2_ref[...] = _gelu(y)

    hspec = pl.BlockSpec((tm, S5_TILE_STATES), lambda i, j: (i, j))
    cspec = pl.BlockSpec((None, S5_TILE_STATES, LANES), lambda i, j: (j, 0, 0))
    uspec = pl.BlockSpec((tm, LANES), lambda i, j: (i, j))
    out = _sds((length, nt * LANES), F32)
    return pl.pallas_call(
        body, name=name, grid=(length // tm, nt),
        in_specs=[hspec, hspec, cspec, cspec, uspec, pl.BlockSpec((1, LANES), lambda i, j: (0, j))],
        out_specs=(uspec, uspec), out_shape=(out, out),
        compiler_params=_params(("parallel", "parallel")),
    )(h_re, h_im, cblk_re, cblk_im, usrc, dskip)


def _s5_fwd(tag, usrc, prep, dskip, wglu):
    ab_re, ab_im, bb_re, bb_im, cb_re, cb_im = prep
    bu = _s5_bu(f"{tag}_bu", usrc, bb_re, bb_im)
    hs = _s5_scan(f"{tag}_scan", bu[0], bu[1], ab_re, ab_im)
    y, y2 = _s5_out(f"{tag}_out", hs[0], hs[1], cb_re, cb_im, usrc, dskip)

    def glu_epilogue(acc, ex, outs):
        outs[0][...] = acc
        outs[1][...] = ex[0][...] * jax.nn.sigmoid(acc)

    gl, tok = _mm_nn(f"{tag}_glu", y2, wglu, extras=(y2,), epilogue=glu_epilogue, n_out=2)
    return tok, (bu, hs, y, y2, gl)


def _s5_bwd(tag, usrc, prep, dskip, wglu, saved, dsrc):
    ab_re, ab_im, bb_re, bb_im, cb_re, cb_im = prep
    bu, hs, y, y2, gl = saved
    length = usrc.shape[0]
    tw = y.shape[1]
    nt = bb_re.shape[0]

    tme = _tile(length, 512, 8)

    def gate_body(dt_ref, y2_ref, gl_ref, dgl_ref, dy2_ref):
        s = jax.nn.sigmoid(gl_ref[...])
        dt = dt_ref[...]
        dgl_ref[...] = (dt * y2_ref[...] * s * (1.0 - s)).astype(BF16)
        dy2_ref[...] = dt * s

    espec = pl.BlockSpec((tme, tw), lambda i: (i, 0))
    dgl, dy2a = pl.pallas_call(
        gate_body, name=f"{tag}_dgate", grid=(length // tme,),
        in_specs=[espec, espec, espec], out_specs=(espec, espec),
        out_shape=(_sds((length, tw), BF16), _sds((length, tw), F32)),
        compiler_params=_params(("parallel",)),
    )(dsrc, y2, gl)

    dwglu = _mm_tn(f"{tag}_dwglu", y2, dgl)

    def dy_epilogue(acc, ex, outs):
        outs[0][...] = (ex[0][...] + acc) * _gelu_grad(ex[1][...])

    dy = _mm_nt(f"{tag}_dy", dgl, wglu, extras=(dy2a, y), epilogue=dy_epilogue)

    tmh = _tile(length, 512, 8)

    def dh_body(dy_ref, cr_ref, ci_ref, gr_ref, gi_ref):
        dyv = dy_ref[...]
        gr_ref[...] = _dotf(dyv, cr_ref[...], _NT)
        gi_ref[...] = -_dotf(dyv, ci_ref[...], _NT)

    hspec = pl.BlockSpec((tmh, S5_TILE_STATES), lambda i, j: (i, j))
    cspec = pl.BlockSpec((None, S5_TILE_STATES, LANES), lambda i, j: (j, 0, 0))
    uspec = pl.BlockSpec((tmh, LANES), lambda i, j: (i, j))
    wide = _sds((length, nt * S5_TILE_STATES), F32)
    g_re, g_im = pl.pallas_call(
        dh_body, name=f"{tag}_dh", grid=(length // tmh, nt),
        in_specs=[uspec, cspec, cspec], out_specs=(hspec, hspec), out_shape=(wide, wide),
        compiler_params=_params(("parallel", "parallel")),
    )(dy, cb_re, cb_im)

    lam_re, lam_im, s_re, s_im = _s5_scan(f"{tag}_rscan", g_re, g_im, ab_re, -ab_im, reverse=True, h=hs, bu=bu)
    den = ab_re * ab_re + ab_im * ab_im
    da_re = (ab_re * s_re - ab_im * s_im) / den
    da_im = (ab_re * s_im + ab_im * s_re) / den

    def du_body(lr_ref, li_ref, br_ref, bi_ref, dy_ref, d_ref, du_ref):
        du_ref[...] = (_dotf(lr_ref[...], br_ref[...], _NT) + _dotf(li_ref[...], bi_ref[...], _NT)
                       + dy_ref[...] * d_ref[...])

    bspec = pl.BlockSpec((None, LANES, S5_TILE_STATES), lambda i, j: (j, 0, 0))
    dvec = pl.BlockSpec((1, LANES), lambda i, j: (0, j))
    du = pl.pallas_call(
        du_body, name=f"{tag}_du", grid=(length // tmh, nt),
        in_specs=[hspec, hspec, bspec, bspec, uspec, dvec], out_specs=uspec,
        out_shape=_sds((length, tw), F32),
        compiler_params=_params(("parallel", "parallel")),
    )(lam_re, lam_im, bb_re, bb_im, dy, dskip)

    tkm = _tile(length, 512, 8)
    nk = length // tkm

    def dpar_body(u_ref, dy_ref, hr_ref, hi_ref, lr_ref, li_ref, dbr_ref, dbi_ref, dcr_ref, dci_ref, dd_ref):
        k = pl.program_id(1)

        @pl.when(k == 0)
        def _():
            for ref in (dbr_ref, dbi_ref, dcr_ref, dci_ref, dd_ref):
                ref[...] = jnp.zeros_like(ref)

        u = u_ref[...]
        dyv = dy_ref[...]
        dbr_ref[...] += _dotf(u, lr_ref[...], _TN)
        dbi_ref[...] += _dotf(u, li_ref[...], _TN)
        dcr_ref[...] += _dotf(hr_ref[...], dyv, _TN)
        dci_ref[...] -= _dotf(hi_ref[...], dyv, _TN)
        dd_ref[...] += jnp.sum(dyv * u, axis=0, keepdims=True)

    kspec_u = pl.BlockSpec((tkm, LANES), lambda j, k: (k, j))
    kspec_h = pl.BlockSpec((tkm, S5_TILE_STATES), lambda j, k: (k, j))
    ob = pl.BlockSpec((None, LANES, S5_TILE_STATES), lambda j, k: (j, 0, 0))
    oc = pl.BlockSpec((None, S5_TILE_STATES, LANES), lambda j, k: (j, 0, 0))
    dbr, dbi, dcr, dci, dd = pl.pallas_call(
        dpar_body, name=f"{tag}_dpar", grid=(nt, nk),
        in_specs=[kspec_u, kspec_u, kspec_h, kspec_h, kspec_h, kspec_h],
        out_specs=(ob, ob, oc, oc, pl.BlockSpec((1, LANES), lambda j, k: (0, j))),
        out_shape=(_sds(bb_re.shape, F32), _sds(bb_re.shape, F32), _sds(cb_re.shape, F32), _sds(cb_re.shape, F32),
                   _sds((1, tw), F32)),
        compiler_params=_params(("parallel", "arbitrary")),
    )(usrc, dy, hs[0], hs[1], lam_re, lam_im)
    return du, (da_re, da_im, dbr, dbi, dcr, dci), dd, dwglu


def _loss_head(name, y, target):
    m, d = y.shape
    tm = _tile(m, 256, 8)
    nsteps = m // tm

    def body(y_ref, t_ref, loss_ref, dy_ref, dyb_ref, acc_ref):
        i = pl.program_id(0)
        diff = y_ref[...] - t_ref[...]
        dy = diff * (1.0 / d)
        dy_ref[...] = dy
        dyb_ref[...] = dy.astype(BF16)
        sq = (diff * diff).reshape(tm // 8, 8, d).sum(axis=0)

        @pl.when(i == 0)
        def _():
            acc_ref[...] = sq

        @pl.when(i > 0)
        def _():
            acc_ref[...] += sq

        @pl.when(i == nsteps - 1)
        def _():
            loss_ref[...] = jnp.full(loss_ref.shape, jnp.sum(acc_ref[...]) * (0.5 / d), F32)

    row = pl.BlockSpec((tm, d), lambda i: (i, 0))
    return pl.pallas_call(
        body, name=name, grid=(nsteps,),
        in_specs=[row, row], out_specs=(pl.BlockSpec((8, LANES), lambda i: (0, 0)), row, row),
        out_shape=(_sds((8, LANES), F32), _sds((m, d), F32), _sds((m, d), BF16)),
        scratch_shapes=[pltpu.VMEM((8, d), F32)],
        compiler_params=_params(("arbitrary",)),
    )(y, target)


def _adamw(name, w, g, m, v):
    shape = w.shape
    total = math.prod(shape)
    if shape[-1] % LANES and total % LANES == 0:
        view = (total // LANES, LANES)
    else:
        view = (total // shape[-1], shape[-1])
    rows, cols = view
    tr = rows
    if rows * cols * 4 > (1 << 20) and rows % 16 == 0:
        tr = _row_tile(rows, cols)
    c1 = 1.0 / (1.0 - ADAM_B1 ** ADAM_STEP)
    c2 = 1.0 / (1.0 - ADAM_B2 ** ADAM_STEP)

    def body(w_ref, g_ref, m_ref, v_ref, d_ref, nm_ref, nv_ref):
        gv = g_ref[...]
        nm = ADAM_B1 * m_ref[...] + (1.0 - ADAM_B1) * gv
        nv = ADAM_B2 * v_ref[...] + (1.0 - ADAM_B2) * gv * gv
        nm_ref[...] = nm
        nv_ref[...] = nv
        d_ref[...] = -ADAM_LR * ((nm * c1) / (jnp.sqrt(nv * c2) + ADAM_EPS) + ADAM_WD * w_ref[...])

    spec = pl.BlockSpec((tr, cols), lambda i: (i, 0))
    out = _sds(view, F32)
    res = pl.pallas_call(
        body, name=name, grid=(rows // tr,),
        in_specs=[spec] * 4, out_specs=(spec,) * 3, out_shape=(out,) * 3,
        compiler_params=_params(("parallel",)),
    )(w.reshape(view), g.reshape(view), m.reshape(view), v.reshape(view))
    return tuple(r.reshape(shape) for r in res)


def _sum_leading(name, x):
    k, rows, cols = x.shape
    tr = _row_tile(rows, cols)

    def body(x_ref, o_ref):
        acc = x_ref[0].astype(F32)
        for j in range(1, k):
            acc = acc + x_ref[j].astype(F32)
        o_ref[...] = acc

    return pl.pallas_call(
        body, name=name, grid=(rows // tr,),
        in_specs=[pl.BlockSpec((k, tr, cols), lambda i: (0, i, 0))],
        out_specs=pl.BlockSpec((tr, cols), lambda i: (i, 0)), out_shape=_sds((rows, cols), F32),
        compiler_params=_params(("parallel",)),
    )(x)


def _add_own_half(name, g, recv):
    n, s, r, c = g.shape
    rows = (n // 2) * s * r
    tr = _row_tile(rows, c)
    core = lax.axis_index("c").astype(jnp.int32).reshape(1)

    def body(core_ref, g_ref, r_ref, o_ref):
        o_ref[...] = (g_ref[...] + r_ref[...]).astype(BF16)

    flat = pl.BlockSpec((tr, c), lambda i, core_ref: (i, 0))
    out = pl.pallas_call(
        body, name=name,
        grid_spec=pltpu.PrefetchScalarGridSpec(
            num_scalar_prefetch=1, grid=(rows // tr,),
            in_specs=[pl.BlockSpec((None, tr, c), lambda i, core_ref: (core_ref[0], i, 0)), flat],
            out_specs=flat),
        out_shape=_sds((rows, c), BF16), compiler_params=_params(("parallel",)),
    )(core, g.reshape(2, rows, c), recv.reshape(rows, c))
    return out.reshape(n // 2, s, r, c)


_ANY = pl.BlockSpec(memory_space=pl.ANY)


def _place():
    x, y, c = lax.axis_index("x"), lax.axis_index("y"), lax.axis_index("c")
    return x, y, c, [(1 - x, y), (x, 1 - y), (1 - x, 1 - y)]


def _remote(src, dst, send_sems, recv_sems, k, to):
    return pltpu.make_async_remote_copy(src_ref=src, dst_ref=dst, send_sem=send_sems.at[k], recv_sem=recv_sems.at[k],
                                        device_id=to, device_id_type=MESH)


def _comm_call(name, body, ins, out_shapes, n_remote, n_local):
    scratch = [pltpu.SemaphoreType.DMA((n_remote,)), pltpu.SemaphoreType.DMA((n_remote,))]
    if n_local:
        scratch.append(pltpu.SemaphoreType.DMA((n_local,)))
    return pl.pallas_call(
        body, name=name, in_specs=[_ANY] * len(ins), out_specs=[_ANY] * len(out_shapes), out_shape=out_shapes,
        scratch_shapes=scratch,
    )(*ins)


def _gather_chips(name, shards):
    nt = len(shards)

    def body(*refs):
        ins, outs = refs[:nt], refs[nt:2 * nt]
        send_sems, recv_sems, local_sems = refs[2 * nt:]
        x, y, c, chips = _place()
        me = 2 * x + y
        sibling = (x, y, 1 - c)
        local, first, passed = [], [], []
        for t in range(nt):
            half = ins[t].shape[0] // 2
            mine = pl.ds(c * half, half)
            local.append(pltpu.make_async_copy(ins[t], outs[t].at[me], local_sems.at[t]))
            local[-1].start()
            for j, chip in enumerate(chips):
                first.append(_remote(ins[t].at[mine], outs[t].at[me, mine], send_sems, recv_sems, 6 * t + j, (*chip, c)))
                first[-1].start()
        for t in range(nt):
            half = ins[t].shape[0] // 2
            mine = pl.ds(c * half, half)
            for j, chip in enumerate(chips):
                got = outs[t].at[2 * chip[0] + chip[1], mine]
                _remote(ins[t].at[mine], got, send_sems, recv_sems, 6 * t + j, (x, y, c)).wait_recv()
                passed.append(_remote(got, got, send_sems, recv_sems, 6 * t + 3 + j, sibling))
                passed[-1].start()
        for t in range(nt):
            half = ins[t].shape[0] // 2
            theirs = pl.ds((1 - c) * half, half)
            for j, chip in enumerate(chips):
                got = outs[t].at[2 * chip[0] + chip[1], theirs]
                _remote(ins[t].at[theirs], got, send_sems, recv_sems, 6 * t + 3 + j, (x, y, c)).wait_recv()
        for cp in first + passed:
            cp.wait_send()
        for cp in local:
            cp.wait()

    return _comm_call(name, body, shards, [_sds((N_CHIPS,) + s.shape, s.dtype) for s in shards], 6 * nt, nt)


def _other_halves_to_sibling(name, grads):
    nt = len(grads)

    def body(*refs):
        ins, outs = refs[:nt], refs[nt:2 * nt]
        send_sems, recv_sems = refs[2 * nt:]
        x, y, c, _ = _place()
        copies = []
        for t in range(nt):
            half = ins[t].shape[0] // 2
            copies.append(_remote(ins[t].at[pl.ds((1 - c) * half, half)], outs[t], send_sems, recv_sems, t, (x, y, 1 - c)))
            copies[-1].start()
        for cp in copies:
            cp.wait()

    return _comm_call(name, body, grads, [_sds((g.shape[0] // 2,) + g.shape[1:], g.dtype) for g in grads], nt, 0)


def _scatter_chips(name, parts):
    nt = len(parts)

    def body(*refs):
        ins, outs = refs[:nt], refs[nt:2 * nt]
        send_sems, recv_sems, local_sems = refs[2 * nt:]
        x, y, c, chips = _place()
        me = 2 * x + y
        local, sends = [], []
        for t in range(nt):
            layers = pl.ds(0, ins[t].shape[0])
            local.append(pltpu.make_async_copy(ins[t].at[layers, me], outs[t].at[me], local_sems.at[t]))
            local[-1].start()
            for j, chip in enumerate(chips):
                sends.append(_remote(ins[t].at[layers, 2 * chip[0] + chip[1]], outs[t].at[me], send_sems, recv_sems,
                                     3 * t + j, (*chip, c)))
                sends[-1].start()
        for t in range(nt):
            layers = pl.ds(0, ins[t].shape[0])
            for j, chip in enumerate(chips):
                _remote(ins[t].at[layers, me], outs[t].at[2 * chip[0] + chip[1]], send_sems, recv_sems, 3 * t + j,
                        (x, y, c)).wait_recv()
        for cp in sends:
            cp.wait_send()
        for cp in local:
            cp.wait()

    shapes = [_sds((p.shape[1], p.shape[0]) + p.shape[2:], p.dtype) for p in parts]
    return _comm_call(name, body, parts, shapes, 3 * nt, nt)


def _join_halves(name, halves):
    nt = len(halves)

    def body(*refs):
        ins, outs = refs[:nt], refs[nt:2 * nt]
        send_sems, recv_sems, local_sems = refs[2 * nt:]
        x, y, c, _ = _place()
        local, sends = [], []
        for t in range(nt):
            half = ins[t].shape[0]
            mine = pl.ds(c * half, half)
            local.append(pltpu.make_async_copy(ins[t], outs[t].at[mine], local_sems.at[t]))
            local[-1].start()
            sends.append(_remote(ins[t], outs[t].at[mine], send_sems, recv_sems, t, (x, y, 1 - c)))
            sends[-1].start()
        for t in range(nt):
            half = ins[t].shape[0]
            _remote(ins[t], outs[t].at[pl.ds((1 - c) * half, half)], send_sems, recv_sems, t, (x, y, c)).wait_recv()
        for cp in sends:
            cp.wait_send()
        for cp in local:
            cp.wait()

    shapes = [_sds((2 * hf.shape[0],) + hf.shape[1:], hf.dtype) for hf in halves]
    return _comm_call(name, body, halves, shapes, nt, nt)


def _gather_all(name, block):
    rows, cols = block.shape

    def body(x_ref, out_ref, send_sems, recv_sems, local_sem):
        x, y, c, chips = _place()
        me, sibling = (x, y, c), (x, y, 1 - c)

        def at(px, py, pc):
            return out_ref.at[4 * px + 2 * py + pc]

        def copy(k, blk, to, src=None):
            return _remote(at(*blk) if src is None else src, at(*blk), send_sems, recv_sems, k, to)

        mine = pltpu.make_async_copy(x_ref, at(*me), local_sem)
        mine.start()
        first = [copy(0, me, sibling, src=x_ref)]
        first += [copy(1 + j, me, (*chip, c), src=x_ref) for j, chip in enumerate(chips)]
        for cp in first:
            cp.start()
        passed = [copy(4 + j, (*chip, c), sibling) for j, chip in enumerate(chips)]
        for j, chip in enumerate(chips):
            copy(1 + j, (*chip, c), me).wait_recv()
            passed[j].start()
        copy(0, sibling, me).wait_recv()
        for j, chip in enumerate(chips):
            copy(4 + j, (*chip, 1 - c), me).wait_recv()
        for cp in first + passed:
            cp.wait_send()
        mine.wait()

    return pl.pallas_call(
        body, name=name,
        in_specs=[pl.BlockSpec(memory_space=pltpu.VMEM)], out_specs=pl.BlockSpec(memory_space=pltpu.VMEM),
        out_shape=_sds((8, rows, cols), block.dtype),
        scratch_shapes=[pltpu.SemaphoreType.DMA((7,)), pltpu.SemaphoreType.DMA((7,)), pltpu.SemaphoreType.DMA],
        compiler_params=pltpu.CompilerParams(vmem_limit_bytes=V7X_VMEM_LIMIT),
    )(block)


WEIGHTS = ("ffn1_norm", "ffn1_w_gu", "ffn1_w_down", "mix_norm", "mem_norm", "w_mem_kv", "xq_norm", "xk_norm", "w_out",
           "ffn2_norm", "ffn2_w_gu", "ffn2_w_down", "sb_w_in", "s5_w_in", "s5_log_dt", "s5_a_re", "s5_a_im", "s5_b_re",
           "s5_b_im", "s5_c_re", "s5_c_im", "s5_d", "s5_w_glu")
BIG = ("ffn1_w_gu", "ffn1_w_down", "w_mem_kv", "w_out", "ffn2_w_gu", "ffn2_w_down", "sb_w_in", "s5_w_in", "s5_w_glu")
COLUMN_SHARDED = ("ffn1_w_gu", "ffn2_w_gu", "sb_w_in")
PER_MIXER = ("sb_w_in", "s5_w_in", "s5_w_glu")
SMALL = tuple(n for n in WEIGHTS if n not in BIG)


def _layer_weights(full, i):
    return {name: _W(full[name], i // 2 if name in PER_MIXER else i, name in COLUMN_SHARDED) for name in BIG}


def _row(v):
    return v.reshape(1, -1)


def _layer_fwd(i, x, mem, w, p, s5prep):
    tag = f"l{i}"
    j = i // 2
    x1, sv_ffn1 = _ffn_fwd(f"{tag}_ffn1", x, _row(p["ffn1_norm"][i]), w["ffn1_w_gu"], w["ffn1_w_down"])
    h = _rmsnorm_fwd(f"{tag}_mixnorm", x1, _row(p["mix_norm"][i]))
    if i % 2 == 0:
        proj = _mm_nn(f"{tag}_inproj", h, w["sb_w_in"])
        n_heads = (proj.shape[1] * 3 // 10) // HEAD_DIM
        tok, mix_saved = _sb_fwd(f"{tag}_sb", proj, n_heads)
        tok_w = n_heads * HEAD_DIM
        qcol0 = 3 * tok_w
    else:
        proj = _mm_nn(f"{tag}_inproj", h, w["s5_w_in"])
        tok, mix_saved = _s5_fwd(f"{tag}_s5", proj, s5prep[j], _row(p["s5_d"][j]), w["s5_w_glu"])
        tok_w = tok.shape[1]
        qcol0 = tok_w
    mem_h = _rmsnorm_fwd(f"{tag}_memnorm", mem, _row(p["mem_norm"][i]))
    kv = _mm_nn(f"{tag}_memkv", mem_h, w["w_mem_kv"])
    gq, gk = _row(p["xq_norm"][i]), _row(p["xk_norm"][i])
    cross = _mem_fwd(f"{tag}_mem", proj, qcol0, kv, gq, gk)
    cat = jnp.concatenate([tok, cross], axis=1).astype(BF16)
    x2 = _mm_nn(f"{tag}_outproj", cat, w["w_out"], extras=(x1,), epilogue=_add_residual)
    x3, sv_ffn2 = _ffn_fwd(f"{tag}_ffn2", x2, _row(p["ffn2_norm"][i]), w["ffn2_w_gu"], w["ffn2_w_down"])
    saved = dict(ffn1=sv_ffn1, x1=x1, h=h, proj=proj, mix=mix_saved, mem_h=mem_h, kv=kv, cat=cat, ffn2=sv_ffn2,
                 tok_w=tok_w, qcol0=qcol0)
    return x3, saved


def _layer_bwd(i, sv, mem, w, p, s5prep, dx3, dx3b):
    tag = f"l{i}b"
    j = i // 2
    g = {}
    dx2, dx2b, g["ffn2_norm"], g["ffn2_w_gu"], g["ffn2_w_down"] = _ffn_bwd(
        f"{tag}_ffn2", sv["ffn2"], _row(p["ffn2_norm"][i]), w["ffn2_w_gu"], w["ffn2_w_down"], dx3, dx3b)
    dcat = _mm_nt(f"{tag}_dcat", dx2b, w["w_out"])
    g["w_out"] = _mm_tn(f"{tag}_dwout", sv["cat"], dx2b)
    gq, gk = _row(p["xq_norm"][i]), _row(p["xk_norm"][i])
    tok_w, qcol0 = sv["tok_w"], sv["qcol0"]
    dqm, dkv, g["xq_norm"], g["xk_norm"] = _mem_bwd(f"{tag}_mem", sv["proj"], qcol0, sv["kv"], gq, gk, dcat, tok_w)
    dkvb = dkv.astype(BF16)
    g["w_mem_kv"] = _mm_tn(f"{tag}_dwkv", sv["mem_h"], dkvb)
    dmem_h = _mm_nt(f"{tag}_dmemh", dkvb, w["w_mem_kv"])
    g["mem_norm"] = _rmsnorm_bwd(f"{tag}_dmemnorm", mem, dmem_h, _row(p["mem_norm"][i]))
    if i % 2 == 0:
        dq, dk, dv = _sb_bwd(f"{tag}_sb", sv["proj"], sv["mix"], dcat, tok_w // HEAD_DIM)
        dproj = jnp.concatenate([dq, dk, dv, dqm], axis=1).astype(BF16)
        g["sb_w_in"] = _mm_tn(f"{tag}_dwin", sv["h"], dproj, shards=N_CHIPS)
        dh = _mm_nt(f"{tag}_dh", dproj, w["sb_w_in"])
    else:
        du, g["s5_prep"], g["s5_d"], g["s5_w_glu"] = _s5_bwd(
            f"{tag}_s5", sv["proj"], s5prep[j], _row(p["s5_d"][j]), w["s5_w_glu"], sv["mix"], dcat)
        dproj = jnp.concatenate([du, dqm], axis=1).astype(BF16)
        g["s5_w_in"] = _mm_tn(f"{tag}_dwin", sv["h"], dproj)
        dh = _mm_nt(f"{tag}_dh", dproj, w["s5_w_in"])
    dx1, dx1b, g["mix_norm"] = _rmsnorm_bwd(f"{tag}_dmixnorm", sv["x1"], dh, _row(p["mix_norm"][i]), dx2)
    dx, dxb, g["ffn1_norm"], g["ffn1_w_gu"], g["ffn1_w_down"] = _ffn_bwd(
        f"{tag}_ffn1", sv["ffn1"], _row(p["ffn1_norm"][i]), w["ffn1_w_gu"], w["ffn1_w_down"], dx1, dx1b)
    return dx, dxb, g


def _local_step(x, mem, target, full, p):
    depth = p["ffn1_norm"].shape[0]
    n_s5 = depth // 2
    s5_names = ("s5_log_dt", "s5_a_re", "s5_a_im", "s5_b_re", "s5_b_im", "s5_c_re", "s5_c_im")
    s5prep, s5vjp = [], []
    for j in range(n_s5):
        out, vjp = jax.vjp(_s5_prep, *[p[n][j] for n in s5_names])
        s5prep.append(out)
        s5vjp.append(vjp)
    ws = [_layer_weights(full, i) for i in range(depth)]
    saved = []
    for i in range(depth):
        x, sv = _layer_fwd(i, x, mem, ws[i], p, s5prep)
        saved.append(sv)
    loss, dx, dxb = _loss_head("loss", x, target)
    per_layer = [None] * depth
    for i in reversed(range(depth)):
        dx, dxb, per_layer[i] = _layer_bwd(i, saved[i], mem, ws[i], p, s5prep, dx, dxb)
    grads = {}
    for name in ("ffn1_norm", "mix_norm", "mem_norm", "xq_norm", "xk_norm", "ffn2_norm"):
        grads[name] = jnp.concatenate([per_layer[i][name] for i in range(depth)], axis=0)
    grads["s5_d"] = jnp.concatenate([per_layer[i]["s5_d"] for i in range(1, depth, 2)], axis=0)
    s5g = [s5vjp[j](tuple(per_layer[2 * j + 1]["s5_prep"])) for j in range(n_s5)]
    for k, name in enumerate(s5_names):
        grads[name] = jnp.stack([s5g[j][k] for j in range(n_s5)], axis=0)
    for name in BIG:
        layers = range(depth) if name not in PER_MIXER else range(0 if name == "sb_w_in" else 1, depth, 2)
        per = [per_layer[i][name] for i in layers]
        if name not in COLUMN_SHARDED:
            per = [gl.reshape((N_CHIPS, gl.shape[0] // N_CHIPS, gl.shape[1])) for gl in per]
        grads[name] = jnp.stack(per, axis=0)
    return loss, dx, grads


def _reduce_big(grads):
    gs = [grads[n] for n in BIG]
    from_sibling = _other_halves_to_sibling("reduce_to_sibling", gs)
    parts = [_add_own_half(f"reduce_add_cores_{n}", g, r) for n, g, r in zip(BIG, gs, from_sibling)]
    slabs = _scatter_chips("reduce_scatter_chips", parts)
    halves = []
    for n, sl in zip(BIG, slabs):
        s, hf, r, c = sl.shape
        halves.append(_sum_leading(f"reduce_add_chips_{n}", sl.reshape(s, hf * r, c)).reshape(hf, r, c))
    return dict(zip(BIG, _join_halves("reduce_join_halves", halves)))


def _reduce_small(grads, shapes):
    flat = jnp.concatenate([grads[n].reshape(-1) for n in SMALL])
    total = flat.shape[0]
    rows = -(-total // (512 * LANES)) * 512
    block = jnp.pad(flat, (0, rows * LANES - total)).reshape(rows, LANES)
    summed = _sum_leading("reduce_small_sum", _gather_all("reduce_small_gather", block)).reshape(-1)
    out, off = {}, 0
    for n in SMALL:
        size = math.prod(shapes[n])
        out[n] = summed[off:off + size].reshape(shapes[n])
        off += size
    return out


def kernel(x, mem, ffn1_norm, ffn1_w_gu, ffn1_w_down, mix_norm, mem_norm, w_mem_kv, xq_norm, xk_norm, w_out, ffn2_norm, ffn2_w_gu, ffn2_w_down, sb_w_in, s5_w_in, s5_log_dt, s5_a_re, s5_a_im, s5_b_re, s5_b_im, s5_c_re, s5_c_im, s5_d, s5_w_glu, loss_target, m_ffn1_norm, m_ffn1_w_gu, m_ffn1_w_down, m_mix_norm, m_mem_norm, m_w_mem_kv, m_xq_norm, m_xk_norm, m_w_out, m_ffn2_norm, m_ffn2_w_gu, m_ffn2_w_down, m_sb_w_in, m_s5_w_in, m_s5_log_dt, m_s5_a_re, m_s5_a_im, m_s5_b_re, m_s5_b_im, m_s5_c_re, m_s5_c_im, m_s5_d, m_s5_w_glu, v_ffn1_norm, v_ffn1_w_gu, v_ffn1_w_down, v_mix_norm, v_mem_norm, v_w_mem_kv, v_xq_norm, v_xk_norm, v_w_out, v_ffn2_norm, v_ffn2_w_gu, v_ffn2_w_down, v_sb_w_in, v_s5_w_in, v_s5_log_dt, v_s5_a_re, v_s5_a_im, v_s5_b_re, v_s5_b_im, v_s5_c_re, v_s5_c_im, v_s5_d, v_s5_w_glu):
    given = dict(locals())
    wts = {n: given[n] for n in WEIGHTS}
    chip = 2 * lax.axis_index("x") + lax.axis_index("y")

    d_sh = s5_d.shape
    d_rows = -(-math.prod(d_sh) // (8 * LANES)) * 8
    d_block = jnp.pad(s5_d.reshape(-1), (0, d_rows * LANES - math.prod(d_sh))).reshape(d_rows, LANES)
    d_all = _gather_all("gather_s5_d", d_block).reshape(8, -1)[:, :math.prod(d_sh)]
    d_full = jnp.concatenate([d_all[2 * s].reshape(d_sh) for s in range(N_CHIPS)], axis=1)

    gathered = _gather_chips("gather_weights", [wts[n].astype(BF16) for n in BIG])
    p = {n: wts[n] for n in SMALL}
    p["s5_d"] = d_full
    loss, grad_x, grads = _local_step(x[0], mem[0], loss_target[0], dict(zip(BIG, gathered)), p)

    small_shapes = {n: wts[n].shape for n in SMALL}
    small_shapes["s5_d"] = d_full.shape
    gsum = _reduce_big(grads)
    gsmall = _reduce_small(grads, small_shapes)
    gsmall["s5_d"] = lax.dynamic_slice_in_dim(gsmall["s5_d"], chip * d_sh[1], d_sh[1], axis=1)
    gsum.update(gsmall)

    deltas, new_m, new_v = {}, {}, {}
    for n in WEIGHTS:
        deltas[n], new_m[n], new_v[n] = _adamw(f"adamw_{n}", wts[n], gsum[n], given["m_" + n], given["v_" + n])
    total_loss = lax.psum(loss[0, 0], ("x", "y", "c"))
    return (total_loss, grad_x[None], *[gsum[n] for n in WEIGHTS], *[deltas[n] for n in WEIGHTS],
            *[new_m[n] for n in WEIGHTS], *[new_v[n] for n in WEIGHTS])
```

```python
import math
from typing import NamedTuple

import jax
import jax.numpy as jnp
from jax import lax
from jax.experimental import pallas as pl
from jax.experimental.pallas import tpu as pltpu

F32 = jnp.float32
BF16 = jnp.bfloat16

HEAD_DIM = 128
S5_GROUP = 16
S5_STATE = 64
EPS = 1e-6
ADAM_LR = 0.001
ADAM_B1 = 0.9
ADAM_B2 = 0.999
ADAM_EPS = 1e-08
ADAM_WD = 0.01
ADAM_STEP = 10

LANES = 128
V7X_VMEM_LIMIT = 56 * 1024 * 1024
N_CHIPS = 4
MESH = pl.DeviceIdType.MESH
_NT = (((1,), (1,)), ((), ()))
_TN = (((0,), (0,)), ((), ()))


def _params(sem):
    return pltpu.CompilerParams(dimension_semantics=sem, vmem_limit_bytes=V7X_VMEM_LIMIT)


def _tile(n, want, mult=LANES):
    if n <= want:
        return n
    t = (want // mult) * mult
    while t > mult and n % t:
        t -= mult
    assert n % t == 0, (n, want, mult)
    return t


def _row_tile(rows, cols, itemsize=4, target=1 << 20):
    return _tile(rows, max(16, (target // (cols * itemsize)) // 16 * 16), 16)


def _sds(shape, dtype):
    return jax.ShapeDtypeStruct(tuple(shape), dtype)


class _W(NamedTuple):
    arr: jax.Array
    layer: int
    cols: bool


def _wspec(w, tr, tc, rb, cb):
    _, _, r, c = w.arr.shape
    layer = w.layer
    assert r % tr == 0 and c % tc == 0, (r, c, tr, tc)
    if w.cols:
        nb = c // tc
        return pl.BlockSpec((None, None, tr, tc), lambda *g: (cb(*g) // nb, layer, rb(*g), cb(*g) % nb))
    nb = r // tr
    return pl.BlockSpec((None, None, tr, tc), lambda *g: (rb(*g) // nb, layer, rb(*g) % nb, cb(*g)))


_DIMS = {"nn": (((1,), (0,)), ((), ())), "nt": _NT, "tn": _TN}


def _matmul(name, mode, a, b, *, grid, a_spec, b_spec, out_shape, out_spec, acc_shape,
            extras=(), extra_specs=(), epilogue=None):
    nk = grid[2]
    ne = len(extras)
    multi = isinstance(out_shape, (tuple, list))
    outs = tuple(out_shape) if multi else (out_shape,)
    ospecs = tuple(out_spec) if multi else (out_spec,)
    no = len(outs)

    def body(a_ref, b_ref, *rest):
        ex = rest[:ne]
        out_refs = rest[ne:ne + no]
        part = lax.dot_general(a_ref[...].astype(BF16), b_ref[...].astype(BF16), _DIMS[mode],
                               preferred_element_type=F32)

        def finish(acc):
            if epilogue is None:
                out_refs[0][...] = acc.astype(out_refs[0].dtype)
            else:
                epilogue(acc, ex, out_refs)

        if nk == 1:
            finish(part)
        else:
            acc_ref = rest[-1]
            k = pl.program_id(2)

            @pl.when(k == 0)
            def _():
                acc_ref[...] = part

            @pl.when(k > 0)
            def _():
                acc_ref[...] += part

            @pl.when(k == nk - 1)
            def _():
                finish(acc_ref[...])

    return pl.pallas_call(
        body, name=name, grid=grid,
        in_specs=[a_spec, b_spec, *extra_specs],
        out_specs=ospecs if multi else ospecs[0],
        out_shape=outs if multi else outs[0],
        scratch_shapes=[pltpu.VMEM(acc_shape, F32)] if nk > 1 else [],
        compiler_params=_params(("parallel", "parallel", "arbitrary")),
    )(a, b, *extras)


def _mm_nn(name, a, w, out_dtype=F32, extras=(), epilogue=None, n_out=1):
    m, kk = a.shape
    _, _, r, c = w.arr.shape
    tm = _tile(m, 1024, 8)
    tk = _tile(r, 2048)
    if w.cols:
        n, tn = N_CHIPS * c, _tile(c, 256)
    else:
        n, tn = c, _tile(c, 512)
    ospec = pl.BlockSpec((tm, tn), lambda i, j, k: (i, j))
    out_shape = _sds((m, n), out_dtype)
    return _matmul(name, "nn", a, w.arr, grid=(m // tm, n // tn, kk // tk),
                   a_spec=pl.BlockSpec((tm, tk), lambda i, j, k: (i, k)),
                   b_spec=_wspec(w, tk, tn, lambda i, j, k: k, lambda i, j, k: j),
                   out_shape=out_shape if n_out == 1 else (out_shape,) * n_out,
                   out_spec=ospec if n_out == 1 else (ospec,) * n_out, acc_shape=(tm, tn),
                   extras=extras, extra_specs=(ospec,) * len(extras), epilogue=epilogue)


def _add_residual(acc, ex, outs):
    outs[0][...] = ex[0][...] + acc


def _mm_nt(name, a, w, extras=(), epilogue=None):
    m, n = a.shape
    _, _, r, c = w.arr.shape
    tm = _tile(m, 512, 8)
    if w.cols:
        kk, tr = r, r
        tc = c if r * c * 2 <= (12 << 20) else _tile(c, 1408)
    else:
        kk, tr, tc = N_CHIPS * r, _tile(r, 512), _tile(c, 2048)
    ospec = pl.BlockSpec((tm, tr), lambda i, j, k: (i, j))
    return _matmul(name, "nt", a, w.arr, grid=(m // tm, kk // tr, n // tc),
                   a_spec=pl.BlockSpec((tm, tc), lambda i, j, k: (i, k)),
                   b_spec=_wspec(w, tr, tc, lambda i, j, k: j, lambda i, j, k: k),
                   out_shape=_sds((m, kk), F32), out_spec=ospec, acc_shape=(tm, tr),
                   extras=extras, extra_specs=(ospec,) * len(extras), epilogue=epilogue)


def _mm_tn(name, a, b, shards=None):
    m, kk = a.shape
    n = b.shape[1]
    tkm = _tile(m, 1024, 8)
    tk = _tile(kk, 512)
    if shards:
        ns = n // shards
        return _matmul(name, "tn", a, b, grid=(kk // tk, shards, m // tkm),
                       a_spec=pl.BlockSpec((tkm, tk), lambda i, j, k: (k, i)),
                       b_spec=pl.BlockSpec((tkm, ns), lambda i, j, k: (k, j)),
                       out_shape=_sds((shards, kk, ns), F32),
                       out_spec=pl.BlockSpec((None, tk, ns), lambda i, j, k: (j, i, 0)), acc_shape=(tk, ns))
    tn = _tile(n, 2048)
    return _matmul(name, "tn", a, b, grid=(kk // tk, n // tn, m // tkm),
                   a_spec=pl.BlockSpec((tkm, tk), lambda i, j, k: (k, i)),
                   b_spec=pl.BlockSpec((tkm, tn), lambda i, j, k: (k, j)),
                   out_shape=_sds((kk, n), F32), out_spec=pl.BlockSpec((tk, tn), lambda i, j, k: (i, j)),
                   acc_shape=(tk, tn))


def _rmsnorm_fwd(name, x, gain):
    m, d = x.shape
    tm = _tile(m, 512, 8)

    def body(x_ref, g_ref, o_ref):
        xv = x_ref[...]
        r = lax.rsqrt(jnp.mean(xv * xv, axis=-1, keepdims=True) + EPS)
        o_ref[...] = (xv * r * g_ref[...]).astype(o_ref.dtype)

    return pl.pallas_call(
        body, name=name, grid=(m // tm,),
        in_specs=[pl.BlockSpec((tm, d), lambda i: (i, 0)), pl.BlockSpec((1, d), lambda i: (0, 0))],
        out_specs=pl.BlockSpec((tm, d), lambda i: (i, 0)),
        out_shape=_sds((m, d), BF16),
        compiler_params=_params(("parallel",)),
    )(x, gain)


def _rmsnorm_bwd(name, x, dh, gain, dres=None):
    m, d = x.shape
    tm = _tile(m, 256, 8)
    nsteps = m // tm
    with_dx = dres is not None

    def body(*refs):
        if with_dx:
            x_ref, dh_ref, g_ref, dres_ref, dx_ref, dxb_ref, dg_ref, acc_ref = refs
        else:
            x_ref, dh_ref, g_ref, dg_ref, acc_ref = refs
        i = pl.program_id(0)
        xv = x_ref[...]
        r = lax.rsqrt(jnp.mean(xv * xv, axis=-1, keepdims=True) + EPS)
        xh = xv * r
        dhv = dh_ref[...].astype(F32)
        contrib = (dhv * xh).reshape(tm // 8, 8, d).sum(axis=0)

        @pl.when(i == 0)
        def _():
            acc_ref[...] = contrib

        @pl.when(i > 0)
        def _():
            acc_ref[...] += contrib

        @pl.when(i == nsteps - 1)
        def _():
            dg_ref[...] = jnp.sum(acc_ref[...], axis=0, keepdims=True)

        if with_dx:
            dxh = dhv * g_ref[...]
            dx = r * (dxh - xh * jnp.mean(dxh * xh, axis=-1, keepdims=True)) + dres_ref[...]
            dx_ref[...] = dx
            dxb_ref[...] = dx.astype(BF16)

    row = pl.BlockSpec((tm, d), lambda i: (i, 0))
    vec = pl.BlockSpec((1, d), lambda i: (0, 0))
    if with_dx:
        return pl.pallas_call(
            body, name=name, grid=(nsteps,),
            in_specs=[row, row, vec, row], out_specs=(row, row, vec),
            out_shape=(_sds((m, d), F32), _sds((m, d), BF16), _sds((1, d), F32)),
            scratch_shapes=[pltpu.VMEM((8, d), F32)],
            compiler_params=_params(("arbitrary",)),
        )(x, dh, gain, dres)
    return pl.pallas_call(
        body, name=name, grid=(nsteps,),
        in_specs=[row, row, vec], out_specs=vec,
        out_shape=_sds((1, d), F32),
        scratch_shapes=[pltpu.VMEM((8, d), F32)],
        compiler_params=_params(("arbitrary",)),
    )(x, dh, gain)


def _ffn_fwd(tag, x, gain, wgu, wdown):
    m, d = x.shape
    ns = wgu.arr.shape[3]
    f = 2 * ns
    h = _rmsnorm_fwd(f"{tag}_norm", x, gain)
    tm = _tile(m, 1024, 8)
    tn = _tile(ns, 256)
    nb = ns // tn

    def gu_body(h_ref, wg_ref, wu_ref, act_ref, gu_ref):
        hv = h_ref[...]
        g = jnp.dot(hv, wg_ref[...], preferred_element_type=F32)
        u = jnp.dot(hv, wu_ref[...], preferred_element_type=F32)
        act_ref[...] = (g * jax.nn.sigmoid(g) * u).astype(BF16)
        gu_ref[0] = g.astype(BF16)
        gu_ref[1] = u.astype(BF16)

    act, gu = pl.pallas_call(
        gu_body, name=f"{tag}_gu", grid=(m // tm, 2 * nb),
        in_specs=[pl.BlockSpec((tm, d), lambda i, j: (i, 0)),
                  _wspec(wgu, d, tn, lambda i, j: 0, lambda i, j: j),
                  _wspec(wgu, d, tn, lambda i, j: 0, lambda i, j: 2 * nb + j)],
        out_specs=(pl.BlockSpec((tm, tn), lambda i, j: (i, j)),
                   pl.BlockSpec((2, tm, tn), lambda i, j: (0, i, j))),
        out_shape=(_sds((m, f), BF16), _sds((2, m, f), BF16)),
        compiler_params=_params(("parallel", "parallel")),
    )(h, wgu.arr, wgu.arr)

    tk = wdown.arr.shape[2]
    tnd = _tile(d, 1024)

    def down_epilogue(acc, ex, outs):
        outs[0][...] = ex[0][...] + 0.5 * acc

    ospec = pl.BlockSpec((tm, tnd), lambda i, j, k: (i, j))
    y = _matmul(
        f"{tag}_down", "nn", act, wdown.arr, grid=(m // tm, d // tnd, f // tk),
        a_spec=pl.BlockSpec((tm, tk), lambda i, j, k: (i, k)),
        b_spec=_wspec(wdown, tk, tnd, lambda i, j, k: k, lambda i, j, k: j),
        out_shape=_sds((m, d), F32), out_spec=ospec, acc_shape=(tm, tnd),
        extras=(x,), extra_specs=(ospec,), epilogue=down_epilogue)
    return y, (x, h, act, gu)


def _ffn_bwd(tag, saved, gain, wgu, wdown, dy, dyb):
    x, h, act, gu = saved
    m, d = x.shape
    ns = wgu.arr.shape[3]
    f = 2 * ns

    def dact_epilogue(acc, ex, outs):
        g = ex[0][0].astype(F32)
        u = ex[0][1].astype(F32)
        da = 0.5 * acc
        s = jax.nn.sigmoid(g)
        outs[0][0] = (da * u * s * (1.0 + g * (1.0 - s))).astype(BF16)
        outs[0][1] = (da * g * s).astype(BF16)

    tma = _tile(m, 512, 8)
    tna = wdown.arr.shape[2]
    gspec = pl.BlockSpec((2, tma, tna), lambda i, j, k: (0, i, j))
    dgu = _matmul(
        f"{tag}_dact", "nt", dyb, wdown.arr, grid=(m // tma, f // tna, 1),
        a_spec=pl.BlockSpec((tma, d), lambda i, j, k: (i, 0)),
        b_spec=_wspec(wdown, tna, d, lambda i, j, k: j, lambda i, j, k: 0),
        out_shape=_sds((2, m, f), BF16), out_spec=gspec, acc_shape=(tma, tna),
        extras=(gu,), extra_specs=(gspec,), epilogue=dact_epilogue)

    tkm = _tile(m, 2048, 8)
    tf = _tile(f, 1408)
    tnd = _tile(d, 1024)

    def half_epilogue(acc, ex, outs):
        outs[0][...] = 0.5 * acc

    dwdown = _matmul(
        f"{tag}_dwdown", "tn", act, dyb, grid=(f // tf, d // tnd, m // tkm),
        a_spec=pl.BlockSpec((tkm, tf), lambda i, j, k: (k, i)),
        b_spec=pl.BlockSpec((tkm, tnd), lambda i, j, k: (k, j)),
        out_shape=_sds((f, d), F32), out_spec=pl.BlockSpec((tf, tnd), lambda i, j, k: (i, j)),
        acc_shape=(tf, tnd), epilogue=half_epilogue)

    td = _tile(d, 512)
    dwgu = _matmul(
        f"{tag}_dwgu", "tn", h, dgu, grid=(d // td, 4, m // tkm),
        a_spec=pl.BlockSpec((tkm, td), lambda i, j, k: (k, i)),
        b_spec=pl.BlockSpec((None, tkm, ns), lambda i, j, k: (j // 2, k, j % 2)),
        out_shape=_sds((4, d, ns), F32), out_spec=pl.BlockSpec((None, td, ns), lambda i, j, k: (j, i, 0)),
        acc_shape=(td, ns))

    tmh = _tile(m, 512, 8)
    dh = _matmul(
        f"{tag}_dh", "nt", dgu, wgu.arr, grid=(m // tmh, 1, 4),
        a_spec=pl.BlockSpec((None, tmh, ns), lambda i, j, k: (k // 2, i, k % 2)),
        b_spec=_wspec(wgu, d, ns, lambda i, j, k: 0, lambda i, j, k: k),
        out_shape=_sds((m, d), F32), out_spec=pl.BlockSpec((tmh, d), lambda i, j, k: (i, 0)),
        acc_shape=(tmh, d))

    dx, dxb, dgain = _rmsnorm_bwd(f"{tag}_dnorm", x, dh, gain, dy)
    return dx, dxb, dgain, dwgu, dwdown


SB_KEY_BLOCK = 256
SB_QUERY_ROWS = 512


def _split2(x):
    hi = x.astype(BF16)
    return hi, (x - hi.astype(F32)).astype(BF16)


def _dot2(parts, mat):
    return jnp.dot(parts[0], mat, preferred_element_type=F32) + jnp.dot(parts[1], mat, preferred_element_type=F32)


def _sb_logits(q, k, scale):
    z = lax.dot_general(q, k, _NT, preferred_element_type=F32) * scale
    lp = jnp.minimum(z, 0.0) - jnp.log(1.0 + jnp.exp(-jnp.abs(z)))
    return lp, lp - z


def _sb_fwd(name, qkv, n_heads):
    length = qkv.shape[0]
    kb = _tile(length, SB_KEY_BLOCK, LANES)
    tq = _tile(length, SB_QUERY_ROWS, kb)
    r = tq // kb
    scale = 1.0 / math.sqrt(HEAD_DIM)

    def body(q_ref, k_ref, v_ref, o_ref, tot_ref):
        qi = pl.program_id(1)
        q = q_ref[...].astype(BF16)
        row = qi * tq + lax.broadcasted_iota(jnp.int32, (tq, kb), 0)
        col = lax.broadcasted_iota(jnp.int32, (tq, kb), 1)
        later = (lax.broadcasted_iota(jnp.int32, (kb, kb), 0) > lax.broadcasted_iota(jnp.int32, (kb, kb), 1)).astype(BF16)

        def block(kbi, carry, masked):
            c, acc = carry
            ks = pl.multiple_of(kbi * kb, kb)
            k = k_ref[pl.ds(ks, kb), :].astype(BF16)
            v = v_ref[pl.ds(ks, kb), :].astype(BF16)
            lp, ln = _sb_logits(q, k, scale)
            if masked:
                valid = (ks + col) < row
                ln = jnp.where(valid, ln, 0.0)
            w = jnp.exp(lp + (c + _dot2(_split2(ln), later)))
            if masked:
                w = jnp.where(valid, w, 0.0)
            acc = acc + jnp.dot(w.astype(BF16), v, preferred_element_type=F32)
            return c + jnp.sum(ln, axis=1, keepdims=True), acc

        carry = (jnp.zeros((tq, 1), F32), jnp.zeros((tq, HEAD_DIM), F32))
        for dgl in range(r - 1, -1, -1):
            carry = block(qi * r + dgl, carry, True)
        carry = lax.fori_loop(0, qi * r, lambda i, cr: block(qi * r - 1 - i, cr, False), carry)
        o_ref[...] = carry[1]
        tot_ref[...] = jnp.broadcast_to(carry[0], (tq, HEAD_DIM))

    h = n_heads
    qblk = pl.BlockSpec((tq, HEAD_DIM), lambda hh, i: (i, hh))
    return pl.pallas_call(
        body, name=name, grid=(h, length // tq),
        in_specs=[qblk,
                  pl.BlockSpec((length, HEAD_DIM), lambda hh, i: (0, h + hh)),
                  pl.BlockSpec((length, HEAD_DIM), lambda hh, i: (0, 2 * h + hh))],
        out_specs=(qblk, qblk),
        out_shape=(_sds((length, h * HEAD_DIM), F32), _sds((length, h * HEAD_DIM), F32)),
        compiler_params=_params(("parallel", "arbitrary")),
    )(qkv, qkv, qkv)


def _sb_bwd(name, qkv, tot, do, n_heads):
    length = qkv.shape[0]
    kb = _tile(length, SB_KEY_BLOCK, LANES)
    tq = _tile(length, SB_QUERY_ROWS, kb)
    r = tq // kb
    scale = 1.0 / math.sqrt(HEAD_DIM)

    def body(q_ref, k_ref, v_ref, tot_ref, do_ref, dq_ref, dk_ref, dv_ref):
        qi = pl.program_id(1)

        @pl.when(qi == 0)
        def _():
            dk_ref[...] = jnp.zeros_like(dk_ref)
            dv_ref[...] = jnp.zeros_like(dv_ref)

        q = q_ref[...].astype(BF16)
        dob = do_ref[...].astype(BF16)
        tot = tot_ref[:, 0:1]
        row = qi * tq + lax.broadcasted_iota(jnp.int32, (tq, kb), 0)
        col = lax.broadcasted_iota(jnp.int32, (tq, kb), 1)
        jj = lax.broadcasted_iota(jnp.int32, (kb, kb), 0)
        ss = lax.broadcasted_iota(jnp.int32, (kb, kb), 1)
        later = (jj > ss).astype(BF16)
        before = (jj < ss).astype(BF16)

        def block(kbi, carry, masked):
            pl_, pe, dq = carry
            ks = pl.multiple_of(kbi * kb, kb)
            k = k_ref[pl.ds(ks, kb), :].astype(BF16)
            v = v_ref[pl.ds(ks, kb), :].astype(BF16)
            lp, ln_raw = _sb_logits(q, k, scale)
            ln = ln_raw
            if masked:
                valid = (ks + col) < row
                ln = jnp.where(valid, ln_raw, 0.0)
            tb = jnp.sum(ln, axis=1, keepdims=True)
            w = jnp.exp(lp + ((tot - pl_ - tb) + _dot2(_split2(ln), later)))
            if masked:
                w = jnp.where(valid, w, 0.0)
            e = w * lax.dot_general(dob, v, _NT, preferred_element_type=F32)
            dv_ref[pl.ds(ks, kb), :] += lax.dot_general(w.astype(BF16), dob, _TN, preferred_element_type=F32)
            dz = e * jnp.exp(ln_raw) - jnp.exp(lp) * (pe + _dot2(_split2(e), before))
            if masked:
                dz = jnp.where(valid, dz, 0.0)
            dzb = (dz * scale).astype(BF16)
            dq = dq + jnp.dot(dzb, k, preferred_element_type=F32)
            dk_ref[pl.ds(ks, kb), :] += lax.dot_general(dzb, q, _TN, preferred_element_type=F32)
            return pl_ + tb, pe + jnp.sum(e, axis=1, keepdims=True), dq

        carry = (jnp.zeros((tq, 1), F32), jnp.zeros((tq, 1), F32), jnp.zeros((tq, HEAD_DIM), F32))
        carry = lax.fori_loop(0, qi * r, lambda i, cr: block(i, cr, False), carry)
        for dgl in range(r):
            carry = block(qi * r + dgl, carry, True)
        dq_ref[...] = carry[2]

    h = n_heads
    qblk = pl.BlockSpec((tq, HEAD_DIM), lambda hh, i: (i, hh))
    full = pl.BlockSpec((length, HEAD_DIM), lambda hh, i: (0, hh))
    out = _sds((length, h * HEAD_DIM), F32)
    return pl.pallas_call(
        body, name=name, grid=(h, length // tq),
        in_specs=[qblk,
                  pl.BlockSpec((length, HEAD_DIM), lambda hh, i: (0, h + hh)),
                  pl.BlockSpec((length, HEAD_DIM), lambda hh, i: (0, 2 * h + hh)),
                  qblk, qblk],
        out_specs=(qblk, full, full), out_shape=(out, out, out),
        compiler_params=_params(("parallel", "arbitrary")),
    )(qkv, qkv, qkv, tot, do)


def _head_rms(xh):
    r = lax.rsqrt(jnp.mean(xh * xh, axis=-1, keepdims=True) + EPS)
    return xh * r, r


def _mem_fwd(name, qsrc, qcol0, kv, gq, gk):
    length = qsrc.shape[0]
    mm, mw2 = kv.shape
    mw = mw2 // 2
    nh = mw // HEAD_DIM
    tq = _tile(length, 512, 8)
    inv = 1.0 / math.sqrt(HEAD_DIM)
    assert qcol0 % mw == 0

    def body(q_ref, kv_ref, gq_ref, gk_ref, o_ref):
        for hh in range(nh):
            sl = slice(hh * HEAD_DIM, (hh + 1) * HEAD_DIM)
            qn = _head_rms(q_ref[:, sl])[0] * gq_ref[...]
            kn = _head_rms(kv_ref[:, sl])[0] * gk_ref[...]
            vh = kv_ref[:, mw + hh * HEAD_DIM:mw + (hh + 1) * HEAD_DIM].astype(BF16)
            s = lax.dot_general(qn.astype(BF16), kn.astype(BF16), _NT, preferred_element_type=F32) * inv
            p = jnp.exp(s - jnp.max(s, axis=-1, keepdims=True))
            p = p / jnp.sum(p, axis=-1, keepdims=True)
            o_ref[:, sl] = jnp.dot(p.astype(BF16), vh, preferred_element_type=F32)

    vec = pl.BlockSpec((1, HEAD_DIM), lambda i: (0, 0))
    return pl.pallas_call(
        body, name=name, grid=(length // tq,),
        in_specs=[pl.BlockSpec((tq, mw), lambda i: (i, qcol0 // mw)),
                  pl.BlockSpec((mm, mw2), lambda i: (0, 0)), vec, vec],
        out_specs=pl.BlockSpec((tq, mw), lambda i: (i, 0)),
        out_shape=_sds((length, mw), F32),
        compiler_params=_params(("parallel",)),
    )(qsrc, kv, gq, gk)


def _mem_bwd(name, qsrc, qcol0, kv, gq, gk, dsrc, docol0):
    length = qsrc.shape[0]
    mm, mw2 = kv.shape
    mw = mw2 // 2
    nh = mw // HEAD_DIM
    tq = _tile(length, 512, 8)
    nsteps = length // tq
    inv = 1.0 / math.sqrt(HEAD_DIM)

    def body(q_ref, kv_ref, gq_ref, gk_ref, do_ref, dq_ref, dkv_ref, dgq_ref, dgk_ref):
        i = pl.program_id(0)

        @pl.when(i == 0)
        def _():
            dkv_ref[...] = jnp.zeros_like(dkv_ref)
            dgq_ref[...] = jnp.zeros_like(dgq_ref)

        gqv = gq_ref[...]
        gkv = gk_ref[...]
        for hh in range(nh):
            sl = slice(hh * HEAD_DIM, (hh + 1) * HEAD_DIM)
            slv = slice(mw + hh * HEAD_DIM, mw + (hh + 1) * HEAD_DIM)
            qhat, rq = _head_rms(q_ref[:, sl])
            qn = (qhat * gqv).astype(BF16)
            kn = (_head_rms(kv_ref[:, sl])[0] * gkv).astype(BF16)
            vh = kv_ref[:, slv].astype(BF16)
            dob = do_ref[:, sl].astype(BF16)
            s = lax.dot_general(qn, kn, _NT, preferred_element_type=F32) * inv
            p = jnp.exp(s - jnp.max(s, axis=-1, keepdims=True))
            p = p / jnp.sum(p, axis=-1, keepdims=True)
            dp = lax.dot_general(dob, vh, _NT, preferred_element_type=F32)
            ds = (p * (dp - jnp.sum(dp * p, axis=-1, keepdims=True)) * inv).astype(BF16)
            dqn = jnp.dot(ds, kn, preferred_element_type=F32)
            dkv_ref[:, sl] += lax.dot_general(ds, qn, _TN, preferred_element_type=F32)
            dkv_ref[:, slv] += lax.dot_general(p.astype(BF16), dob, _TN, preferred_element_type=F32)
            dgq_ref[...] += jnp.sum(dqn * qhat, axis=0, keepdims=True)
            dqh = dqn * gqv
            dq_ref[:, sl] = rq * (dqh - qhat * jnp.mean(dqh * qhat, axis=-1, keepdims=True))

        @pl.when(i == nsteps - 1)
        def _():
            dgk = jnp.zeros((1, HEAD_DIM), F32)
            for hh in range(nh):
                sl = slice(hh * HEAD_DIM, (hh + 1) * HEAD_DIM)
                khat, rk = _head_rms(kv_ref[:, sl])
                dkn = dkv_ref[:, sl]
                dgk = dgk + jnp.sum(dkn * khat, axis=0, keepdims=True)
                dkh = dkn * gkv
                dkv_ref[:, sl] = rk * (dkh - khat * jnp.mean(dkh * khat, axis=-1, keepdims=True))
            dgk_ref[...] = dgk

    vec = pl.BlockSpec((1, HEAD_DIM), lambda i: (0, 0))
    kvs = pl.BlockSpec((mm, mw2), lambda i: (0, 0))
    blk = pl.BlockSpec((tq, mw), lambda i: (i, 0))
    return pl.pallas_call(
        body, name=name, grid=(nsteps,),
        in_specs=[pl.BlockSpec((tq, mw), lambda i: (i, qcol0 // mw)), kvs, vec, vec,
                  pl.BlockSpec((tq, mw), lambda i: (i, docol0 // mw))],
        out_specs=(blk, kvs, vec, vec),
        out_shape=(_sds((length, mw), F32), _sds((mm, mw2), F32), _sds((1, HEAD_DIM), F32), _sds((1, HEAD_DIM), F32)),
        compiler_params=_params(("arbitrary",)),
    )(qsrc, kv, gq, gk, dsrc)


S5_TILE_GROUPS = LANES // S5_GROUP
S5_TILE_STATES = S5_TILE_GROUPS * S5_STATE
S5_SCAN_ROWS = 64
_HI = lax.Precision.HIGHEST


def _dotf(a, b, dims=None):
    if dims is None:
        return jnp.dot(a, b, precision=_HI, preferred_element_type=F32)
    return lax.dot_general(a, b, dims, precision=_HI, preferred_element_type=F32)


def _s5_prep(log_dt, a_re, a_im, b_re, b_im, c_re, c_im):
    g, n = a_re.shape
    nt = g // S5_TILE_GROUPS
    dt = jnp.exp(log_dt)[:, None]
    mag = jnp.exp(a_re * dt)
    ab_re = mag * jnp.cos(a_im * dt)
    ab_im = mag * jnp.sin(a_im * dt)
    den = a_re * a_re + a_im * a_im
    num_re = ab_re - 1.0
    co_re = (num_re * a_re + ab_im * a_im) / den
    co_im = (ab_im * a_re - num_re * a_im) / den
    bb_re = co_re[..., None] * b_re - co_im[..., None] * b_im
    bb_im = co_re[..., None] * b_im + co_im[..., None] * b_re
    eye = jnp.eye(S5_TILE_GROUPS, dtype=F32)

    def blk_b(bb):
        t = bb.reshape(nt, S5_TILE_GROUPS, n, S5_GROUP).transpose(0, 1, 3, 2)
        return jnp.einsum("jgcn,gh->jgchn", t, eye).reshape(nt, LANES, S5_TILE_STATES)

    def blk_c(cc):
        t = cc.reshape(nt, S5_TILE_GROUPS, S5_GROUP, n).transpose(0, 1, 3, 2)
        return jnp.einsum("jgnc,gh->jgnhc", t, eye).reshape(nt, S5_TILE_STATES, LANES)

    return (ab_re.reshape(1, g * n), ab_im.reshape(1, g * n), blk_b(bb_re), blk_b(bb_im), blk_c(c_re), blk_c(c_im))


def _s5_bu(name, usrc, bblk_re, bblk_im):
    length = usrc.shape[0]
    nt = bblk_re.shape[0]
    tm = _tile(length, 512, 8)

    def body(u_ref, br_ref, bi_ref, or_ref, oi_ref):
        u = u_ref[...]
        or_ref[...] = _dotf(u, br_ref[...])
        oi_ref[...] = _dotf(u, bi_ref[...])

    bspec = pl.BlockSpec((None, LANES, S5_TILE_STATES), lambda i, j: (j, 0, 0))
    ospec = pl.BlockSpec((tm, S5_TILE_STATES), lambda i, j: (i, j))
    out = _sds((length, nt * S5_TILE_STATES), F32)
    return pl.pallas_call(
        body, name=name, grid=(length // tm, nt),
        in_specs=[pl.BlockSpec((tm, LANES), lambda i, j: (i, j)), bspec, bspec],
        out_specs=(ospec, ospec), out_shape=(out, out),
        compiler_params=_params(("parallel", "parallel")),
    )(usrc, bblk_re, bblk_im)


def _scan_rows(hr, hi, ar, ai, reverse):
    t = hr.shape[0]
    rows = lax.broadcasted_iota(jnp.int32, hr.shape, 0)
    d = 1
    while d < t:
        if reverse:
            sr = jnp.where(rows < t - d, pltpu.roll(hr, t - d, 0), 0.0)
            si = jnp.where(rows < t - d, pltpu.roll(hi, t - d, 0), 0.0)
        else:
            sr = jnp.where(rows >= d, pltpu.roll(hr, d, 0), 0.0)
            si = jnp.where(rows >= d, pltpu.roll(hi, d, 0), 0.0)
        hr, hi = hr + ar * sr - ai * si, hi + ar * si + ai * sr
        ar, ai = ar * ar - ai * ai, 2.0 * ar * ai
        d *= 2
    return hr, hi


def _s5_scan(name, x_re, x_im, a_re, a_im, reverse=False, h=None, bu=None):
    length, width = x_re.shape
    t = _tile(length, S5_SCAN_ROWS, 8)
    nsteps = length // t
    with_sum = h is not None

    def body(*refs):
        if with_sum:
            (xr_ref, xi_ref, ar_ref, ai_ref, hr_ref, hi_ref, br_ref, bi_ref,
             or_ref, oi_ref, sr_ref, si_ref, cr, ci, pr, pi, accr, acci) = refs
        else:
            xr_ref, xi_ref, ar_ref, ai_ref, or_ref, oi_ref, cr, ci, pr, pi = refs
        step = pl.program_id(0)
        ar = ar_ref[...]
        ai = ai_ref[...]
        edge = 0 if reverse else t - 1
        last = t - 1 if reverse else 0

        @pl.when(step == 0)
        def _():
            cr[...] = jnp.zeros_like(cr)
            ci[...] = jnp.zeros_like(ci)
            rows = lax.broadcasted_iota(jnp.int32, (t, width), 0)
            seed_r = jnp.where(rows == last, ar, 0.0)
            seed_i = jnp.where(rows == last, ai, 0.0)
            p_r, p_i = _scan_rows(seed_r, seed_i, ar, ai, reverse)
            pr[...] = p_r
            pi[...] = p_i
            if with_sum:
                accr[...] = jnp.zeros_like(accr)
                acci[...] = jnp.zeros_like(acci)

        hr, hi = _scan_rows(xr_ref[...], xi_ref[...], ar, ai, reverse)
        c_r = cr[...]
        c_i = ci[...]
        p_r = pr[...]
        p_i = pi[...]
        hr = hr + p_r * c_r - p_i * c_i
        hi = hi + p_r * c_i + p_i * c_r
        or_ref[...] = hr
        oi_ref[...] = hi
        cr[...] = hr[edge:edge + 1, :]
        ci[...] = hi[edge:edge + 1, :]
        if with_sum:
            wr = hr_ref[...] - br_ref[...]
            wi = hi_ref[...] - bi_ref[...]
            accr[...] += (wr * hr + wi * hi).reshape(t // 8, 8, width).sum(axis=0)
            acci[...] += (wr * hi - wi * hr).reshape(t // 8, 8, width).sum(axis=0)

            @pl.when(step == nsteps - 1)
            def _():
                sr_ref[...] = jnp.sum(accr[...], axis=0, keepdims=True)
                si_ref[...] = jnp.sum(acci[...], axis=0, keepdims=True)

    if reverse:
        blk = pl.BlockSpec((t, width), lambda s: (nsteps - 1 - s, 0))
    else:
        blk = pl.BlockSpec((t, width), lambda s: (s, 0))
    vec = pl.BlockSpec((1, width), lambda s: (0, 0))
    full = _sds((length, width), F32)
    row = _sds((1, width), F32)
    scratch = [pltpu.VMEM((1, width), F32), pltpu.VMEM((1, width), F32),
               pltpu.VMEM((t, width), F32), pltpu.VMEM((t, width), F32)]
    if with_sum:
        return pl.pallas_call(
            body, name=name, grid=(nsteps,),
            in_specs=[blk, blk, vec, vec, blk, blk, blk, blk],
            out_specs=(blk, blk, vec, vec), out_shape=(full, full, row, row),
            scratch_shapes=scratch + [pltpu.VMEM((8, width), F32), pltpu.VMEM((8, width), F32)],
            compiler_params=_params(("arbitrary",)),
        )(x_re, x_im, a_re, a_im, h[0], h[1], bu[0], bu[1])
    return pl.pallas_call(
        body, name=name, grid=(nsteps,),
        in_specs=[blk, blk, vec, vec], out_specs=(blk, blk), out_shape=(full, full),
        scratch_shapes=scratch,
        compiler_params=_params(("arbitrary",)),
    )(x_re, x_im, a_re, a_im)


_GELU_C = math.sqrt(2.0 / math.pi)


def _gelu(y):
    return 0.5 * y * (1.0 + jnp.tanh(_GELU_C * (y + 0.044715 * y * y * y)))


def _gelu_grad(y):
    th = jnp.tanh(_GELU_C * (y + 0.044715 * y * y * y))
    return 0.5 * (1.0 + th) + 0.5 * y * (1.0 - th * th) * _GELU_C * (1.0 + 3 * 0.044715 * y * y)


def _s5_out(name, h_re, h_im, cblk_re, cblk_im, usrc, dskip):
    length = h_re.shape[0]
    nt = cblk_re.shape[0]
    tm = _tile(length, 512, 8)

    def body(hr_ref, hi_ref, cr_ref, ci_ref, u_ref, d_ref, y_ref, y2_ref):
        y = _dotf(hr_ref[...], cr_ref[...]) - _dotf(hi_ref[...], ci_ref[...]) + d_ref[...] * u_ref[...]
        y_ref[...] = y
        y2_ref[...] = _gelu(y)

    hspec = pl.BlockSpec((tm, S5_TILE_STATES), lambda i, j: (i, j))
    cspec = pl.BlockSpec((None, S5_TILE_STATES, LANES), lambda i, j: (j, 0, 0))
    uspec = pl.BlockSpec((tm, LANES), lambda i, j: (i, j))
    out = _sds((length, nt * LANES), F32)
    return pl.pallas_call(
        body, name=name, grid=(length // tm, nt),
        in_specs=[hspec, hspec, cspec, cspec, uspec, pl.BlockSpec((1, LANES), lambda i, j: (0, j))],
        out_specs=(uspec, uspec), out_shape=(out, out),
        compiler_params=_params(("parallel", "parallel")),
    )(h_re, h_im, cblk_re, cblk_im, usrc, dskip)


def _s5_fwd(tag, usrc, prep, dskip, wglu):
    ab_re, ab_im, bb_re, bb_im, cb_re, cb_im = prep
    bu = _s5_bu(f"{tag}_bu", usrc, bb_re, bb_im)
    hs = _s5_scan(f"{tag}_scan", bu[0], bu[1], ab_re, ab_im)
    y, y2 = _s5_out(f"{tag}_out", hs[0], hs[1], cb_re, cb_im, usrc, dskip)

    def glu_epilogue(acc, ex, outs):
        outs[0][...] = acc
        outs[1][...] = ex[0][...] * jax.nn.sigmoid(acc)

    gl, tok = _mm_nn(f"{tag}_glu", y2, wglu, extras=(y2,), epilogue=glu_epilogue, n_out=2)
    return tok, (bu, hs, y, y2, gl)


def _s5_bwd(tag, usrc, prep, dskip, wglu, saved, dsrc):
    ab_re, ab_im, bb_re, bb_im, cb_re, cb_im = prep
    bu, hs, y, y2, gl = saved
    length = usrc.shape[0]
    tw = y.shape[1]
    nt = bb_re.shape[0]

    tme = _tile(length, 512, 8)

    def gate_body(dt_ref, y2_ref, gl_ref, dgl_ref, dy2_ref):
        s = jax.nn.sigmoid(gl_ref[...])
        dt = dt_ref[...]
        dgl_ref[...] = (dt * y2_ref[...] * s * (1.0 - s)).astype(BF16)
        dy2_ref[...] = dt * s

    espec = pl.BlockSpec((tme, tw), lambda i: (i, 0))
    dgl, dy2a = pl.pallas_call(
        gate_body, name=f"{tag}_dgate", grid=(length // tme,),
        in_specs=[espec, espec, espec], out_specs=(espec, espec),
        out_shape=(_sds((length, tw), BF16), _sds((length, tw), F32)),
        compiler_params=_params(("parallel",)),
    )(dsrc, y2, gl)

    dwglu = _mm_tn(f"{tag}_dwglu", y2, dgl)

    def dy_epilogue(acc, ex, outs):
        outs[0][...] = (ex[0][...] + acc) * _gelu_grad(ex[1][...])

    dy = _mm_nt(f"{tag}_dy", dgl, wglu, extras=(dy2a, y), epilogue=dy_epilogue)

    tmh = _tile(length, 512, 8)

    def dh_body(dy_ref, cr_ref, ci_ref, gr_ref, gi_ref):
        dyv = dy_ref[...]
        gr_ref[...] = _dotf(dyv, cr_ref[...], _NT)
        gi_ref[...] = -_dotf(dyv, ci_ref[...], _NT)

    hspec = pl.BlockSpec((tmh, S5_TILE_STATES), lambda i, j: (i, j))
    cspec = pl.BlockSpec((None, S5_TILE_STATES, LANES), lambda i, j: (j, 0, 0))
    uspec = pl.BlockSpec((tmh, LANES), lambda i, j: (i, j))
    wide = _sds((length, nt * S5_TILE_STATES), F32)
    g_re, g_im = pl.pallas_call(
        dh_body, name=f"{tag}_dh", grid=(length // tmh, nt),
        in_specs=[uspec, cspec, cspec], out_specs=(hspec, hspec), out_shape=(wide, wide),
        compiler_params=_params(("parallel", "parallel")),
    )(dy, cb_re, cb_im)

    lam_re, lam_im, s_re, s_im = _s5_scan(f"{tag}_rscan", g_re, g_im, ab_re, -ab_im, reverse=True, h=hs, bu=bu)
    den = ab_re * ab_re + ab_im * ab_im
    da_re = (ab_re * s_re - ab_im * s_im) / den
    da_im = (ab_re * s_im + ab_im * s_re) / den

    def du_body(lr_ref, li_ref, br_ref, bi_ref, dy_ref, d_ref, du_ref):
        du_ref[...] = (_dotf(lr_ref[...], br_ref[...], _NT) + _dotf(li_ref[...], bi_ref[...], _NT)
                       + dy_ref[...] * d_ref[...])

    bspec = pl.BlockSpec((None, LANES, S5_TILE_STATES), lambda i, j: (j, 0, 0))
    dvec = pl.BlockSpec((1, LANES), lambda i, j: (0, j))
    du = pl.pallas_call(
        du_body, name=f"{tag}_du", grid=(length // tmh, nt),
        in_specs=[hspec, hspec, bspec, bspec, uspec, dvec], out_specs=uspec,
        out_shape=_sds((length, tw), F32),
        compiler_params=_params(("parallel", "parallel")),
    )(lam_re, lam_im, bb_re, bb_im, dy, dskip)

    tkm = _tile(length, 512, 8)
    nk = length // tkm

    def dpar_body(u_ref, dy_ref, hr_ref, hi_ref, lr_ref, li_ref, dbr_ref, dbi_ref, dcr_ref, dci_ref, dd_ref):
        k = pl.program_id(1)

        @pl.when(k == 0)
        def _():
            for ref in (dbr_ref, dbi_ref, dcr_ref, dci_ref, dd_ref):
                ref[...] = jnp.zeros_like(ref)

        u = u_ref[...]
        dyv = dy_ref[...]
        dbr_ref[...] += _dotf(u, lr_ref[...], _TN)
        dbi_ref[...] += _dotf(u, li_ref[...], _TN)
        dcr_ref[...] += _dotf(hr_ref[...], dyv, _TN)
        dci_ref[...] -= _dotf(hi_ref[...], dyv, _TN)
        dd_ref[...] += jnp.sum(dyv * u, axis=0, keepdims=True)

    kspec_u = pl.BlockSpec((tkm, LANES), lambda j, k: (k, j))
    kspec_h = pl.BlockSpec((tkm, S5_TILE_STATES), lambda j, k: (k, j))
    ob = pl.BlockSpec((None, LANES, S5_TILE_STATES), lambda j, k: (j, 0, 0))
    oc = pl.BlockSpec((None, S5_TILE_STATES, LANES), lambda j, k: (j, 0, 0))
    dbr, dbi, dcr, dci, dd = pl.pallas_call(
        dpar_body, name=f"{tag}_dpar", grid=(nt, nk),
        in_specs=[kspec_u, kspec_u, kspec_h, kspec_h, kspec_h, kspec_h],
        out_specs=(ob, ob, oc, oc, pl.BlockSpec((1, LANES), lambda j, k: (0, j))),
        out_shape=(_sds(bb_re.shape, F32), _sds(bb_re.shape, F32), _sds(cb_re.shape, F32), _sds(cb_re.shape, F32),
                   _sds((1, tw), F32)),
        compiler_params=_params(("parallel", "arbitrary")),
    )(usrc, dy, hs[0], hs[1], lam_re, lam_im)
    return du, (da_re, da_im, dbr, dbi, dcr, dci), dd, dwglu


def _loss_head(name, y, target):
    m, d = y.shape
    tm = _tile(m, 256, 8)
    nsteps = m // tm

    def body(y_ref, t_ref, loss_ref, dy_ref, dyb_ref, acc_ref):
        i = pl.program_id(0)
        diff = y_ref[...] - t_ref[...]
        dy = diff * (1.0 / d)
        dy_ref[...] = dy
        dyb_ref[...] = dy.astype(BF16)
        sq = (diff * diff).reshape(tm // 8, 8, d).sum(axis=0)

        @pl.when(i == 0)
        def _():
            acc_ref[...] = sq

        @pl.when(i > 0)
        def _():
            acc_ref[...] += sq

        @pl.when(i == nsteps - 1)
        def _():
            loss_ref[...] = jnp.full(loss_ref.shape, jnp.sum(acc_ref[...]) * (0.5 / d), F32)

    row = pl.BlockSpec((tm, d), lambda i: (i, 0))
    return pl.pallas_call(
        body, name=name, grid=(nsteps,),
        in_specs=[row, row], out_specs=(pl.BlockSpec((8, LANES), lambda i: (0, 0)), row, row),
        out_shape=(_sds((8, LANES), F32), _sds((m, d), F32), _sds((m, d), BF16)),
        scratch_shapes=[pltpu.VMEM((8, d), F32)],
        compiler_params=_params(("arbitrary",)),
    )(y, target)


def _adamw(name, w, g, m, v):
    shape = w.shape
    total = math.prod(shape)
    if shape[-1] % LANES and total % LANES == 0:
        view = (total // LANES, LANES)
    else:
        view = (total // shape[-1], shape[-1])
    rows, cols = view
    tr = rows
    if rows * cols * 4 > (1 << 20) and rows % 16 == 0:
        tr = _row_tile(rows, cols)
    c1 = 1.0 / (1.0 - ADAM_B1 ** ADAM_STEP)
    c2 = 1.0 / (1.0 - ADAM_B2 ** ADAM_STEP)

    def body(w_ref, g_ref, m_ref, v_ref, d_ref, nm_ref, nv_ref):
        gv = g_ref[...]
        nm = ADAM_B1 * m_ref[...] + (1.0 - ADAM_B1) * gv
        nv = ADAM_B2 * v_ref[...] + (1.0 - ADAM_B2) * gv * gv
        nm_ref[...] = nm
        nv_ref[...] = nv
        d_ref[...] = -ADAM_LR * ((nm * c1) / (jnp.sqrt(nv * c2) + ADAM_EPS) + ADAM_WD * w_ref[...])

    spec = pl.BlockSpec((tr, cols), lambda i: (i, 0))
    out = _sds(view, F32)
    res = pl.pallas_call(
        body, name=name, grid=(rows // tr,),
        in_specs=[spec] * 4, out_specs=(spec,) * 3, out_shape=(out,) * 3,
        compiler_params=_params(("parallel",)),
    )(w.reshape(view), g.reshape(view), m.reshape(view), v.reshape(view))
    return tuple(r.reshape(shape) for r in res)


def _sum_leading(name, x):
    k, rows, cols = x.shape
    tr = _row_tile(rows, cols)

    def body(x_ref, o_ref):
        acc = x_ref[0].astype(F32)
        for j in range(1, k):
            acc = acc + x_ref[j].astype(F32)
        o_ref[...] = acc

    return pl.pallas_call(
        body, name=name, grid=(rows // tr,),
        in_specs=[pl.BlockSpec((k, tr, cols), lambda i: (0, i, 0))],
        out_specs=pl.BlockSpec((tr, cols), lambda i: (i, 0)), out_shape=_sds((rows, cols), F32),
        compiler_params=_params(("parallel",)),
    )(x)


def _sum_chips_into_half(name, slabs):
    s, hf, r, c = slabs.shape
    rows = hf * r
    tr = _row_tile(rows, c)
    core = lax.axis_index("c").astype(jnp.int32).reshape(1)

    def body(core_ref, x_ref, o_ref):
        acc = x_ref[0].astype(F32)
        for j in range(1, s):
            acc = acc + x_ref[j].astype(F32)
        o_ref[...] = acc

    out = pl.pallas_call(
        body, name=name,
        grid_spec=pltpu.PrefetchScalarGridSpec(
            num_scalar_prefetch=1, grid=(rows // tr,),
            in_specs=[pl.BlockSpec((s, tr, c), lambda i, core_ref: (0, i, 0))],
            out_specs=pl.BlockSpec((None, tr, c), lambda i, core_ref: (core_ref[0], i, 0))),
        out_shape=_sds((2, rows, c), F32), compiler_params=_params(("parallel",)),
    )(core, slabs.reshape(s, rows, c))
    return out.reshape(2 * hf, r, c)


def _add_own_half(name, g, recv):
    n, s, r, c = g.shape
    rows = (n // 2) * s * r
    tr = _row_tile(rows, c)
    core = lax.axis_index("c").astype(jnp.int32).reshape(1)

    def body(core_ref, g_ref, r_ref, o_ref):
        o_ref[...] = (g_ref[...] + r_ref[...]).astype(BF16)

    flat = pl.BlockSpec((tr, c), lambda i, core_ref: (i, 0))
    out = pl.pallas_call(
        body, name=name,
        grid_spec=pltpu.PrefetchScalarGridSpec(
            num_scalar_prefetch=1, grid=(rows // tr,),
            in_specs=[pl.BlockSpec((None, tr, c), lambda i, core_ref: (core_ref[0], i, 0)), flat],
            out_specs=flat),
        out_shape=_sds((rows, c), BF16), compiler_params=_params(("parallel",)),
    )(core, g.reshape(2, rows, c), recv.reshape(rows, c))
    return out.reshape(n // 2, s, r, c)


_ANY = pl.BlockSpec(memory_space=pl.ANY)


def _place():
    x, y, c = lax.axis_index("x"), lax.axis_index("y"), lax.axis_index("c")
    return x, y, c, [(1 - x, y), (x, 1 - y), (1 - x, 1 - y)]


def _remote(src, dst, send_sems, recv_sems, k, to):
    return pltpu.make_async_remote_copy(src_ref=src, dst_ref=dst, send_sem=send_sems.at[k], recv_sem=recv_sems.at[k],
                                        device_id=to, device_id_type=MESH)


def _comm_call(name, body, ins, out_shapes, n_remote, n_local, aliases=None):
    scratch = [pltpu.SemaphoreType.DMA((n_remote,)), pltpu.SemaphoreType.DMA((n_remote,))]
    if n_local:
        scratch.append(pltpu.SemaphoreType.DMA((n_local,)))
    return pl.pallas_call(
        body, name=name, in_specs=[_ANY] * len(ins), out_specs=[_ANY] * len(out_shapes), out_shape=out_shapes,
        scratch_shapes=scratch, input_output_aliases=aliases or {},
    )(*ins)


def _gather_chips(name, shards):
    nt = len(shards)

    def body(*refs):
        ins, outs = refs[:nt], refs[nt:2 * nt]
        send_sems, recv_sems = refs[2 * nt:]
        x, y, c, chips = _place()
        me = 2 * x + y
        sibling = (x, y, 1 - c)
        first, passed = [], []
        for t in range(nt):
            half = ins[t].shape[0] // 2
            mine = pl.ds(c * half, half)
            first.append(_remote(ins[t], outs[t].at[me], send_sems, recv_sems, 7 * t + 6, sibling))
            first[-1].start()
            for j, chip in enumerate(chips):
                first.append(_remote(ins[t].at[mine], outs[t].at[me, mine], send_sems, recv_sems, 7 * t + j, (*chip, c)))
                first[-1].start()
        for t in range(nt):
            half = ins[t].shape[0] // 2
            mine = pl.ds(c * half, half)
            for j, chip in enumerate(chips):
                got = outs[t].at[2 * chip[0] + chip[1], mine]
                _remote(ins[t].at[mine], got, send_sems, recv_sems, 7 * t + j, (x, y, c)).wait_recv()
                passed.append(_remote(got, got, send_sems, recv_sems, 7 * t + 3 + j, sibling))
                passed[-1].start()
        for t in range(nt):
            half = ins[t].shape[0] // 2
            theirs = pl.ds((1 - c) * half, half)
            _remote(ins[t], outs[t].at[me], send_sems, recv_sems, 7 * t + 6, (x, y, c)).wait_recv()
            for j, chip in enumerate(chips):
                got = outs[t].at[2 * chip[0] + chip[1], theirs]
                _remote(ins[t].at[theirs], got, send_sems, recv_sems, 7 * t + 3 + j, (x, y, c)).wait_recv()
        for cp in first + passed:
            cp.wait_send()

    return _comm_call(name, body, shards, [_sds((N_CHIPS,) + s.shape, s.dtype) for s in shards], 7 * nt, 0)


def _other_halves_to_sibling(name, grads):
    nt = len(grads)

    def body(*refs):
        ins, outs = refs[:nt], refs[nt:2 * nt]
        send_sems, recv_sems = refs[2 * nt:]
        x, y, c, _ = _place()
        copies = []
        for t in range(nt):
            half = ins[t].shape[0] // 2
            copies.append(_remote(ins[t].at[pl.ds((1 - c) * half, half)], outs[t], send_sems, recv_sems, t, (x, y, 1 - c)))
            copies[-1].start()
        for cp in copies:
            cp.wait()

    return _comm_call(name, body, grads, [_sds((g.shape[0] // 2,) + g.shape[1:], g.dtype) for g in grads], nt, 0)


def _scatter_chips(name, parts):
    nt = len(parts)

    def body(*refs):
        ins, outs = refs[:nt], refs[nt:2 * nt]
        send_sems, recv_sems, local_sems = refs[2 * nt:]
        x, y, c, chips = _place()
        me = 2 * x + y
        local, sends = [], []
        for t in range(nt):
            layers = pl.ds(0, ins[t].shape[0])
            local.append(pltpu.make_async_copy(ins[t].at[layers, me], outs[t].at[me], local_sems.at[t]))
            local[-1].start()
            for j, chip in enumerate(chips):
                sends.append(_remote(ins[t].at[layers, 2 * chip[0] + chip[1]], outs[t].at[me], send_sems, recv_sems,
                                     3 * t + j, (*chip, c)))
                sends[-1].start()
        for t in range(nt):
            layers = pl.ds(0, ins[t].shape[0])
            for j, chip in enumerate(chips):
                _remote(ins[t].at[layers, me], outs[t].at[2 * chip[0] + chip[1]], send_sems, recv_sems, 3 * t + j,
                        (x, y, c)).wait_recv()
        for cp in sends:
            cp.wait_send()
        for cp in local:
            cp.wait()

    shapes = [_sds((p.shape[1], p.shape[0]) + p.shape[2:], p.dtype) for p in parts]
    return _comm_call(name, body, parts, shapes, 3 * nt, nt)


def _join_halves(name, bufs):
    nt = len(bufs)

    def body(*refs):
        outs = refs[nt:2 * nt]
        send_sems, recv_sems = refs[2 * nt:]
        x, y, c, _ = _place()
        sends = []
        for t in range(nt):
            half = outs[t].shape[0] // 2
            mine = pl.ds(c * half, half)
            sends.append(_remote(outs[t].at[mine], outs[t].at[mine], send_sems, recv_sems, t, (x, y, 1 - c)))
            sends[-1].start()
        for t in range(nt):
            half = outs[t].shape[0] // 2
            theirs = pl.ds((1 - c) * half, half)
            _remote(outs[t].at[theirs], outs[t].at[theirs], send_sems, recv_sems, t, (x, y, c)).wait_recv()
        for cp in sends:
            cp.wait_send()

    return _comm_call(name, body, bufs, [_sds(b.shape, b.dtype) for b in bufs], nt, 0,
                      aliases={t: t for t in range(nt)})


def _gather_all(name, block):
    rows, cols = block.shape

    def body(x_ref, out_ref, send_sems, recv_sems, local_sem):
        x, y, c, chips = _place()
        me, sibling = (x, y, c), (x, y, 1 - c)

        def at(px, py, pc):
            return out_ref.at[4 * px + 2 * py + pc]

        def copy(k, blk, to, src=None):
            return _remote(at(*blk) if src is None else src, at(*blk), send_sems, recv_sems, k, to)

        mine = pltpu.make_async_copy(x_ref, at(*me), local_sem)
        mine.start()
        first = [copy(0, me, sibling, src=x_ref)]
        first += [copy(1 + j, me, (*chip, c), src=x_ref) for j, chip in enumerate(chips)]
        for cp in first:
            cp.start()
        passed = [copy(4 + j, (*chip, c), sibling) for j, chip in enumerate(chips)]
        for j, chip in enumerate(chips):
            copy(1 + j, (*chip, c), me).wait_recv()
            passed[j].start()
        copy(0, sibling, me).wait_recv()
        for j, chip in enumerate(chips):
            copy(4 + j, (*chip, 1 - c), me).wait_recv()
        for cp in first + passed:
            cp.wait_send()
        mine.wait()

    return pl.pallas_call(
        body, name=name,
        in_specs=[pl.BlockSpec(memory_space=pltpu.VMEM)], out_specs=pl.BlockSpec(memory_space=pltpu.VMEM),
        out_shape=_sds((8, rows, cols), block.dtype),
        scratch_shapes=[pltpu.SemaphoreType.DMA((7,)), pltpu.SemaphoreType.DMA((7,)), pltpu.SemaphoreType.DMA],
        compiler_params=pltpu.CompilerParams(vmem_limit_bytes=V7X_VMEM_LIMIT),
    )(block)


WEIGHTS = ("ffn1_norm", "ffn1_w_gu", "ffn1_w_down", "mix_norm", "mem_norm", "w_mem_kv", "xq_norm", "xk_norm", "w_out",
           "ffn2_norm", "ffn2_w_gu", "ffn2_w_down", "sb_w_in", "s5_w_in", "s5_log_dt", "s5_a_re", "s5_a_im", "s5_b_re",
           "s5_b_im", "s5_c_re", "s5_c_im", "s5_d", "s5_w_glu")
BIG = ("ffn1_w_gu", "ffn1_w_down", "w_mem_kv", "w_out", "ffn2_w_gu", "ffn2_w_down", "sb_w_in", "s5_w_in", "s5_w_glu")
COLUMN_SHARDED = ("ffn1_w_gu", "ffn2_w_gu", "sb_w_in")
PER_MIXER = ("sb_w_in", "s5_w_in", "s5_w_glu")
SMALL = tuple(n for n in WEIGHTS if n not in BIG)


def _layer_weights(full, i):
    return {name: _W(full[name], i // 2 if name in PER_MIXER else i, name in COLUMN_SHARDED) for name in BIG}


def _row(v):
    return v.reshape(1, -1)


def _layer_fwd(i, x, mem, w, p, s5prep):
    tag = f"l{i}"
    j = i // 2
    x1, sv_ffn1 = _ffn_fwd(f"{tag}_ffn1", x, _row(p["ffn1_norm"][i]), w["ffn1_w_gu"], w["ffn1_w_down"])
    h = _rmsnorm_fwd(f"{tag}_mixnorm", x1, _row(p["mix_norm"][i]))
    if i % 2 == 0:
        proj = _mm_nn(f"{tag}_inproj", h, w["sb_w_in"])
        n_heads = (proj.shape[1] * 3 // 10) // HEAD_DIM
        tok, mix_saved = _sb_fwd(f"{tag}_sb", proj, n_heads)
        tok_w = n_heads * HEAD_DIM
        qcol0 = 3 * tok_w
    else:
        proj = _mm_nn(f"{tag}_inproj", h, w["s5_w_in"])
        tok, mix_saved = _s5_fwd(f"{tag}_s5", proj, s5prep[j], _row(p["s5_d"][j]), w["s5_w_glu"])
        tok_w = tok.shape[1]
        qcol0 = tok_w
    mem_h = _rmsnorm_fwd(f"{tag}_memnorm", mem, _row(p["mem_norm"][i]))
    kv = _mm_nn(f"{tag}_memkv", mem_h, w["w_mem_kv"])
    gq, gk = _row(p["xq_norm"][i]), _row(p["xk_norm"][i])
    cross = _mem_fwd(f"{tag}_mem", proj, qcol0, kv, gq, gk)
    cat = jnp.concatenate([tok, cross], axis=1).astype(BF16)
    x2 = _mm_nn(f"{tag}_outproj", cat, w["w_out"], extras=(x1,), epilogue=_add_residual)
    x3, sv_ffn2 = _ffn_fwd(f"{tag}_ffn2", x2, _row(p["ffn2_norm"][i]), w["ffn2_w_gu"], w["ffn2_w_down"])
    saved = dict(ffn1=sv_ffn1, x1=x1, h=h, proj=proj, mix=mix_saved, mem_h=mem_h, kv=kv, cat=cat, ffn2=sv_ffn2,
                 tok_w=tok_w, qcol0=qcol0)
    return x3, saved


def _layer_bwd(i, sv, mem, w, p, s5prep, dx3, dx3b):
    tag = f"l{i}b"
    j = i // 2
    g = {}
    dx2, dx2b, g["ffn2_norm"], g["ffn2_w_gu"], g["ffn2_w_down"] = _ffn_bwd(
        f"{tag}_ffn2", sv["ffn2"], _row(p["ffn2_norm"][i]), w["ffn2_w_gu"], w["ffn2_w_down"], dx3, dx3b)
    dcat = _mm_nt(f"{tag}_dcat", dx2b, w["w_out"])
    g["w_out"] = _mm_tn(f"{tag}_dwout", sv["cat"], dx2b)
    gq, gk = _row(p["xq_norm"][i]), _row(p["xk_norm"][i])
    tok_w, qcol0 = sv["tok_w"], sv["qcol0"]
    dqm, dkv, g["xq_norm"], g["xk_norm"] = _mem_bwd(f"{tag}_mem", sv["proj"], qcol0, sv["kv"], gq, gk, dcat, tok_w)
    dkvb = dkv.astype(BF16)
    g["w_mem_kv"] = _mm_tn(f"{tag}_dwkv", sv["mem_h"], dkvb)
    dmem_h = _mm_nt(f"{tag}_dmemh", dkvb, w["w_mem_kv"])
    g["mem_norm"] = _rmsnorm_bwd(f"{tag}_dmemnorm", mem, dmem_h, _row(p["mem_norm"][i]))
    if i % 2 == 0:
        dq, dk, dv = _sb_bwd(f"{tag}_sb", sv["proj"], sv["mix"], dcat, tok_w // HEAD_DIM)
        dproj = jnp.concatenate([dq, dk, dv, dqm], axis=1).astype(BF16)
        g["sb_w_in"] = _mm_tn(f"{tag}_dwin", sv["h"], dproj, shards=N_CHIPS)
        dh = _mm_nt(f"{tag}_dh", dproj, w["sb_w_in"])
    else:
        du, g["s5_prep"], g["s5_d"], g["s5_w_glu"] = _s5_bwd(
            f"{tag}_s5", sv["proj"], s5prep[j], _row(p["s5_d"][j]), w["s5_w_glu"], sv["mix"], dcat)
        dproj = jnp.concatenate([du, dqm], axis=1).astype(BF16)
        g["s5_w_in"] = _mm_tn(f"{tag}_dwin", sv["h"], dproj)
        dh = _mm_nt(f"{tag}_dh", dproj, w["s5_w_in"])
    dx1, dx1b, g["mix_norm"] = _rmsnorm_bwd(f"{tag}_dmixnorm", sv["x1"], dh, _row(p["mix_norm"][i]), dx2)
    dx, dxb, g["ffn1_norm"], g["ffn1_w_gu"], g["ffn1_w_down"] = _ffn_bwd(
        f"{tag}_ffn1", sv["ffn1"], _row(p["ffn1_norm"][i]), w["ffn1_w_gu"], w["ffn1_w_down"], dx1, dx1b)
    return dx, dxb, g


def _local_step(x, mem, target, full, p):
    depth = p["ffn1_norm"].shape[0]
    n_s5 = depth // 2
    s5_names = ("s5_log_dt", "s5_a_re", "s5_a_im", "s5_b_re", "s5_b_im", "s5_c_re", "s5_c_im")
    s5prep, s5vjp = [], []
    for j in range(n_s5):
        out, vjp = jax.vjp(_s5_prep, *[p[n][j] for n in s5_names])
        s5prep.append(out)
        s5vjp.append(vjp)
    ws = [_layer_weights(full, i) for i in range(depth)]
    saved = []
    for i in range(depth):
        x, sv = _layer_fwd(i, x, mem, ws[i], p, s5prep)
        saved.append(sv)
    loss, dx, dxb = _loss_head("loss", x, target)
    per_layer = [None] * depth
    for i in reversed(range(depth)):
        dx, dxb, per_layer[i] = _layer_bwd(i, saved[i], mem, ws[i], p, s5prep, dx, dxb)
    grads = {}
    for name in ("ffn1_norm", "mix_norm", "mem_norm", "xq_norm", "xk_norm", "ffn2_norm"):
        grads[name] = jnp.concatenate([per_layer[i][name] for i in range(depth)], axis=0)
    grads["s5_d"] = jnp.concatenate([per_layer[i]["s5_d"] for i in range(1, depth, 2)], axis=0)
    s5g = [s5vjp[j](tuple(per_layer[2 * j + 1]["s5_prep"])) for j in range(n_s5)]
    for k, name in enumerate(s5_names):
        grads[name] = jnp.stack([s5g[j][k] for j in range(n_s5)], axis=0)
    for name in BIG:
        layers = range(depth) if name not in PER_MIXER else range(0 if name == "sb_w_in" else 1, depth, 2)
        per = [per_layer[i][name] for i in layers]
        if name not in COLUMN_SHARDED:
            per = [gl.reshape((N_CHIPS, gl.shape[0] // N_CHIPS, gl.shape[1])) for gl in per]
        grads[name] = jnp.stack(per, axis=0)
    return loss, dx, grads


def _reduce_big(grads):
    gs = [grads[n] for n in BIG]
    from_sibling = _other_halves_to_sibling("reduce_to_sibling", gs)
    parts = [_add_own_half(f"reduce_add_cores_{n}", g, r) for n, g, r in zip(BIG, gs, from_sibling)]
    slabs = _scatter_chips("reduce_scatter_chips", parts)
    bufs = [_sum_chips_into_half(f"reduce_add_chips_{n}", sl) for n, sl in zip(BIG, slabs)]
    return dict(zip(BIG, _join_halves("reduce_join_halves", bufs)))


def _reduce_small(grads, shapes):
    flat = jnp.concatenate([grads[n].reshape(-1) for n in SMALL])
    total = flat.shape[0]
    rows = -(-total // (512 * LANES)) * 512
    block = jnp.pad(flat, (0, rows * LANES - total)).reshape(rows, LANES)
    summed = _sum_leading("reduce_small_sum", _gather_all("reduce_small_gather", block)).reshape(-1)
    out, off = {}, 0
    for n in SMALL:
        size = math.prod(shapes[n])
        out[n] = summed[off:off + size].reshape(shapes[n])
        off += size
    return out


def kernel(x, mem, ffn1_norm, ffn1_w_gu, ffn1_w_down, mix_norm, mem_norm, w_mem_kv, xq_norm, xk_norm, w_out, ffn2_norm, ffn2_w_gu, ffn2_w_down, sb_w_in, s5_w_in, s5_log_dt, s5_a_re, s5_a_im, s5_b_re, s5_b_im, s5_c_re, s5_c_im, s5_d, s5_w_glu, loss_target, m_ffn1_norm, m_ffn1_w_gu, m_ffn1_w_down, m_mix_norm, m_mem_norm, m_w_mem_kv, m_xq_norm, m_xk_norm, m_w_out, m_ffn2_norm, m_ffn2_w_gu, m_ffn2_w_down, m_sb_w_in, m_s5_w_in, m_s5_log_dt, m_s5_a_re, m_s5_a_im, m_s5_b_re, m_s5_b_im, m_s5_c_re, m_s5_c_im, m_s5_d, m_s5_w_glu, v_ffn1_norm, v_ffn1_w_gu, v_ffn1_w_down, v_mix_norm, v_mem_norm, v_w_mem_kv, v_xq_norm, v_xk_norm, v_w_out, v_ffn2_norm, v_ffn2_w_gu, v_ffn2_w_down, v_sb_w_in, v_s5_w_in, v_s5_log_dt, v_s5_a_re, v_s5_a_im, v_s5_b_re, v_s5_b_im, v_s5_c_re, v_s5_c_im, v_s5_d, v_s5_w_glu):
    given = dict(locals())
    wts = {n: given[n] for n in WEIGHTS}
    chip = 2 * lax.axis_index("x") + lax.axis_index("y")

    d_sh = s5_d.shape
    d_rows = -(-math.prod(d_sh) // (8 * LANES)) * 8
    d_block = jnp.pad(s5_d.reshape(-1), (0, d_rows * LANES - math.prod(d_sh))).reshape(d_rows, LANES)
    d_all = _gather_all("gather_s5_d", d_block).reshape(8, -1)[:, :math.prod(d_sh)]
    d_full = jnp.concatenate([d_all[2 * s].reshape(d_sh) for s in range(N_CHIPS)], axis=1)

    gathered = _gather_chips("gather_weights", [wts[n].astype(BF16) for n in BIG])
    p = {n: wts[n] for n in SMALL}
    p["s5_d"] = d_full
    loss, grad_x, grads = _local_step(x[0], mem[0], loss_target[0], dict(zip(BIG, gathered)), p)

    small_shapes = {n: wts[n].shape for n in SMALL}
    small_shapes["s5_d"] = d_full.shape
    gsum = _reduce_big(grads)
    gsmall = _reduce_small(grads, small_shapes)
    gsmall["s5_d"] = lax.dynamic_slice_in_dim(gsmall["s5_d"], chip * d_sh[1], d_sh[1], axis=1)
    gsum.update(gsmall)

    deltas, new_m, new_v = {}, {}, {}
    for n in WEIGHTS:
        deltas[n], new_m[n], new_v[n] = _adamw(f"adamw_{n}", wts[n], gsum[n], given["m_" + n], given["v_" + n])
    total_loss = lax.psum(loss[0, 0], ("x", "y", "c"))
    return (total_loss, grad_x[None], *[gsum[n] for n in WEIGHTS], *[deltas[n] for n in WEIGHTS],
            *[new_m[n] for n in WEIGHTS], *[new_v[n] for n in WEIGHTS])
```

```python
import math
from typing import NamedTuple

import jax
import jax.numpy as jnp
from jax import lax
from jax.experimental import pallas as pl
from jax.experimental.pallas import tpu as pltpu

F32 = jnp.float32
BF16 = jnp.bfloat16

HEAD_DIM = 128
S5_GROUP = 16
S5_STATE = 64
EPS = 1e-6
ADAM_LR = 0.001
ADAM_B1 = 0.9
ADAM_B2 = 0.999
ADAM_EPS = 1e-08
ADAM_WD = 0.01
ADAM_STEP = 10

LANES = 128
V7X_VMEM_LIMIT = 56 * 1024 * 1024
N_CHIPS = 4
MESH = pl.DeviceIdType.MESH
_NT = (((1,), (1,)), ((), ()))
_TN = (((0,), (0,)), ((), ()))


def _params(sem):
    return pltpu.CompilerParams(dimension_semantics=sem, vmem_limit_bytes=V7X_VMEM_LIMIT)


def _tile(n, want, mult=LANES):
    if n <= want:
        return n
    t = (want // mult) * mult
    while t > mult and n % t:
        t -= mult
    assert n % t == 0, (n, want, mult)
    return t


def _row_tile(rows, cols, itemsize=4, target=1 << 20):
    return _tile(rows, max(16, (target // (cols * itemsize)) // 16 * 16), 16)


def _sds(shape, dtype):
    return jax.ShapeDtypeStruct(tuple(shape), dtype)


class _W(NamedTuple):
    arr: jax.Array
    layer: int
    cols: bool


def _wspec(w, tr, tc, rb, cb):
    _, _, r, c = w.arr.shape
    layer = w.layer
    assert r % tr == 0 and c % tc == 0, (r, c, tr, tc)
    if w.cols:
        nb = c // tc
        return pl.BlockSpec((None, None, tr, tc), lambda *g: (cb(*g) // nb, layer, rb(*g), cb(*g) % nb))
    nb = r // tr
    return pl.BlockSpec((None, None, tr, tc), lambda *g: (rb(*g) // nb, layer, rb(*g) % nb, cb(*g)))


_DIMS = {"nn": (((1,), (0,)), ((), ())), "nt": _NT, "tn": _TN}


def _matmul(name, mode, a, b, *, grid, a_spec, b_spec, out_shape, out_spec, acc_shape,
            extras=(), extra_specs=(), epilogue=None):
    nk = grid[2]
    ne = len(extras)
    multi = isinstance(out_shape, (tuple, list))
    outs = tuple(out_shape) if multi else (out_shape,)
    ospecs = tuple(out_spec) if multi else (out_spec,)
    no = len(outs)

    def body(a_ref, b_ref, *rest):
        ex = rest[:ne]
        out_refs = rest[ne:ne + no]
        part = lax.dot_general(a_ref[...].astype(BF16), b_ref[...].astype(BF16), _DIMS[mode],
                               preferred_element_type=F32)

        def finish(acc):
            if epilogue is None:
                out_refs[0][...] = acc.astype(out_refs[0].dtype)
            else:
                epilogue(acc, ex, out_refs)

        if nk == 1:
            finish(part)
        else:
            acc_ref = rest[-1]
            k = pl.program_id(2)

            @pl.when(k == 0)
            def _():
                acc_ref[...] = part

            @pl.when(k > 0)
            def _():
                acc_ref[...] += part

            @pl.when(k == nk - 1)
            def _():
                finish(acc_ref[...])

    return pl.pallas_call(
        body, name=name, grid=grid,
        in_specs=[a_spec, b_spec, *extra_specs],
        out_specs=ospecs if multi else ospecs[0],
        out_shape=outs if multi else outs[0],
        scratch_shapes=[pltpu.VMEM(acc_shape, F32)] if nk > 1 else [],
        compiler_params=_params(("parallel", "parallel", "arbitrary")),
    )(a, b, *extras)


def _mm_nn(name, a, w, out_dtype=F32, extras=(), epilogue=None, n_out=1):
    m, kk = a.shape
    _, _, r, c = w.arr.shape
    tm = _tile(m, 1024, 8)
    tk = _tile(r, 2048)
    if w.cols:
        n, tn = N_CHIPS * c, _tile(c, 256)
    else:
        n, tn = c, _tile(c, 512)
    ospec = pl.BlockSpec((tm, tn), lambda i, j, k: (i, j))
    out_shape = _sds((m, n), out_dtype)
    return _matmul(name, "nn", a, w.arr, grid=(m // tm, n // tn, kk // tk),
                   a_spec=pl.BlockSpec((tm, tk), lambda i, j, k: (i, k)),
                   b_spec=_wspec(w, tk, tn, lambda i, j, k: k, lambda i, j, k: j),
                   out_shape=out_shape if n_out == 1 else (out_shape,) * n_out,
                   out_spec=ospec if n_out == 1 else (ospec,) * n_out, acc_shape=(tm, tn),
                   extras=extras, extra_specs=(ospec,) * len(extras), epilogue=epilogue)


def _add_residual(acc, ex, outs):
    outs[0][...] = ex[0][...] + acc


def _mm_nt(name, a, w, extras=(), epilogue=None):
    m, n = a.shape
    _, _, r, c = w.arr.shape
    tm = _tile(m, 512, 8)
    if w.cols:
        kk, tr = r, r
        tc = c if r * c * 2 <= (12 << 20) else _tile(c, 1408)
    else:
        kk, tr, tc = N_CHIPS * r, _tile(r, 512), _tile(c, 2048)
    ospec = pl.BlockSpec((tm, tr), lambda i, j, k: (i, j))
    return _matmul(name, "nt", a, w.arr, grid=(m // tm, kk // tr, n // tc),
                   a_spec=pl.BlockSpec((tm, tc), lambda i, j, k: (i, k)),
                   b_spec=_wspec(w, tr, tc, lambda i, j, k: j, lambda i, j, k: k),
                   out_shape=_sds((m, kk), F32), out_spec=ospec, acc_shape=(tm, tr),
                   extras=extras, extra_specs=(ospec,) * len(extras), epilogue=epilogue)


def _mm_tn(name, a, b, shards=None):
    m, kk = a.shape
    n = b.shape[1]
    tkm = _tile(m, 1024, 8)
    tk = _tile(kk, 512)
    if shards:
        ns = n // shards
        return _matmul(name, "tn", a, b, grid=(kk // tk, shards, m // tkm),
                       a_spec=pl.BlockSpec((tkm, tk), lambda i, j, k: (k, i)),
                       b_spec=pl.BlockSpec((tkm, ns), lambda i, j, k: (k, j)),
                       out_shape=_sds((shards, kk, ns), F32),
                       out_spec=pl.BlockSpec((None, tk, ns), lambda i, j, k: (j, i, 0)), acc_shape=(tk, ns))
    tn = _tile(n, 2048)
    return _matmul(name, "tn", a, b, grid=(kk // tk, n // tn, m // tkm),
                   a_spec=pl.BlockSpec((tkm, tk), lambda i, j, k: (k, i)),
                   b_spec=pl.BlockSpec((tkm, tn), lambda i, j, k: (k, j)),
                   out_shape=_sds((kk, n), F32), out_spec=pl.BlockSpec((tk, tn), lambda i, j, k: (i, j)),
                   acc_shape=(tk, tn))


def _rmsnorm_fwd(name, x, gain):
    m, d = x.shape
    tm = _tile(m, 512, 8)

    def body(x_ref, g_ref, o_ref):
        xv = x_ref[...]
        r = lax.rsqrt(jnp.mean(xv * xv, axis=-1, keepdims=True) + EPS)
        o_ref[...] = (xv * r * g_ref[...]).astype(o_ref.dtype)

    return pl.pallas_call(
        body, name=name, grid=(m // tm,),
        in_specs=[pl.BlockSpec((tm, d), lambda i: (i, 0)), pl.BlockSpec((1, d), lambda i: (0, 0))],
        out_specs=pl.BlockSpec((tm, d), lambda i: (i, 0)),
        out_shape=_sds((m, d), BF16),
        compiler_params=_params(("parallel",)),
    )(x, gain)


def _rmsnorm_bwd(name, x, dh, gain, dres=None):
    m, d = x.shape
    tm = _tile(m, 256, 8)
    nsteps = m // tm
    with_dx = dres is not None

    def body(*refs):
        if with_dx:
            x_ref, dh_ref, g_ref, dres_ref, dx_ref, dxb_ref, dg_ref, acc_ref = refs
        else:
            x_ref, dh_ref, g_ref, dg_ref, acc_ref = refs
        i = pl.program_id(0)
        xv = x_ref[...]
        r = lax.rsqrt(jnp.mean(xv * xv, axis=-1, keepdims=True) + EPS)
        xh = xv * r
        dhv = dh_ref[...].astype(F32)
        contrib = (dhv * xh).reshape(tm // 8, 8, d).sum(axis=0)

        @pl.when(i == 0)
        def _():
            acc_ref[...] = contrib

        @pl.when(i > 0)
        def _():
            acc_ref[...] += contrib

        @pl.when(i == nsteps - 1)
        def _():
            dg_ref[...] = jnp.sum(acc_ref[...], axis=0, keepdims=True)

        if with_dx:
            dxh = dhv * g_ref[...]
            dx = r * (dxh - xh * jnp.mean(dxh * xh, axis=-1, keepdims=True)) + dres_ref[...]
            dx_ref[...] = dx
            dxb_ref[...] = dx.astype(BF16)

    row = pl.BlockSpec((tm, d), lambda i: (i, 0))
    vec = pl.BlockSpec((1, d), lambda i: (0, 0))
    if with_dx:
        return pl.pallas_call(
            body, name=name, grid=(nsteps,),
            in_specs=[row, row, vec, row], out_specs=(row, row, vec),
            out_shape=(_sds((m, d), F32), _sds((m, d), BF16), _sds((1, d), F32)),
            scratch_shapes=[pltpu.VMEM((8, d), F32)],
            compiler_params=_params(("arbitrary",)),
        )(x, dh, gain, dres)
    return pl.pallas_call(
        body, name=name, grid=(nsteps,),
        in_specs=[row, row, vec], out_specs=vec,
        out_shape=_sds((1, d), F32),
        scratch_shapes=[pltpu.VMEM((8, d), F32)],
        compiler_params=_params(("arbitrary",)),
    )(x, dh, gain)


def _ffn_fwd(tag, x, gain, wgu, wdown):
    m, d = x.shape
    ns = wgu.arr.shape[3]
    f = 2 * ns
    h = _rmsnorm_fwd(f"{tag}_norm", x, gain)
    tm = _tile(m, 1024, 8)
    tn = _tile(ns, 256)
    nb = ns // tn

    def gu_body(h_ref, wg_ref, wu_ref, act_ref, gu_ref):
        hv = h_ref[...]
        g = jnp.dot(hv, wg_ref[...], preferred_element_type=F32)
        u = jnp.dot(hv, wu_ref[...], preferred_element_type=F32)
        act_ref[...] = (g * jax.nn.sigmoid(g) * u).astype(BF16)
        gu_ref[0] = g.astype(BF16)
        gu_ref[1] = u.astype(BF16)

    act, gu = pl.pallas_call(
        gu_body, name=f"{tag}_gu", grid=(m // tm, 2 * nb),
        in_specs=[pl.BlockSpec((tm, d), lambda i, j: (i, 0)),
                  _wspec(wgu, d, tn, lambda i, j: 0, lambda i, j: j),
                  _wspec(wgu, d, tn, lambda i, j: 0, lambda i, j: 2 * nb + j)],
        out_specs=(pl.BlockSpec((tm, tn), lambda i, j: (i, j)),
                   pl.BlockSpec((2, tm, tn), lambda i, j: (0, i, j))),
        out_shape=(_sds((m, f), BF16), _sds((2, m, f), BF16)),
        compiler_params=_params(("parallel", "parallel")),
    )(h, wgu.arr, wgu.arr)

    tk = wdown.arr.shape[2]
    tnd = _tile(d, 1024)

    def down_epilogue(acc, ex, outs):
        outs[0][...] = ex[0][...] + 0.5 * acc

    ospec = pl.BlockSpec((tm, tnd), lambda i, j, k: (i, j))
    y = _matmul(
        f"{tag}_down", "nn", act, wdown.arr, grid=(m // tm, d // tnd, f // tk),
        a_spec=pl.BlockSpec((tm, tk), lambda i, j, k: (i, k)),
        b_spec=_wspec(wdown, tk, tnd, lambda i, j, k: k, lambda i, j, k: j),
        out_shape=_sds((m, d), F32), out_spec=ospec, acc_shape=(tm, tnd),
        extras=(x,), extra_specs=(ospec,), epilogue=down_epilogue)
    return y, (x, h, act, gu)


def _ffn_bwd(tag, saved, gain, wgu, wdown, dy, dyb):
    x, h, act, gu = saved
    m, d = x.shape
    ns = wgu.arr.shape[3]
    f = 2 * ns

    def dact_epilogue(acc, ex, outs):
        g = ex[0][0].astype(F32)
        u = ex[0][1].astype(F32)
        da = 0.5 * acc
        s = jax.nn.sigmoid(g)
        outs[0][0] = (da * u * s * (1.0 + g * (1.0 - s))).astype(BF16)
        outs[0][1] = (da * g * s).astype(BF16)

    tma = _tile(m, 512, 8)
    tna = wdown.arr.shape[2]
    gspec = pl.BlockSpec((2, tma, tna), lambda i, j, k: (0, i, j))
    dgu = _matmul(
        f"{tag}_dact", "nt", dyb, wdown.arr, grid=(m // tma, f // tna, 1),
        a_spec=pl.BlockSpec((tma, d), lambda i, j, k: (i, 0)),
        b_spec=_wspec(wdown, tna, d, lambda i, j, k: j, lambda i, j, k: 0),
        out_shape=_sds((2, m, f), BF16), out_spec=gspec, acc_shape=(tma, tna),
        extras=(gu,), extra_specs=(gspec,), epilogue=dact_epilogue)

    tkm = _tile(m, 2048, 8)
    tf = _tile(f, 1408)
    tnd = _tile(d, 1024)

    def half_epilogue(acc, ex, outs):
        outs[0][...] = 0.5 * acc

    dwdown = _matmul(
        f"{tag}_dwdown", "tn", act, dyb, grid=(f // tf, d // tnd, m // tkm),
        a_spec=pl.BlockSpec((tkm, tf), lambda i, j, k: (k, i)),
        b_spec=pl.BlockSpec((tkm, tnd), lambda i, j, k: (k, j)),
        out_shape=_sds((f, d), F32), out_spec=pl.BlockSpec((tf, tnd), lambda i, j, k: (i, j)),
        acc_shape=(tf, tnd), epilogue=half_epilogue)

    td = _tile(d, 512)
    dwgu = _matmul(
        f"{tag}_dwgu", "tn", h, dgu, grid=(d // td, 4, m // tkm),
        a_spec=pl.BlockSpec((tkm, td), lambda i, j, k: (k, i)),
        b_spec=pl.BlockSpec((None, tkm, ns), lambda i, j, k: (j // 2, k, j % 2)),
        out_shape=_sds((4, d, ns), F32), out_spec=pl.BlockSpec((None, td, ns), lambda i, j, k: (j, i, 0)),
        acc_shape=(td, ns))

    tmh = _tile(m, 512, 8)
    dh = _matmul(
        f"{tag}_dh", "nt", dgu, wgu.arr, grid=(m // tmh, 1, 4),
        a_spec=pl.BlockSpec((None, tmh, ns), lambda i, j, k: (k // 2, i, k % 2)),
        b_spec=_wspec(wgu, d, ns, lambda i, j, k: 0, lambda i, j, k: k),
        out_shape=_sds((m, d), F32), out_spec=pl.BlockSpec((tmh, d), lambda i, j, k: (i, 0)),
        acc_shape=(tmh, d))

    dx, dxb, dgain = _rmsnorm_bwd(f"{tag}_dnorm", x, dh, gain, dy)
    return dx, dxb, dgain, dwgu, dwdown


SB_KEY_BLOCK = 256
SB_QUERY_ROWS = 512


def _split2(x):
    hi = x.astype(BF16)
    return hi, (x - hi.astype(F32)).astype(BF16)


def _dot2(parts, mat):
    return jnp.dot(parts[0], mat, preferred_element_type=F32) + jnp.dot(parts[1], mat, preferred_element_type=F32)


def _sb_logits(q, k, scale):
    z = lax.dot_general(q, k, _NT, preferred_element_type=F32) * scale
    lp = jnp.minimum(z, 0.0) - jnp.log(1.0 + jnp.exp(-jnp.abs(z)))
    return lp, lp - z


def _comm_hooks(comm, n_in, n_out, grid):
    if comm is None:
        return (lambda refs: None), (lambda refs: None)
    nci, nco = len(comm.ins), len(comm.out_shapes)

    def comm_refs(refs):
        return refs[n_in:n_in + nci] + refs[n_in + nci + n_out:]

    def at_first(refs):
        @pl.when(sum(pl.program_id(a) for a in range(len(grid))) == 0)
        def _():
            comm.start(comm_refs(refs))

    def at_last(refs):
        @pl.when(sum(pl.program_id(a) for a in range(len(grid))) == sum(g - 1 for g in grid))
        def _():
            comm.finish(comm_refs(refs))

    return at_first, at_last


def _sb_fwd(name, qkv, n_heads, comm=None):
    length = qkv.shape[0]
    kb = _tile(length, SB_KEY_BLOCK, LANES)
    tq = _tile(length, SB_QUERY_ROWS, kb)
    r = tq // kb
    scale = 1.0 / math.sqrt(HEAD_DIM)
    grid = (n_heads, length // tq)
    nci = len(comm.ins) if comm else 0
    at_first, at_last = _comm_hooks(comm, 3, 2, grid)

    def body(*refs):
        q_ref, k_ref, v_ref = refs[:3]
        o_ref, tot_ref = refs[3 + nci:5 + nci]
        at_first(refs)
        qi = pl.program_id(1)
        q = q_ref[...].astype(BF16)
        row = qi * tq + lax.broadcasted_iota(jnp.int32, (tq, kb), 0)
        col = lax.broadcasted_iota(jnp.int32, (tq, kb), 1)
        later = (lax.broadcasted_iota(jnp.int32, (kb, kb), 0) > lax.broadcasted_iota(jnp.int32, (kb, kb), 1)).astype(BF16)

        def block(kbi, carry, masked):
            c, acc = carry
            ks = pl.multiple_of(kbi * kb, kb)
            k = k_ref[pl.ds(ks, kb), :].astype(BF16)
            v = v_ref[pl.ds(ks, kb), :].astype(BF16)
            lp, ln = _sb_logits(q, k, scale)
            if masked:
                valid = (ks + col) < row
                ln = jnp.where(valid, ln, 0.0)
            w = jnp.exp(lp + (c + _dot2(_split2(ln), later)))
            if masked:
                w = jnp.where(valid, w, 0.0)
            acc = acc + jnp.dot(w.astype(BF16), v, preferred_element_type=F32)
            return c + jnp.sum(ln, axis=1, keepdims=True), acc

        carry = (jnp.zeros((tq, 1), F32), jnp.zeros((tq, HEAD_DIM), F32))
        for dgl in range(r - 1, -1, -1):
            carry = block(qi * r + dgl, carry, True)
        carry = lax.fori_loop(0, qi * r, lambda i, cr: block(qi * r - 1 - i, cr, False), carry)
        o_ref[...] = carry[1]
        tot_ref[...] = jnp.broadcast_to(carry[0], (tq, HEAD_DIM))
        at_last(refs)

    h = n_heads
    qblk = pl.BlockSpec((tq, HEAD_DIM), lambda hh, i: (i, hh))
    out = _sds((length, h * HEAD_DIM), F32)
    res = pl.pallas_call(
        body, name=name, grid=grid,
        in_specs=[qblk,
                  pl.BlockSpec((length, HEAD_DIM), lambda hh, i: (0, h + hh)),
                  pl.BlockSpec((length, HEAD_DIM), lambda hh, i: (0, 2 * h + hh))] + [_ANY] * nci,
        out_specs=(qblk, qblk) + ((_ANY,) * len(comm.out_shapes) if comm else ()),
        out_shape=(out, out) + (tuple(comm.out_shapes) if comm else ()),
        scratch_shapes=comm.scratch() if comm else [],
        compiler_params=_params(("arbitrary", "arbitrary") if comm else ("parallel", "arbitrary")),
    )(qkv, qkv, qkv, *(comm.ins if comm else ()))
    return (res[0], res[1]), tuple(res[2:])


def _sb_bwd(name, qkv, tot, do, n_heads, comm=None):
    length = qkv.shape[0]
    kb = _tile(length, SB_KEY_BLOCK, LANES)
    tq = _tile(length, SB_QUERY_ROWS, kb)
    r = tq // kb
    scale = 1.0 / math.sqrt(HEAD_DIM)
    grid = (n_heads, length // tq)
    nci = len(comm.ins) if comm else 0
    at_first, at_last = _comm_hooks(comm, 5, 3, grid)

    def body(*refs):
        q_ref, k_ref, v_ref, tot_ref, do_ref = refs[:5]
        dq_ref, dk_ref, dv_ref = refs[5 + nci:8 + nci]
        at_first(refs)
        qi = pl.program_id(1)

        @pl.when(qi == 0)
        def _():
            dk_ref[...] = jnp.zeros_like(dk_ref)
            dv_ref[...] = jnp.zeros_like(dv_ref)

        q = q_ref[...].astype(BF16)
        dob = do_ref[...].astype(BF16)
        tot = tot_ref[:, 0:1]
        row = qi * tq + lax.broadcasted_iota(jnp.int32, (tq, kb), 0)
        col = lax.broadcasted_iota(jnp.int32, (tq, kb), 1)
        jj = lax.broadcasted_iota(jnp.int32, (kb, kb), 0)
        ss = lax.broadcasted_iota(jnp.int32, (kb, kb), 1)
        later = (jj > ss).astype(BF16)
        before = (jj < ss).astype(BF16)

        def block(kbi, carry, masked):
            pl_, pe, dq = carry
            ks = pl.multiple_of(kbi * kb, kb)
            k = k_ref[pl.ds(ks, kb), :].astype(BF16)
            v = v_ref[pl.ds(ks, kb), :].astype(BF16)
            lp, ln_raw = _sb_logits(q, k, scale)
            ln = ln_raw
            if masked:
                valid = (ks + col) < row
                ln = jnp.where(valid, ln_raw, 0.0)
            tb = jnp.sum(ln, axis=1, keepdims=True)
            w = jnp.exp(lp + ((tot - pl_ - tb) + _dot2(_split2(ln), later)))
            if masked:
                w = jnp.where(valid, w, 0.0)
            e = w * lax.dot_general(dob, v, _NT, preferred_element_type=F32)
            dv_ref[pl.ds(ks, kb), :] += lax.dot_general(w.astype(BF16), dob, _TN, preferred_element_type=F32)
            beta = jnp.exp(lp)
            dz = e * (1.0 - beta) - beta * (pe + _dot2(_split2(e), before))
            if masked:
                dz = jnp.where(valid, dz, 0.0)
            dzb = (dz * scale).astype(BF16)
            dq = dq + jnp.dot(dzb, k, preferred_element_type=F32)
            dk_ref[pl.ds(ks, kb), :] += lax.dot_general(dzb, q, _TN, preferred_element_type=F32)
            return pl_ + tb, pe + jnp.sum(e, axis=1, keepdims=True), dq

        carry = (jnp.zeros((tq, 1), F32), jnp.zeros((tq, 1), F32), jnp.zeros((tq, HEAD_DIM), F32))
        carry = lax.fori_loop(0, qi * r, lambda i, cr: block(i, cr, False), carry)
        for dgl in range(r):
            carry = block(qi * r + dgl, carry, True)
        dq_ref[...] = carry[2]
        at_last(refs)

    h = n_heads
    qblk = pl.BlockSpec((tq, HEAD_DIM), lambda hh, i: (i, hh))
    full = pl.BlockSpec((length, HEAD_DIM), lambda hh, i: (0, hh))
    out = _sds((length, h * HEAD_DIM), F32)
    res = pl.pallas_call(
        body, name=name, grid=grid,
        in_specs=[qblk,
                  pl.BlockSpec((length, HEAD_DIM), lambda hh, i: (0, h + hh)),
                  pl.BlockSpec((length, HEAD_DIM), lambda hh, i: (0, 2 * h + hh)),
                  qblk, qblk] + [_ANY] * nci,
        out_specs=(qblk, full, full) + ((_ANY,) * len(comm.out_shapes) if comm else ()),
        out_shape=(out, out, out) + (tuple(comm.out_shapes) if comm else ()),
        scratch_shapes=comm.scratch() if comm else [],
        compiler_params=_params(("arbitrary", "arbitrary") if comm else ("parallel", "arbitrary")),
    )(qkv, qkv, qkv, tot, do, *(comm.ins if comm else ()))
    return tuple(res[:3]), tuple(res[3:])


def _head_rms(xh):
    r = lax.rsqrt(jnp.mean(xh * xh, axis=-1, keepdims=True) + EPS)
    return xh * r, r


def _mem_fwd(name, qsrc, qcol0, kv, gq, gk):
    length = qsrc.shape[0]
    mm, mw2 = kv.shape
    mw = mw2 // 2
    nh = mw // HEAD_DIM
    tq = _tile(length, 512, 8)
    inv = 1.0 / math.sqrt(HEAD_DIM)
    assert qcol0 % mw == 0

    def body(q_ref, kv_ref, gq_ref, gk_ref, o_ref):
        for hh in range(nh):
            sl = slice(hh * HEAD_DIM, (hh + 1) * HEAD_DIM)
            qn = _head_rms(q_ref[:, sl])[0] * gq_ref[...]
            kn = _head_rms(kv_ref[:, sl])[0] * gk_ref[...]
            vh = kv_ref[:, mw + hh * HEAD_DIM:mw + (hh + 1) * HEAD_DIM].astype(BF16)
            s = lax.dot_general(qn.astype(BF16), kn.astype(BF16), _NT, preferred_element_type=F32) * inv
            p = jnp.exp(s - jnp.max(s, axis=-1, keepdims=True))
            p = p / jnp.sum(p, axis=-1, keepdims=True)
            o_ref[:, sl] = jnp.dot(p.astype(BF16), vh, preferred_element_type=F32)

    vec = pl.BlockSpec((1, HEAD_DIM), lambda i: (0, 0))
    return pl.pallas_call(
        body, name=name, grid=(length // tq,),
        in_specs=[pl.BlockSpec((tq, mw), lambda i: (i, qcol0 // mw)),
                  pl.BlockSpec((mm, mw2), lambda i: (0, 0)), vec, vec],
        out_specs=pl.BlockSpec((tq, mw), lambda i: (i, 0)),
        out_shape=_sds((length, mw), F32),
        compiler_params=_params(("parallel",)),
    )(qsrc, kv, gq, gk)


def _mem_bwd(name, qsrc, qcol0, kv, gq, gk, dsrc, docol0):
    length = qsrc.shape[0]
    mm, mw2 = kv.shape
    mw = mw2 // 2
    nh = mw // HEAD_DIM
    tq = _tile(length, 512, 8)
    nsteps = length // tq
    inv = 1.0 / math.sqrt(HEAD_DIM)

    def body(q_ref, kv_ref, gq_ref, gk_ref, do_ref, dq_ref, dkv_ref, dgq_ref, dgk_ref):
        i = pl.program_id(0)

        @pl.when(i == 0)
        def _():
            dkv_ref[...] = jnp.zeros_like(dkv_ref)
            dgq_ref[...] = jnp.zeros_like(dgq_ref)

        gqv = gq_ref[...]
        gkv = gk_ref[...]
        for hh in range(nh):
            sl = slice(hh * HEAD_DIM, (hh + 1) * HEAD_DIM)
            slv = slice(mw + hh * HEAD_DIM, mw + (hh + 1) * HEAD_DIM)
            qhat, rq = _head_rms(q_ref[:, sl])
            qn = (qhat * gqv).astype(BF16)
            kn = (_head_rms(kv_ref[:, sl])[0] * gkv).astype(BF16)
            vh = kv_ref[:, slv].astype(BF16)
            dob = do_ref[:, sl].astype(BF16)
            s = lax.dot_general(qn, kn, _NT, preferred_element_type=F32) * inv
            p = jnp.exp(s - jnp.max(s, axis=-1, keepdims=True))
            p = p / jnp.sum(p, axis=-1, keepdims=True)
            dp = lax.dot_general(dob, vh, _NT, preferred_element_type=F32)
            ds = (p * (dp - jnp.sum(dp * p, axis=-1, keepdims=True)) * inv).astype(BF16)
            dqn = jnp.dot(ds, kn, preferred_element_type=F32)
            dkv_ref[:, sl] += lax.dot_general(ds, qn, _TN, preferred_element_type=F32)
            dkv_ref[:, slv] += lax.dot_general(p.astype(BF16), dob, _TN, preferred_element_type=F32)
            dgq_ref[...] += jnp.sum(dqn * qhat, axis=0, keepdims=True)
            dqh = dqn * gqv
            dq_ref[:, sl] = rq * (dqh - qhat * jnp.mean(dqh * qhat, axis=-1, keepdims=True))

        @pl.when(i == nsteps - 1)
        def _():
            dgk = jnp.zeros((1, HEAD_DIM), F32)
            for hh in range(nh):
                sl = slice(hh * HEAD_DIM, (hh + 1) * HEAD_DIM)
                khat, rk = _head_rms(kv_ref[:, sl])
                dkn = dkv_ref[:, sl]
                dgk = dgk + jnp.sum(dkn * khat, axis=0, keepdims=True)
                dkh = dkn * gkv
                dkv_ref[:, sl] = rk * (dkh - khat * jnp.mean(dkh * khat, axis=-1, keepdims=True))
            dgk_ref[...] = dgk

    vec = pl.BlockSpec((1, HEAD_DIM), lambda i: (0, 0))
    kvs = pl.BlockSpec((mm, mw2), lambda i: (0, 0))
    blk = pl.BlockSpec((tq, mw), lambda i: (i, 0))
    return pl.pallas_call(
        body, name=name, grid=(nsteps,),
        in_specs=[pl.BlockSpec((tq, mw), lambda i: (i, qcol0 // mw)), kvs, vec, vec,
                  pl.BlockSpec((tq, mw), lambda i: (i, docol0 // mw))],
        out_specs=(blk, kvs, vec, vec),
        out_shape=(_sds((length, mw), F32), _sds((mm, mw2), F32), _sds((1, HEAD_DIM), F32), _sds((1, HEAD_DIM), F32)),
        compiler_params=_params(("arbitrary",)),
    )(qsrc, kv, gq, gk, dsrc)


S5_TILE_GROUPS = LANES // S5_GROUP
S5_TILE_STATES = S5_TILE_GROUPS * S5_STATE
S5_SCAN_ROWS = 64


def _dotf(a, b, dims=_DIMS["nn"]):
    a_hi, a_lo = _split2(a)
    b_hi, b_lo = _split2(b)
    return (lax.dot_general(a_hi, b_hi, dims, preferred_element_type=F32)
            + lax.dot_general(a_hi, b_lo, dims, preferred_element_type=F32)
            + lax.dot_general(a_lo, b_hi, dims, preferred_element_type=F32))


def _s5_prep(log_dt, a_re, a_im, b_re, b_im, c_re, c_im):
    g, n = a_re.shape
    nt = g // S5_TILE_GROUPS
    dt = jnp.exp(log_dt)[:, None]
    mag = jnp.exp(a_re * dt)
    ab_re = mag * jnp.cos(a_im * dt)
    ab_im = mag * jnp.sin(a_im * dt)
    den = a_re * a_re + a_im * a_im
    num_re = ab_re - 1.0
    co_re = (num_re * a_re + ab_im * a_im) / den
    co_im = (ab_im * a_re - num_re * a_im) / den
    bb_re = co_re[..., None] * b_re - co_im[..., None] * b_im
    bb_im = co_re[..., None] * b_im + co_im[..., None] * b_re
    eye = jnp.eye(S5_TILE_GROUPS, dtype=F32)

    def blk_b(bb):
        t = bb.reshape(nt, S5_TILE_GROUPS, n, S5_GROUP).transpose(0, 1, 3, 2)
        return jnp.einsum("jgcn,gh->jgchn", t, eye).reshape(nt, LANES, S5_TILE_STATES)

    def blk_c(cc):
        t = cc.reshape(nt, S5_TILE_GROUPS, S5_GROUP, n).transpose(0, 1, 3, 2)
        return jnp.einsum("jgnc,gh->jgnhc", t, eye).reshape(nt, S5_TILE_STATES, LANES)

    return (ab_re.reshape(1, g * n), ab_im.reshape(1, g * n), blk_b(bb_re), blk_b(bb_im), blk_c(c_re), blk_c(c_im))


def _s5_bu(name, usrc, bblk_re, bblk_im):
    length = usrc.shape[0]
    nt = bblk_re.shape[0]
    tm = _tile(length, 512, 8)

    def body(u_ref, br_ref, bi_ref, or_ref, oi_ref):
        u = u_ref[...]
        or_ref[...] = _dotf(u, br_ref[...])
        oi_ref[...] = _dotf(u, bi_ref[...])

    bspec = pl.BlockSpec((None, LANES, S5_TILE_STATES), lambda i, j: (j, 0, 0))
    ospec = pl.BlockSpec((tm, S5_TILE_STATES), lambda i, j: (i, j))
    out = _sds((length, nt * S5_TILE_STATES), F32)
    return pl.pallas_call(
        body, name=name, grid=(length // tm, nt),
        in_specs=[pl.BlockSpec((tm, LANES), lambda i, j: (i, j)), bspec, bspec],
        out_specs=(ospec, ospec), out_shape=(out, out),
        compiler_params=_params(("parallel", "parallel")),
    )(usrc, bblk_re, bblk_im)


def _scan_rows(hr, hi, ar, ai, reverse):
    t = hr.shape[0]
    rows = lax.broadcasted_iota(jnp.int32, hr.shape, 0)
    d = 1
    while d < t:
        if reverse:
            sr = jnp.where(rows < t - d, pltpu.roll(hr, t - d, 0), 0.0)
            si = jnp.where(rows < t - d, pltpu.roll(hi, t - d, 0), 0.0)
        else:
            sr = jnp.where(rows >= d, pltpu.roll(hr, d, 0), 0.0)
            si = jnp.where(rows >= d, pltpu.roll(hi, d, 0), 0.0)
        hr, hi = hr + ar * sr - ai * si, hi + ar * si + ai * sr
        ar, ai = ar * ar - ai * ai, 2.0 * ar * ai
        d *= 2
    return hr, hi


def _s5_scan(name, x_re, x_im, a_re, a_im, reverse=False, h=None, bu=None):
    length, width = x_re.shape
    t = _tile(length, S5_SCAN_ROWS, 8)
    nsteps = length // t
    with_sum = h is not None

    def body(*refs):
        if with_sum:
            (xr_ref, xi_ref, ar_ref, ai_ref, hr_ref, hi_ref, br_ref, bi_ref,
             or_ref, oi_ref, sr_ref, si_ref, cr, ci, pr, pi, accr, acci) = refs
        else:
            xr_ref, xi_ref, ar_ref, ai_ref, or_ref, oi_ref, cr, ci, pr, pi = refs
        step = pl.program_id(0)
        ar = ar_ref[...]
        ai = ai_ref[...]
        edge = 0 if reverse else t - 1
        last = t - 1 if reverse else 0

        @pl.when(step == 0)
        def _():
            cr[...] = jnp.zeros_like(cr)
            ci[...] = jnp.zeros_like(ci)
            rows = lax.broadcasted_iota(jnp.int32, (t, width), 0)
            seed_r = jnp.where(rows == last, ar, 0.0)
            seed_i = jnp.where(rows == last, ai, 0.0)
            p_r, p_i = _scan_rows(seed_r, seed_i, ar, ai, reverse)
            pr[...] = p_r
            pi[...] = p_i
            if with_sum:
                accr[...] = jnp.zeros_like(accr)
                acci[...] = jnp.zeros_like(acci)

        hr, hi = _scan_rows(xr_ref[...], xi_ref[...], ar, ai, reverse)
        c_r = cr[...]
        c_i = ci[...]
        p_r = pr[...]
        p_i = pi[...]
        hr = hr + p_r * c_r - p_i * c_i
        hi = hi + p_r * c_i + p_i * c_r
        or_ref[...] = hr
        oi_ref[...] = hi
        cr[...] = hr[edge:edge + 1, :]
        ci[...] = hi[edge:edge + 1, :]
        if with_sum:
            wr = hr_ref[...] - br_ref[...]
            wi = hi_ref[...] - bi_ref[...]
            accr[...] += (wr * hr + wi * hi).reshape(t // 8, 8, width).sum(axis=0)
            acci[...] += (wr * hi - wi * hr).reshape(t // 8, 8, width).sum(axis=0)

            @pl.when(step == nsteps - 1)
            def _():
                sr_ref[...] = jnp.sum(accr[...], axis=0, keepdims=True)
                si_ref[...] = jnp.sum(acci[...], axis=0, keepdims=True)

    if reverse:
        blk = pl.BlockSpec((t, width), lambda s: (nsteps - 1 - s, 0))
    else:
        blk = pl.BlockSpec((t, width), lambda s: (s, 0))
    vec = pl.BlockSpec((1, width), lambda s: (0, 0))
    full = _sds((length, width), F32)
    row = _sds((1, width), F32)
    scratch = [pltpu.VMEM((1, width), F32), pltpu.VMEM((1, width), F32),
               pltpu.VMEM((t, width), F32), pltpu.VMEM((t, width), F32)]
    if with_sum:
        return pl.pallas_call(
            body, name=name, grid=(nsteps,),
            in_specs=[blk, blk, vec, vec, blk, blk, blk, blk],
            out_specs=(blk, blk, vec, vec), out_shape=(full, full, row, row),
            scratch_shapes=scratch + [pltpu.VMEM((8, width), F32), pltpu.VMEM((8, width), F32)],
            compiler_params=_params(("arbitrary",)),
        )(x_re, x_im, a_re, a_im, h[0], h[1], bu[0], bu[1])
    return pl.pallas_call(
        body, name=name, grid=(nsteps,),
        in_specs=[blk, blk, vec, vec], out_specs=(blk, blk), out_shape=(full, full),
        scratch_shapes=scratch,
        compiler_params=_params(("arbitrary",)),
    )(x_re, x_im, a_re, a_im)


_GELU_C = math.sqrt(2.0 / math.pi)


def _gelu(y):
    return 0.5 * y * (1.0 + jnp.tanh(_GELU_C * (y + 0.044715 * y * y * y)))


def _gelu_grad(y):
    th = jnp.tanh(_GELU_C * (y + 0.044715 * y * y * y))
    return 0.5 * (1.0 + th) + 0.5 * y * (1.0 - th * th) * _GELU_C * (1.0 + 3 * 0.044715 * y * y)


def _s5_out(name, h_re, h_im, cblk_re, cblk_im, usrc, dskip):
    length = h_re.shape[0]
    nt = cblk_re.shape[0]
    tm = _tile(length, 512, 8)

    def body(hr_ref, hi_ref, cr_ref, ci_ref, u_ref, d_ref, y_ref, y2_ref):
        y = _dotf(hr_ref[...], cr_ref[...]) - _dotf(hi_ref[...], ci_ref[...]) + d_ref[...] * u_ref[...]
        y_ref[...] = y
        y2_ref[...] = _gelu(y)

    hspec = pl.BlockSpec((tm, S5_TILE_STATES), lambda i, j: (i, j))
    cspec = pl.BlockSpec((None, S5_TILE_STATES, LANES), lambda i, j: (j, 0, 0))
    uspec = pl.BlockSpec((tm, LANES), lambda i, j: (i, j))
    out = _sds((length, nt * LANES), F32)
    return pl.pallas_call(
        body, name=name, grid=(length // tm, nt),
        in_specs=[hspec, hspec, cspec, cspec, uspec, pl.BlockSpec((1, LANES), lambda i, j: (0, j))],
        out_specs=(uspec, uspec), out_shape=(out, out),
        compiler_params=_params(("parallel", "parallel")),
    )(h_re, h_im, cblk_re, cblk_im, usrc, dskip)


def _s5_fwd(tag, usrc, prep, dskip, wglu):
    ab_re, ab_im, bb_re, bb_im, cb_re, cb_im = prep
    bu = _s5_bu(f"{tag}_bu", usrc, bb_re, bb_im)
    hs = _s5_scan(f"{tag}_scan", bu[0], bu[1], ab_re, ab_im)
    y, y2 = _s5_out(f"{tag}_out", hs[0], hs[1], cb_re, cb_im, usrc, dskip)

    def glu_epilogue(acc, ex, outs):
        outs[0][...] = acc
        outs[1][...] = ex[0][...] * jax.nn.sigmoid(acc)

    gl, tok = _mm_nn(f"{tag}_glu", y2, wglu, extras=(y2,), epilogue=glu_epilogue, n_out=2)
    return tok, (bu, hs, y, y2, gl)


def _s5_bwd(tag, usrc, prep, dskip, wglu, saved, dsrc):
    ab_re, ab_im, bb_re, bb_im, cb_re, cb_im = prep
    bu, hs, y, y2, gl = saved
    length = usrc.shape[0]
    tw = y.shape[1]
    nt = bb_re.shape[0]

    tme = _tile(length, 512, 8)

    def gate_body(dt_ref, y2_ref, gl_ref, dgl_ref, dy2_ref):
        s = jax.nn.sigmoid(gl_ref[...])
        dt = dt_ref[...]
        dgl_ref[...] = (dt * y2_ref[...] * s * (1.0 - s)).astype(BF16)
        dy2_ref[...] = dt * s

    espec = pl.BlockSpec((tme, tw), lambda i: (i, 0))
    dgl, dy2a = pl.pallas_call(
        gate_body, name=f"{tag}_dgate", grid=(length // tme,),
        in_specs=[espec, espec, espec], out_specs=(espec, espec),
        out_shape=(_sds((length, tw), BF16), _sds((length, tw), F32)),
        compiler_params=_params(("parallel",)),
    )(dsrc, y2, gl)

    dwglu = _mm_tn(f"{tag}_dwglu", y2, dgl)

    def dy_epilogue(acc, ex, outs):
        outs[0][...] = (ex[0][...] + acc) * _gelu_grad(ex[1][...])

    dy = _mm_nt(f"{tag}_dy", dgl, wglu, extras=(dy2a, y), epilogue=dy_epilogue)

    tmh = _tile(length, 512, 8)

    def dh_body(dy_ref, cr_ref, ci_ref, gr_ref, gi_ref):
        dyv = dy_ref[...]
        gr_ref[...] = _dotf(dyv, cr_ref[...], _NT)
        gi_ref[...] = -_dotf(dyv, ci_ref[...], _NT)

    hspec = pl.BlockSpec((tmh, S5_TILE_STATES), lambda i, j: (i, j))
    cspec = pl.BlockSpec((None, S5_TILE_STATES, LANES), lambda i, j: (j, 0, 0))
    uspec = pl.BlockSpec((tmh, LANES), lambda i, j: (i, j))
    wide = _sds((length, nt * S5_TILE_STATES), F32)
    g_re, g_im = pl.pallas_call(
        dh_body, name=f"{tag}_dh", grid=(length // tmh, nt),
        in_specs=[uspec, cspec, cspec], out_specs=(hspec, hspec), out_shape=(wide, wide),
        compiler_params=_params(("parallel", "parallel")),
    )(dy, cb_re, cb_im)

    lam_re, lam_im, s_re, s_im = _s5_scan(f"{tag}_rscan", g_re, g_im, ab_re, -ab_im, reverse=True, h=hs, bu=bu)
    den = ab_re * ab_re + ab_im * ab_im
    da_re = (ab_re * s_re - ab_im * s_im) / den
    da_im = (ab_re * s_im + ab_im * s_re) / den

    def du_body(lr_ref, li_ref, br_ref, bi_ref, dy_ref, d_ref, du_ref):
        du_ref[...] = (_dotf(lr_ref[...], br_ref[...], _NT) + _dotf(li_ref[...], bi_ref[...], _NT)
                       + dy_ref[...] * d_ref[...])

    bspec = pl.BlockSpec((None, LANES, S5_TILE_STATES), lambda i, j: (j, 0, 0))
    dvec = pl.BlockSpec((1, LANES), lambda i, j: (0, j))
    du = pl.pallas_call(
        du_body, name=f"{tag}_du", grid=(length // tmh, nt),
        in_specs=[hspec, hspec, bspec, bspec, uspec, dvec], out_specs=uspec,
        out_shape=_sds((length, tw), F32),
        compiler_params=_params(("parallel", "parallel")),
    )(lam_re, lam_im, bb_re, bb_im, dy, dskip)

    tkm = _tile(length, 512, 8)
    nk = length // tkm

    def dpar_body(u_ref, dy_ref, hr_ref, hi_ref, lr_ref, li_ref, dbr_ref, dbi_ref, dcr_ref, dci_ref, dd_ref):
        k = pl.program_id(1)

        @pl.when(k == 0)
        def _():
            for ref in (dbr_ref, dbi_ref, dcr_ref, dci_ref, dd_ref):
                ref[...] = jnp.zeros_like(ref)

        u = u_ref[...]
        dyv = dy_ref[...]
        dbr_ref[...] += _dotf(u, lr_ref[...], _TN)
        dbi_ref[...] += _dotf(u, li_ref[...], _TN)
        dcr_ref[...] += _dotf(hr_ref[...], dyv, _TN)
        dci_ref[...] -= _dotf(hi_ref[...], dyv, _TN)
        dd_ref[...] += jnp.sum(dyv * u, axis=0, keepdims=True)

    kspec_u = pl.BlockSpec((tkm, LANES), lambda j, k: (k, j))
    kspec_h = pl.BlockSpec((tkm, S5_TILE_STATES), lambda j, k: (k, j))
    ob = pl.BlockSpec((None, LANES, S5_TILE_STATES), lambda j, k: (j, 0, 0))
    oc = pl.BlockSpec((None, S5_TILE_STATES, LANES), lambda j, k: (j, 0, 0))
    dbr, dbi, dcr, dci, dd = pl.pallas_call(
        dpar_body, name=f"{tag}_dpar", grid=(nt, nk),
        in_specs=[kspec_u, kspec_u, kspec_h, kspec_h, kspec_h, kspec_h],
        out_specs=(ob, ob, oc, oc, pl.BlockSpec((1, LANES), lambda j, k: (0, j))),
        out_shape=(_sds(bb_re.shape, F32), _sds(bb_re.shape, F32), _sds(cb_re.shape, F32), _sds(cb_re.shape, F32),
                   _sds((1, tw), F32)),
        compiler_params=_params(("parallel", "arbitrary")),
    )(usrc, dy, hs[0], hs[1], lam_re, lam_im)
    return du, (da_re, da_im, dbr, dbi, dcr, dci), dd, dwglu


def _loss_head(name, y, target):
    m, d = y.shape
    tm = _tile(m, 256, 8)
    nsteps = m // tm

    def body(y_ref, t_ref, loss_ref, dy_ref, dyb_ref, acc_ref):
        i = pl.program_id(0)
        diff = y_ref[...] - t_ref[...]
        dy = diff * (1.0 / d)
        dy_ref[...] = dy
        dyb_ref[...] = dy.astype(BF16)
        sq = (diff * diff).reshape(tm // 8, 8, d).sum(axis=0)

        @pl.when(i == 0)
        def _():
            acc_ref[...] = sq

        @pl.when(i > 0)
        def _():
            acc_ref[...] += sq

        @pl.when(i == nsteps - 1)
        def _():
            loss_ref[...] = jnp.full(loss_ref.shape, jnp.sum(acc_ref[...]) * (0.5 / d), F32)

    row = pl.BlockSpec((tm, d), lambda i: (i, 0))
    return pl.pallas_call(
        body, name=name, grid=(nsteps,),
        in_specs=[row, row], out_specs=(pl.BlockSpec((8, LANES), lambda i: (0, 0)), row, row),
        out_shape=(_sds((8, LANES), F32), _sds((m, d), F32), _sds((m, d), BF16)),
        scratch_shapes=[pltpu.VMEM((8, d), F32)],
        compiler_params=_params(("arbitrary",)),
    )(y, target)


def _adamw(name, w, g, m, v):
    shape = w.shape
    total = math.prod(shape)
    if shape[-1] % LANES and total % LANES == 0:
        view = (total // LANES, LANES)
    else:
        view = (total // shape[-1], shape[-1])
    rows, cols = view
    tr = rows
    if rows * cols * 4 > (1 << 20) and rows % 16 == 0:
        tr = _row_tile(rows, cols)
    c1 = 1.0 / (1.0 - ADAM_B1 ** ADAM_STEP)
    c2 = 1.0 / (1.0 - ADAM_B2 ** ADAM_STEP)

    def body(w_ref, g_ref, m_ref, v_ref, d_ref, nm_ref, nv_ref):
        gv = g_ref[...]
        nm = ADAM_B1 * m_ref[...] + (1.0 - ADAM_B1) * gv
        nv = ADAM_B2 * v_ref[...] + (1.0 - ADAM_B2) * gv * gv
        nm_ref[...] = nm
        nv_ref[...] = nv
        d_ref[...] = -ADAM_LR * ((nm * c1) / (jnp.sqrt(nv * c2) + ADAM_EPS) + ADAM_WD * w_ref[...])

    spec = pl.BlockSpec((tr, cols), lambda i: (i, 0))
    out = _sds(view, F32)
    res = pl.pallas_call(
        body, name=name, grid=(rows // tr,),
        in_specs=[spec] * 4, out_specs=(spec,) * 3, out_shape=(out,) * 3,
        compiler_params=_params(("parallel",)),
    )(w.reshape(view), g.reshape(view), m.reshape(view), v.reshape(view))
    return tuple(r.reshape(shape) for r in res)


def _sum_leading(name, x):
    k, rows, cols = x.shape
    tr = _row_tile(rows, cols)

    def body(x_ref, o_ref):
        acc = x_ref[0].astype(F32)
        for j in range(1, k):
            acc = acc + x_ref[j].astype(F32)
        o_ref[...] = acc

    return pl.pallas_call(
        body, name=name, grid=(rows // tr,),
        in_specs=[pl.BlockSpec((k, tr, cols), lambda i: (0, i, 0))],
        out_specs=pl.BlockSpec((tr, cols), lambda i: (i, 0)), out_shape=_sds((rows, cols), F32),
        compiler_params=_params(("parallel",)),
    )(x)


def _sum_chips_into_half(name, slabs):
    s, hf, r, c = slabs.shape
    rows = hf * r
    tr = _row_tile(rows, c)
    core = lax.axis_index("c").astype(jnp.int32).reshape(1)

    def body(core_ref, x_ref, o_ref):
        acc = x_ref[0].astype(F32)
        for j in range(1, s):
            acc = acc + x_ref[j].astype(F32)
        o_ref[...] = acc

    out = pl.pallas_call(
        body, name=name,
        grid_spec=pltpu.PrefetchScalarGridSpec(
            num_scalar_prefetch=1, grid=(rows // tr,),
            in_specs=[pl.BlockSpec((s, tr, c), lambda i, core_ref: (0, i, 0))],
            out_specs=pl.BlockSpec((None, tr, c), lambda i, core_ref: (core_ref[0], i, 0))),
        out_shape=_sds((2, rows, c), F32), compiler_params=_params(("parallel",)),
    )(core, slabs.reshape(s, rows, c))
    return out.reshape(2 * hf, r, c)


def _add_own_layer(name, g0, g1, recv):
    s, r, c = g0.shape
    rows = s * r
    tr = _row_tile(rows, c)
    core = lax.axis_index("c").astype(jnp.int32).reshape(1)

    def body(core_ref, g0_ref, g1_ref, r_ref, o_ref):
        own = jnp.where(core_ref[0] == 0, g0_ref[...], g1_ref[...])
        o_ref[...] = (own + r_ref[...]).astype(BF16)

    flat = pl.BlockSpec((tr, c), lambda i, core_ref: (i, 0))
    out = pl.pallas_call(
        body, name=name,
        grid_spec=pltpu.PrefetchScalarGridSpec(
            num_scalar_prefetch=1, grid=(rows // tr,),
            in_specs=[pl.BlockSpec((tr, c), lambda i, core_ref: (i * (1 - core_ref[0]), 0)),
                      pl.BlockSpec((tr, c), lambda i, core_ref: (i * core_ref[0], 0)), flat],
            out_specs=flat),
        out_shape=_sds((rows, c), BF16), compiler_params=_params(("parallel",)),
    )(core, g0.reshape(rows, c), g1.reshape(rows, c), recv.reshape(rows, c))
    return out.reshape(1, s, r, c)


_ANY = pl.BlockSpec(memory_space=pl.ANY)


def _place():
    x, y, c = lax.axis_index("x"), lax.axis_index("y"), lax.axis_index("c")
    return x, y, c, [(1 - x, y), (x, 1 - y), (1 - x, 1 - y)]


def _remote(src, dst, send_sems, recv_sems, k, to):
    return pltpu.make_async_remote_copy(src_ref=src, dst_ref=dst, send_sem=send_sems.at[k], recv_sem=recv_sems.at[k],
                                        device_id=to, device_id_type=MESH)


def _comm_call(name, body, ins, out_shapes, n_remote, n_local, aliases=None):
    scratch = [pltpu.SemaphoreType.DMA((n_remote,)), pltpu.SemaphoreType.DMA((n_remote,))]
    if n_local:
        scratch.append(pltpu.SemaphoreType.DMA((n_local,)))
    return pl.pallas_call(
        body, name=name, in_specs=[_ANY] * len(ins), out_specs=[_ANY] * len(out_shapes), out_shape=out_shapes,
        scratch_shapes=scratch, input_output_aliases=aliases or {},
    )(*ins)


class _Comm(NamedTuple):
    start: object
    finish: object
    ins: tuple
    out_shapes: tuple
    n_remote: int
    n_local: int

    def scratch(self):
        sems = [pltpu.SemaphoreType.DMA((self.n_remote,)), pltpu.SemaphoreType.DMA((self.n_remote,))]
        return sems + ([pltpu.SemaphoreType.DMA((self.n_local,))] if self.n_local else [])


def _run_comm(name, comm):
    def body(*refs):
        comm.start(refs)
        comm.finish(refs)

    return pl.pallas_call(
        body, name=name, in_specs=[_ANY] * len(comm.ins), out_specs=[_ANY] * len(comm.out_shapes),
        out_shape=list(comm.out_shapes), scratch_shapes=comm.scratch(),
    )(*comm.ins)


def _gather_comm(shards, owned):
    nt = len(shards)

    def copies(refs):
        ins, outs, send_sems, recv_sems = refs[:nt], refs[nt:2 * nt], refs[2 * nt], refs[2 * nt + 1]
        x, y, c, chips = _place()
        me = 2 * x + y
        sibling = (x, y, 1 - c)
        own, first, arrive, passed, forwarded = [], [], [], [], []
        for t, (l0, l1) in enumerate(owned):
            src_mine = ins[t].at[pl.ds(l0 + c * (l1 - l0), 1)]
            src_theirs = ins[t].at[pl.ds(l1 - c * (l1 - l0), 1)]
            mine, theirs = pl.ds(c, 1), pl.ds(1 - c, 1)
            for k, layer in enumerate((l0, l1)):
                own.append(_remote(ins[t].at[pl.ds(layer, 1)], outs[t].at[me, pl.ds(k, 1)], send_sems, recv_sems,
                                   8 * t + 6 + k, sibling))
            for j, chip in enumerate(chips):
                got, fwd = outs[t].at[2 * chip[0] + chip[1], mine], outs[t].at[2 * chip[0] + chip[1], theirs]
                first.append(_remote(src_mine, outs[t].at[me, mine], send_sems, recv_sems, 8 * t + j, (*chip, c)))
                arrive.append(_remote(src_mine, got, send_sems, recv_sems, 8 * t + j, (x, y, c)))
                passed.append(_remote(got, got, send_sems, recv_sems, 8 * t + 3 + j, sibling))
                forwarded.append(_remote(src_theirs, fwd, send_sems, recv_sems, 8 * t + 3 + j, (x, y, c)))
        return own, first, arrive, passed, forwarded

    def start(refs):
        own, first, _, _, _ = copies(refs)
        for cp in own + first:
            cp.start()

    def finish(refs):
        own, first, arrive, passed, forwarded = copies(refs)
        for landed, onward in zip(arrive, passed):
            landed.wait_recv()
            onward.start()
        for cp in own + forwarded:
            cp.wait_recv()
        for cp in own + first + passed:
            cp.wait_send()

    shapes = tuple(_sds((N_CHIPS, 2) + s.shape[1:], s.dtype) for s in shards)
    return _Comm(start, finish, tuple(shards), shapes, 8 * nt, 0)


def _other_layers_to_sibling(name, pairs):
    nt = len(pairs)

    def body(*refs):
        ins, outs = refs[:2 * nt], refs[2 * nt:3 * nt]
        send_sems, recv_sems = refs[3 * nt:]
        x, y, c, _ = _place()
        for t in range(nt):
            for core in (0, 1):
                @pl.when(c == core)
                def _(t=t, core=core):
                    _remote(ins[2 * t + 1 - core], outs[t], send_sems, recv_sems, t, (x, y, 1 - c)).start()
        for t in range(nt):
            _remote(ins[2 * t], outs[t], send_sems, recv_sems, t, (x, y, 1 - c)).wait()

    flat = [g for pair in pairs for g in pair]
    return _comm_call(name, body, flat, [_sds(g0.shape, g0.dtype) for g0, _ in pairs], nt, 0)


def _scatter_comm(parts):
    nt = len(parts)

    def copies(refs):
        ins, outs = refs[:nt], refs[nt:2 * nt]
        send_sems, recv_sems, local_sems = refs[2 * nt:2 * nt + 3]
        x, y, c, chips = _place()
        me = 2 * x + y
        local, sends, arrive = [], [], []
        for t in range(nt):
            layers = pl.ds(0, ins[t].shape[0])
            local.append(pltpu.make_async_copy(ins[t].at[layers, me], outs[t].at[me], local_sems.at[t]))
            for j, chip in enumerate(chips):
                there = 2 * chip[0] + chip[1]
                sends.append(_remote(ins[t].at[layers, there], outs[t].at[me], send_sems, recv_sems, 3 * t + j, (*chip, c)))
                arrive.append(_remote(ins[t].at[layers, me], outs[t].at[there], send_sems, recv_sems, 3 * t + j, (x, y, c)))
        return local, sends, arrive

    def start(refs):
        local, sends, _ = copies(refs)
        for cp in local + sends:
            cp.start()

    def finish(refs):
        local, sends, arrive = copies(refs)
        for cp in arrive:
            cp.wait_recv()
        for cp in sends:
            cp.wait_send()
        for cp in local:
            cp.wait()

    shapes = tuple(_sds((p.shape[1], p.shape[0]) + p.shape[2:], p.dtype) for p in parts)
    return _Comm(start, finish, tuple(parts), shapes, 3 * nt, nt)


def _join_halves(name, bufs):
    nt = len(bufs)

    def body(*refs):
        outs = refs[nt:2 * nt]
        send_sems, recv_sems = refs[2 * nt:]
        x, y, c, _ = _place()
        sends = []
        for t in range(nt):
            half = outs[t].shape[0] // 2
            mine = pl.ds(c * half, half)
            sends.append(_remote(outs[t].at[mine], outs[t].at[mine], send_sems, recv_sems, t, (x, y, 1 - c)))
            sends[-1].start()
        for t in range(nt):
            half = outs[t].shape[0] // 2
            theirs = pl.ds((1 - c) * half, half)
            _remote(outs[t].at[theirs], outs[t].at[theirs], send_sems, recv_sems, t, (x, y, c)).wait_recv()
        for cp in sends:
            cp.wait_send()

    return _comm_call(name, body, bufs, [_sds(b.shape, b.dtype) for b in bufs], nt, 0,
                      aliases={t: t for t in range(nt)})


def _gather_all(name, block):
    rows, cols = block.shape

    def body(x_ref, out_ref, send_sems, recv_sems, local_sem):
        x, y, c, chips = _place()
        me, sibling = (x, y, c), (x, y, 1 - c)

        def at(px, py, pc):
            return out_ref.at[4 * px + 2 * py + pc]

        def copy(k, blk, to, src=None):
            return _remote(at(*blk) if src is None else src, at(*blk), send_sems, recv_sems, k, to)

        mine = pltpu.make_async_copy(x_ref, at(*me), local_sem)
        mine.start()
        first = [copy(0, me, sibling, src=x_ref)]
        first += [copy(1 + j, me, (*chip, c), src=x_ref) for j, chip in enumerate(chips)]
        for cp in first:
            cp.start()
        passed = [copy(4 + j, (*chip, c), sibling) for j, chip in enumerate(chips)]
        for j, chip in enumerate(chips):
            copy(1 + j, (*chip, c), me).wait_recv()
            passed[j].start()
        copy(0, sibling, me).wait_recv()
        for j, chip in enumerate(chips):
            copy(4 + j, (*chip, 1 - c), me).wait_recv()
        for cp in first + passed:
            cp.wait_send()
        mine.wait()

    return pl.pallas_call(
        body, name=name,
        in_specs=[pl.BlockSpec(memory_space=pltpu.VMEM)], out_specs=pl.BlockSpec(memory_space=pltpu.VMEM),
        out_shape=_sds((8, rows, cols), block.dtype),
        scratch_shapes=[pltpu.SemaphoreType.DMA((7,)), pltpu.SemaphoreType.DMA((7,)), pltpu.SemaphoreType.DMA],
        compiler_params=pltpu.CompilerParams(vmem_limit_bytes=V7X_VMEM_LIMIT),
    )(block)


WEIGHTS = ("ffn1_norm", "ffn1_w_gu", "ffn1_w_down", "mix_norm", "mem_norm", "w_mem_kv", "xq_norm", "xk_norm", "w_out",
           "ffn2_norm", "ffn2_w_gu", "ffn2_w_down", "sb_w_in", "s5_w_in", "s5_log_dt", "s5_a_re", "s5_a_im", "s5_b_re",
           "s5_b_im", "s5_c_re", "s5_c_im", "s5_d", "s5_w_glu")
BIG = ("ffn1_w_gu", "ffn1_w_down", "w_mem_kv", "w_out", "ffn2_w_gu", "ffn2_w_down", "sb_w_in", "s5_w_in", "s5_w_glu")
COLUMN_SHARDED = ("ffn1_w_gu", "ffn2_w_gu", "sb_w_in")
EVERY_LAYER = ("ffn1_w_gu", "ffn1_w_down", "w_mem_kv", "w_out", "ffn2_w_gu", "ffn2_w_down")
SMALL = tuple(n for n in WEIGHTS if n not in BIG)
GROUPS = (EVERY_LAYER + ("sb_w_in",), EVERY_LAYER + ("s5_w_in", "s5_w_glu"))


def _owned(name, group):
    return (group, group + 2) if name in EVERY_LAYER else (0, 1)


def _layer_weights(full, i):
    return {name: _W(full[name], i // 2, name in COLUMN_SHARDED) for name in GROUPS[i % 2]}


def _row(v):
    return v.reshape(1, -1)


def _layer_fwd(i, x, mem, w, p, s5prep, comm=None):
    tag = f"l{i}"
    j = i // 2
    comm_out = ()
    x1, sv_ffn1 = _ffn_fwd(f"{tag}_ffn1", x, _row(p["ffn1_norm"][i]), w["ffn1_w_gu"], w["ffn1_w_down"])
    h = _rmsnorm_fwd(f"{tag}_mixnorm", x1, _row(p["mix_norm"][i]))
    if i % 2 == 0:
        proj = _mm_nn(f"{tag}_inproj", h, w["sb_w_in"])
        n_heads = (proj.shape[1] * 3 // 10) // HEAD_DIM
        (tok, mix_saved), comm_out = _sb_fwd(f"{tag}_sb", proj, n_heads, comm)
        tok_w = n_heads * HEAD_DIM
        qcol0 = 3 * tok_w
    else:
        proj = _mm_nn(f"{tag}_inproj", h, w["s5_w_in"])
        tok, mix_saved = _s5_fwd(f"{tag}_s5", proj, s5prep[j], _row(p["s5_d"][j]), w["s5_w_glu"])
        tok_w = tok.shape[1]
        qcol0 = tok_w
    mem_h = _rmsnorm_fwd(f"{tag}_memnorm", mem, _row(p["mem_norm"][i]))
    kv = _mm_nn(f"{tag}_memkv", mem_h, w["w_mem_kv"])
    gq, gk = _row(p["xq_norm"][i]), _row(p["xk_norm"][i])
    cross = _mem_fwd(f"{tag}_mem", proj, qcol0, kv, gq, gk)
    cat = jnp.concatenate([tok, cross], axis=1).astype(BF16)
    x2 = _mm_nn(f"{tag}_outproj", cat, w["w_out"], extras=(x1,), epilogue=_add_residual)
    x3, sv_ffn2 = _ffn_fwd(f"{tag}_ffn2", x2, _row(p["ffn2_norm"][i]), w["ffn2_w_gu"], w["ffn2_w_down"])
    saved = dict(ffn1=sv_ffn1, x1=x1, h=h, proj=proj, mix=mix_saved, mem_h=mem_h, kv=kv, cat=cat, ffn2=sv_ffn2,
                 tok_w=tok_w, qcol0=qcol0)
    return x3, saved, comm_out


def _layer_bwd(i, sv, mem, w, p, s5prep, dx3, dx3b, comm=None):
    tag = f"l{i}b"
    j = i // 2
    g = {}
    comm_out = ()
    dx2, dx2b, g["ffn2_norm"], g["ffn2_w_gu"], g["ffn2_w_down"] = _ffn_bwd(
        f"{tag}_ffn2", sv["ffn2"], _row(p["ffn2_norm"][i]), w["ffn2_w_gu"], w["ffn2_w_down"], dx3, dx3b)
    dcat = _mm_nt(f"{tag}_dcat", dx2b, w["w_out"])
    g["w_out"] = _mm_tn(f"{tag}_dwout", sv["cat"], dx2b)
    gq, gk = _row(p["xq_norm"][i]), _row(p["xk_norm"][i])
    tok_w, qcol0 = sv["tok_w"], sv["qcol0"]
    dqm, dkv, g["xq_norm"], g["xk_norm"] = _mem_bwd(f"{tag}_mem", sv["proj"], qcol0, sv["kv"], gq, gk, dcat, tok_w)
    dkvb = dkv.astype(BF16)
    g["w_mem_kv"] = _mm_tn(f"{tag}_dwkv", sv["mem_h"], dkvb)
    dmem_h = _mm_nt(f"{tag}_dmemh", dkvb, w["w_mem_kv"])
    g["mem_norm"] = _rmsnorm_bwd(f"{tag}_dmemnorm", mem, dmem_h, _row(p["mem_norm"][i]))
    if i % 2 == 0:
        (dq, dk, dv), comm_out = _sb_bwd(f"{tag}_sb", sv["proj"], sv["mix"], dcat, tok_w // HEAD_DIM, comm)
        dproj = jnp.concatenate([dq, dk, dv, dqm], axis=1).astype(BF16)
        g["sb_w_in"] = _mm_tn(f"{tag}_dwin", sv["h"], dproj, shards=N_CHIPS)
        dh = _mm_nt(f"{tag}_dh", dproj, w["sb_w_in"])
    else:
        du, g["s5_prep"], g["s5_d"], g["s5_w_glu"] = _s5_bwd(
            f"{tag}_s5", sv["proj"], s5prep[j], _row(p["s5_d"][j]), w["s5_w_glu"], sv["mix"], dcat)
        dproj = jnp.concatenate([du, dqm], axis=1).astype(BF16)
        g["s5_w_in"] = _mm_tn(f"{tag}_dwin", sv["h"], dproj)
        dh = _mm_nt(f"{tag}_dh", dproj, w["s5_w_in"])
    dx1, dx1b, g["mix_norm"] = _rmsnorm_bwd(f"{tag}_dmixnorm", sv["x1"], dh, _row(p["mix_norm"][i]), dx2)
    dx, dxb, g["ffn1_norm"], g["ffn1_w_gu"], g["ffn1_w_down"] = _ffn_bwd(
        f"{tag}_ffn1", sv["ffn1"], _row(p["ffn1_norm"][i]), w["ffn1_w_gu"], w["ffn1_w_down"], dx1, dx1b)
    return dx, dxb, g, comm_out


def _group_grads(per_layer, group):
    out = []
    for name in GROUPS[group]:
        per = [per_layer[i][name] for i in (group, group + 2)]
        if name not in COLUMN_SHARDED:
            per = [gl.reshape((N_CHIPS, gl.shape[0] // N_CHIPS, gl.shape[1])) for gl in per]
        out.append(tuple(per))
    return out


def _reduce_group(group, gs, beside=None):
    names = GROUPS[group]
    tag = f"reduce{group}"
    from_sibling = _other_layers_to_sibling(f"{tag}_to_sibling", gs)
    parts = [_add_own_layer(f"{tag}_add_cores_{n}", g0, g1, r) for n, (g0, g1), r in zip(names, gs, from_sibling)]
    comm = _scatter_comm(parts)
    slabs = beside(comm) if beside else _run_comm(f"{tag}_scatter_chips", comm)
    bufs = [_sum_chips_into_half(f"{tag}_add_chips_{n}", sl) for n, sl in zip(names, slabs)]
    return dict(zip(names, _join_halves(f"{tag}_join_halves", bufs)))


def _local_step(x, mem, target, shards, p):
    depth = p["ffn1_norm"].shape[0]
    n_s5 = depth // 2
    s5_names = ("s5_log_dt", "s5_a_re", "s5_a_im", "s5_b_re", "s5_b_im", "s5_c_re", "s5_c_im")
    s5prep, s5vjp = [], []
    for j in range(n_s5):
        out, vjp = jax.vjp(_s5_prep, *[p[n][j] for n in s5_names])
        s5prep.append(out)
        s5vjp.append(vjp)

    def gather(group):
        return _gather_comm([shards[n] for n in GROUPS[group]], [_owned(n, group) for n in GROUPS[group]])

    full = [dict(zip(GROUPS[0], _run_comm("gather0_weights", gather(0)))), None]
    saved = []
    for i in range(depth):
        x, sv, got = _layer_fwd(i, x, mem, _layer_weights(full[i % 2], i), p, s5prep, gather(1) if i == 0 else None)
        if i == 0:
            full[1] = dict(zip(GROUPS[1], got))
        saved.append(sv)
    loss, dx, dxb = _loss_head("loss", x, target)
    per_layer = [None] * depth
    reduced = [None, None]
    for i in reversed(range(depth)):
        if i == 0:
            def beside(comm):
                nonlocal dx, dxb
                dx, dxb, per_layer[0], slabs = _layer_bwd(0, saved[0], mem, _layer_weights(full[0], 0), p, s5prep,
                                                          dx, dxb, comm)
                return slabs

            reduced[1] = _reduce_group(1, _group_grads(per_layer, 1), beside)
        else:
            dx, dxb, per_layer[i], _ = _layer_bwd(i, saved[i], mem, _layer_weights(full[i % 2], i), p, s5prep, dx, dxb)
    reduced[0] = _reduce_group(0, _group_grads(per_layer, 0))
    grads = {}
    for name in ("ffn1_norm", "mix_norm", "mem_norm", "xq_norm", "xk_norm", "ffn2_norm"):
        grads[name] = jnp.concatenate([per_layer[i][name] for i in range(depth)], axis=0)
    grads["s5_d"] = jnp.concatenate([per_layer[i]["s5_d"] for i in range(1, depth, 2)], axis=0)
    s5g = [s5vjp[j](tuple(per_layer[2 * j + 1]["s5_prep"])) for j in range(n_s5)]
    for k, name in enumerate(s5_names):
        grads[name] = jnp.stack([s5g[j][k] for j in range(n_s5)], axis=0)
    for name in BIG:
        if name in EVERY_LAYER:
            grads[name] = jnp.stack([reduced[i % 2][name][i // 2] for i in range(depth)], axis=0)
        else:
            grads[name] = reduced[0 if name in GROUPS[0] else 1][name]
    return loss, dx, grads


def _reduce_small(grads, shapes):
    flat = jnp.concatenate([grads[n].reshape(-1) for n in SMALL])
    total = flat.shape[0]
    rows = -(-total // (512 * LANES)) * 512
    block = jnp.pad(flat, (0, rows * LANES - total)).reshape(rows, LANES)
    summed = _sum_leading("reduce_small_sum", _gather_all("reduce_small_gather", block)).reshape(-1)
    out, off = {}, 0
    for n in SMALL:
        size = math.prod(shapes[n])
        out[n] = summed[off:off + size].reshape(shapes[n])
        off += size
    return out


def kernel(x, mem, ffn1_norm, ffn1_w_gu, ffn1_w_down, mix_norm, mem_norm, w_mem_kv, xq_norm, xk_norm, w_out, ffn2_norm, ffn2_w_gu, ffn2_w_down, sb_w_in, s5_w_in, s5_log_dt, s5_a_re, s5_a_im, s5_b_re, s5_b_im, s5_c_re, s5_c_im, s5_d, s5_w_glu, loss_target, m_ffn1_norm, m_ffn1_w_gu, m_ffn1_w_down, m_mix_norm, m_mem_norm, m_w_mem_kv, m_xq_norm, m_xk_norm, m_w_out, m_ffn2_norm, m_ffn2_w_gu, m_ffn2_w_down, m_sb_w_in, m_s5_w_in, m_s5_log_dt, m_s5_a_re, m_s5_a_im, m_s5_b_re, m_s5_b_im, m_s5_c_re, m_s5_c_im, m_s5_d, m_s5_w_glu, v_ffn1_norm, v_ffn1_w_gu, v_ffn1_w_down, v_mix_norm, v_mem_norm, v_w_mem_kv, v_xq_norm, v_xk_norm, v_w_out, v_ffn2_norm, v_ffn2_w_gu, v_ffn2_w_down, v_sb_w_in, v_s5_w_in, v_s5_log_dt, v_s5_a_re, v_s5_a_im, v_s5_b_re, v_s5_b_im, v_s5_c_re, v_s5_c_im, v_s5_d, v_s5_w_glu):
    given = dict(locals())
    wts = {n: given[n] for n in WEIGHTS}
    chip = 2 * lax.axis_index("x") + lax.axis_index("y")

    d_sh = s5_d.shape
    d_rows = -(-math.prod(d_sh) // (8 * LANES)) * 8
    d_block = jnp.pad(s5_d.reshape(-1), (0, d_rows * LANES - math.prod(d_sh))).reshape(d_rows, LANES)
    d_all = _gather_all("gather_s5_d", d_block).reshape(8, -1)[:, :math.prod(d_sh)]
    d_full = jnp.concatenate([d_all[2 * s].reshape(d_sh) for s in range(N_CHIPS)], axis=1)

    p = {n: wts[n] for n in SMALL}
    p["s5_d"] = d_full
    loss, grad_x, grads = _local_step(x[0], mem[0], loss_target[0], {n: wts[n].astype(BF16) for n in BIG}, p)

    small_shapes = {n: wts[n].shape for n in SMALL}
    small_shapes["s5_d"] = d_full.shape
    gsum = {n: grads[n] for n in BIG}
    gsmall = _reduce_small(grads, small_shapes)
    gsmall["s5_d"] = lax.dynamic_slice_in_dim(gsmall["s5_d"], chip * d_sh[1], d_sh[1], axis=1)
    gsum.update(gsmall)

    deltas, new_m, new_v = {}, {}, {}
    for n in WEIGHTS:
        deltas[n], new_m[n], new_v[n] = _adamw(f"adamw_{n}", wts[n], gsum[n], given["m_" + n], given["v_" + n])
    total_loss = lax.psum(loss[0, 0], ("x", "y", "c"))
    return (total_loss, grad_x[None], *[gsum[n] for n in WEIGHTS], *[deltas[n] for n in WEIGHTS],
            *[new_m[n] for n in WEIGHTS], *[new_v[n] for n in WEIGHTS])
```

```python
import math
from typing import NamedTuple

import jax
import jax.numpy as jnp
from jax import lax
from jax.experimental import pallas as pl
from jax.experimental.pallas import tpu as pltpu

F32 = jnp.float32
BF16 = jnp.bfloat16

HEAD_DIM = 128
S5_GROUP = 16
S5_STATE = 64
EPS = 1e-6
ADAM_LR = 0.001
ADAM_B1 = 0.9
ADAM_B2 = 0.999
ADAM_EPS = 1e-08
ADAM_WD = 0.01
ADAM_STEP = 10

LANES = 128
V7X_VMEM_LIMIT = 56 * 1024 * 1024
N_CHIPS = 4
MESH = pl.DeviceIdType.MESH
_NT = (((1,), (1,)), ((), ()))
_TN = (((0,), (0,)), ((), ()))


def _params(sem):
    return pltpu.CompilerParams(dimension_semantics=sem, vmem_limit_bytes=V7X_VMEM_LIMIT)


def _tile(n, want, mult=LANES):
    if n <= want:
        return n
    t = (want // mult) * mult
    while t > mult and n % t:
        t -= mult
    assert n % t == 0, (n, want, mult)
    return t


def _row_tile(rows, cols, itemsize=4, target=1 << 20):
    return _tile(rows, max(16, (target // (cols * itemsize)) // 16 * 16), 16)


def _sds(shape, dtype):
    return jax.ShapeDtypeStruct(tuple(shape), dtype)


class _W(NamedTuple):
    arr: jax.Array
    layer: int
    cols: bool


def _wspec(w, tr, tc, rb, cb):
    _, _, r, c = w.arr.shape
    layer = w.layer
    assert r % tr == 0 and c % tc == 0, (r, c, tr, tc)
    if w.cols:
        nb = c // tc
        return pl.BlockSpec((None, None, tr, tc), lambda *g: (cb(*g) // nb, layer, rb(*g), cb(*g) % nb))
    nb = r // tr
    return pl.BlockSpec((None, None, tr, tc), lambda *g: (rb(*g) // nb, layer, rb(*g) % nb, cb(*g)))


_DIMS = {"nn": (((1,), (0,)), ((), ())), "nt": _NT, "tn": _TN}


def _matmul(name, mode, a, b, *, grid, a_spec, b_spec, out_shape, out_spec, acc_shape,
            extras=(), extra_specs=(), epilogue=None):
    nk = grid[2]
    ne = len(extras)
    multi = isinstance(out_shape, (tuple, list))
    outs = tuple(out_shape) if multi else (out_shape,)
    ospecs = tuple(out_spec) if multi else (out_spec,)
    no = len(outs)

    def body(a_ref, b_ref, *rest):
        ex = rest[:ne]
        out_refs = rest[ne:ne + no]
        part = lax.dot_general(a_ref[...].astype(BF16), b_ref[...].astype(BF16), _DIMS[mode],
                               preferred_element_type=F32)

        def finish(acc):
            if epilogue is None:
                out_refs[0][...] = acc.astype(out_refs[0].dtype)
            else:
                epilogue(acc, ex, out_refs)

        if nk == 1:
            finish(part)
        else:
            acc_ref = rest[-1]
            k = pl.program_id(2)

            @pl.when(k == 0)
            def _():
                acc_ref[...] = part

            @pl.when(k > 0)
            def _():
                acc_ref[...] += part

            @pl.when(k == nk - 1)
            def _():
                finish(acc_ref[...])

    return pl.pallas_call(
        body, name=name, grid=grid,
        in_specs=[a_spec, b_spec, *extra_specs],
        out_specs=ospecs if multi else ospecs[0],
        out_shape=outs if multi else outs[0],
        scratch_shapes=[pltpu.VMEM(acc_shape, F32)] if nk > 1 else [],
        compiler_params=_params(("parallel", "parallel", "arbitrary")),
    )(a, b, *extras)


def _mm_nn(name, a, w, out_dtype=F32, extras=(), epilogue=None, n_out=1):
    m, kk = a.shape
    _, _, r, c = w.arr.shape
    tm = _tile(m, 1024, 8)
    tk = _tile(r, 2048)
    if w.cols:
        n, tn = N_CHIPS * c, _tile(c, 256)
    else:
        n, tn = c, _tile(c, 512)
    ospec = pl.BlockSpec((tm, tn), lambda i, j, k: (i, j))
    out_shape = _sds((m, n), out_dtype)
    return _matmul(name, "nn", a, w.arr, grid=(m // tm, n // tn, kk // tk),
                   a_spec=pl.BlockSpec((tm, tk), lambda i, j, k: (i, k)),
                   b_spec=_wspec(w, tk, tn, lambda i, j, k: k, lambda i, j, k: j),
                   out_shape=out_shape if n_out == 1 else (out_shape,) * n_out,
                   out_spec=ospec if n_out == 1 else (ospec,) * n_out, acc_shape=(tm, tn),
                   extras=extras, extra_specs=(ospec,) * len(extras), epilogue=epilogue)


def _add_residual(acc, ex, outs):
    outs[0][...] = ex[0][...] + acc


def _mm_nt(name, a, w, extras=(), epilogue=None):
    m, n = a.shape
    _, _, r, c = w.arr.shape
    tm = _tile(m, 512, 8)
    if w.cols:
        kk, tr = r, r
        tc = c if r * c * 2 <= (12 << 20) else _tile(c, 1408)
    else:
        kk, tr, tc = N_CHIPS * r, _tile(r, 512), _tile(c, 2048)
    ospec = pl.BlockSpec((tm, tr), lambda i, j, k: (i, j))
    return _matmul(name, "nt", a, w.arr, grid=(m // tm, kk // tr, n // tc),
                   a_spec=pl.BlockSpec((tm, tc), lambda i, j, k: (i, k)),
                   b_spec=_wspec(w, tr, tc, lambda i, j, k: j, lambda i, j, k: k),
                   out_shape=_sds((m, kk), F32), out_spec=ospec, acc_shape=(tm, tr),
                   extras=extras, extra_specs=(ospec,) * len(extras), epilogue=epilogue)


def _mm_tn(name, a, b, shards=None):
    m, kk = a.shape
    n = b.shape[1]
    tkm = _tile(m, 1024, 8)
    tk = _tile(kk, 512)
    if shards:
        ns = n // shards
        return _matmul(name, "tn", a, b, grid=(kk // tk, shards, m // tkm),
                       a_spec=pl.BlockSpec((tkm, tk), lambda i, j, k: (k, i)),
                       b_spec=pl.BlockSpec((tkm, ns), lambda i, j, k: (k, j)),
                       out_shape=_sds((shards, kk, ns), F32),
                       out_spec=pl.BlockSpec((None, tk, ns), lambda i, j, k: (j, i, 0)), acc_shape=(tk, ns))
    tn = _tile(n, 2048)
    return _matmul(name, "tn", a, b, grid=(kk // tk, n // tn, m // tkm),
                   a_spec=pl.BlockSpec((tkm, tk), lambda i, j, k: (k, i)),
                   b_spec=pl.BlockSpec((tkm, tn), lambda i, j, k: (k, j)),
                   out_shape=_sds((kk, n), F32), out_spec=pl.BlockSpec((tk, tn), lambda i, j, k: (i, j)),
                   acc_shape=(tk, tn))


def _rmsnorm_fwd(name, x, gain):
    m, d = x.shape
    tm = _tile(m, 512, 8)

    def body(x_ref, g_ref, o_ref):
        xv = x_ref[...]
        r = lax.rsqrt(jnp.mean(xv * xv, axis=-1, keepdims=True) + EPS)
        o_ref[...] = (xv * r * g_ref[...]).astype(o_ref.dtype)

    return pl.pallas_call(
        body, name=name, grid=(m // tm,),
        in_specs=[pl.BlockSpec((tm, d), lambda i: (i, 0)), pl.BlockSpec((1, d), lambda i: (0, 0))],
        out_specs=pl.BlockSpec((tm, d), lambda i: (i, 0)),
        out_shape=_sds((m, d), BF16),
        compiler_params=_params(("parallel",)),
    )(x, gain)


def _rmsnorm_bwd(name, x, dh, gain, dres=None):
    m, d = x.shape
    tm = _tile(m, 256, 8)
    nsteps = m // tm
    with_dx = dres is not None

    def body(*refs):
        if with_dx:
            x_ref, dh_ref, g_ref, dres_ref, dx_ref, dxb_ref, dg_ref, acc_ref = refs
        else:
            x_ref, dh_ref, g_ref, dg_ref, acc_ref = refs
        i = pl.program_id(0)
        xv = x_ref[...]
        r = lax.rsqrt(jnp.mean(xv * xv, axis=-1, keepdims=True) + EPS)
        xh = xv * r
        dhv = dh_ref[...].astype(F32)
        contrib = (dhv * xh).reshape(tm // 8, 8, d).sum(axis=0)

        @pl.when(i == 0)
        def _():
            acc_ref[...] = contrib

        @pl.when(i > 0)
        def _():
            acc_ref[...] += contrib

        @pl.when(i == nsteps - 1)
        def _():
            dg_ref[...] = jnp.sum(acc_ref[...], axis=0, keepdims=True)

        if with_dx:
            dxh = dhv * g_ref[...]
            dx = r * (dxh - xh * jnp.mean(dxh * xh, axis=-1, keepdims=True)) + dres_ref[...]
            dx_ref[...] = dx
            dxb_ref[...] = dx.astype(BF16)

    row = pl.BlockSpec((tm, d), lambda i: (i, 0))
    vec = pl.BlockSpec((1, d), lambda i: (0, 0))
    if with_dx:
        return pl.pallas_call(
            body, name=name, grid=(nsteps,),
            in_specs=[row, row, vec, row], out_specs=(row, row, vec),
            out_shape=(_sds((m, d), F32), _sds((m, d), BF16), _sds((1, d), F32)),
            scratch_shapes=[pltpu.VMEM((8, d), F32)],
            compiler_params=_params(("arbitrary",)),
        )(x, dh, gain, dres)
    return pl.pallas_call(
        body, name=name, grid=(nsteps,),
        in_specs=[row, row, vec], out_specs=vec,
        out_shape=_sds((1, d), F32),
        scratch_shapes=[pltpu.VMEM((8, d), F32)],
        compiler_params=_params(("arbitrary",)),
    )(x, dh, gain)


def _ffn_fwd(tag, x, gain, wgu, wdown):
    m, d = x.shape
    ns = wgu.arr.shape[3]
    f = 2 * ns
    h = _rmsnorm_fwd(f"{tag}_norm", x, gain)
    tm = _tile(m, 1024, 8)
    tn = _tile(ns, 256)
    nb = ns // tn

    def gu_body(h_ref, wg_ref, wu_ref, act_ref, gu_ref):
        hv = h_ref[...]
        g = jnp.dot(hv, wg_ref[...], preferred_element_type=F32)
        u = jnp.dot(hv, wu_ref[...], preferred_element_type=F32)
        act_ref[...] = (g * jax.nn.sigmoid(g) * u).astype(BF16)
        gu_ref[0] = g.astype(BF16)
        gu_ref[1] = u.astype(BF16)

    act, gu = pl.pallas_call(
        gu_body, name=f"{tag}_gu", grid=(m // tm, 2 * nb),
        in_specs=[pl.BlockSpec((tm, d), lambda i, j: (i, 0)),
                  _wspec(wgu, d, tn, lambda i, j: 0, lambda i, j: j),
                  _wspec(wgu, d, tn, lambda i, j: 0, lambda i, j: 2 * nb + j)],
        out_specs=(pl.BlockSpec((tm, tn), lambda i, j: (i, j)),
                   pl.BlockSpec((2, tm, tn), lambda i, j: (0, i, j))),
        out_shape=(_sds((m, f), BF16), _sds((2, m, f), BF16)),
        compiler_params=_params(("parallel", "parallel")),
    )(h, wgu.arr, wgu.arr)

    tk = wdown.arr.shape[2]
    tnd = _tile(d, 1024)

    def down_epilogue(acc, ex, outs):
        outs[0][...] = ex[0][...] + 0.5 * acc

    ospec = pl.BlockSpec((tm, tnd), lambda i, j, k: (i, j))
    y = _matmul(
        f"{tag}_down", "nn", act, wdown.arr, grid=(m // tm, d // tnd, f // tk),
        a_spec=pl.BlockSpec((tm, tk), lambda i, j, k: (i, k)),
        b_spec=_wspec(wdown, tk, tnd, lambda i, j, k: k, lambda i, j, k: j),
        out_shape=_sds((m, d), F32), out_spec=ospec, acc_shape=(tm, tnd),
        extras=(x,), extra_specs=(ospec,), epilogue=down_epilogue)
    return y, (x, h, act, gu)


def _ffn_bwd(tag, saved, gain, wgu, wdown, dy, dyb):
    x, h, act, gu = saved
    m, d = x.shape
    ns = wgu.arr.shape[3]
    f = 2 * ns

    def dact_epilogue(acc, ex, outs):
        g = ex[0][0].astype(F32)
        u = ex[0][1].astype(F32)
        da = 0.5 * acc
        s = jax.nn.sigmoid(g)
        outs[0][0] = (da * u * s * (1.0 + g * (1.0 - s))).astype(BF16)
        outs[0][1] = (da * g * s).astype(BF16)

    tma = _tile(m, 512, 8)
    tna = wdown.arr.shape[2]
    gspec = pl.BlockSpec((2, tma, tna), lambda i, j, k: (0, i, j))
    dgu = _matmul(
        f"{tag}_dact", "nt", dyb, wdown.arr, grid=(m // tma, f // tna, 1),
        a_spec=pl.BlockSpec((tma, d), lambda i, j, k: (i, 0)),
        b_spec=_wspec(wdown, tna, d, lambda i, j, k: j, lambda i, j, k: 0),
        out_shape=_sds((2, m, f), BF16), out_spec=gspec, acc_shape=(tma, tna),
        extras=(gu,), extra_specs=(gspec,), epilogue=dact_epilogue)

    tkm = _tile(m, 2048, 8)
    tf = _tile(f, 1408)
    tnd = _tile(d, 1024)

    def half_epilogue(acc, ex, outs):
        outs[0][...] = 0.5 * acc

    dwdown = _matmul(
        f"{tag}_dwdown", "tn", act, dyb, grid=(f // tf, d // tnd, m // tkm),
        a_spec=pl.BlockSpec((tkm, tf), lambda i, j, k: (k, i)),
        b_spec=pl.BlockSpec((tkm, tnd), lambda i, j, k: (k, j)),
        out_shape=_sds((f, d), F32), out_spec=pl.BlockSpec((tf, tnd), lambda i, j, k: (i, j)),
        acc_shape=(tf, tnd), epilogue=half_epilogue)

    td = _tile(d, 512)
    dwgu = _matmul(
        f"{tag}_dwgu", "tn", h, dgu, grid=(d // td, 4, m // tkm),
        a_spec=pl.BlockSpec((tkm, td), lambda i, j, k: (k, i)),
        b_spec=pl.BlockSpec((None, tkm, ns), lambda i, j, k: (j // 2, k, j % 2)),
        out_shape=_sds((4, d, ns), F32), out_spec=pl.BlockSpec((None, td, ns), lambda i, j, k: (j, i, 0)),
        acc_shape=(td, ns))

    tmh = _tile(m, 512, 8)
    dh = _matmul(
        f"{tag}_dh", "nt", dgu, wgu.arr, grid=(m // tmh, 1, 4),
        a_spec=pl.BlockSpec((None, tmh, ns), lambda i, j, k: (k // 2, i, k % 2)),
        b_spec=_wspec(wgu, d, ns, lambda i, j, k: 0, lambda i, j, k: k),
        out_shape=_sds((m, d), F32), out_spec=pl.BlockSpec((tmh, d), lambda i, j, k: (i, 0)),
        acc_shape=(tmh, d))

    dx, dxb, dgain = _rmsnorm_bwd(f"{tag}_dnorm", x, dh, gain, dy)
    return dx, dxb, dgain, dwgu, dwdown


SB_KEY_BLOCK = 256
SB_QUERY_ROWS = 1024


def _split2(x):
    hi = x.astype(BF16)
    return hi, (x - hi.astype(F32)).astype(BF16)


def _dot2(parts, mat):
    return jnp.dot(parts[0], mat, preferred_element_type=F32) + jnp.dot(parts[1], mat, preferred_element_type=F32)


def _sb_logits(q, k, scale):
    z = lax.dot_general(q, k, _NT, preferred_element_type=F32) * scale
    lp = jnp.minimum(z, 0.0) - jnp.log(1.0 + jnp.exp(-jnp.abs(z)))
    return lp, lp - z


def _comm_hooks(comm, n_in, n_out, grid):
    if comm is None:
        return (lambda refs: None), (lambda refs: None)
    nci, nco = len(comm.ins), len(comm.out_shapes)

    def comm_refs(refs):
        return refs[n_in:n_in + nci] + refs[n_in + nci + n_out:]

    def at_first(refs):
        @pl.when(sum(pl.program_id(a) for a in range(len(grid))) == 0)
        def _():
            comm.start(comm_refs(refs))

    def at_last(refs):
        @pl.when(sum(pl.program_id(a) for a in range(len(grid))) == sum(g - 1 for g in grid))
        def _():
            comm.finish(comm_refs(refs))

    return at_first, at_last


def _sb_fwd(name, qkv, n_heads, comm=None):
    length = qkv.shape[0]
    kb = _tile(length, SB_KEY_BLOCK, LANES)
    tq = _tile(length, SB_QUERY_ROWS, kb)
    r = tq // kb
    scale = 1.0 / math.sqrt(HEAD_DIM)
    grid = (n_heads, length // tq)
    nci = len(comm.ins) if comm else 0
    at_first, at_last = _comm_hooks(comm, 3, 2, grid)

    def body(*refs):
        q_ref, k_ref, v_ref = refs[:3]
        o_ref, tot_ref = refs[3 + nci:5 + nci]
        at_first(refs)
        qi = pl.program_id(1)
        q = q_ref[...].astype(BF16)
        row = qi * tq + lax.broadcasted_iota(jnp.int32, (tq, kb), 0)
        col = lax.broadcasted_iota(jnp.int32, (tq, kb), 1)
        later = (lax.broadcasted_iota(jnp.int32, (kb, kb), 0) > lax.broadcasted_iota(jnp.int32, (kb, kb), 1)).astype(BF16)

        def block(kbi, carry, masked):
            c, acc = carry
            ks = pl.multiple_of(kbi * kb, kb)
            k = k_ref[pl.ds(ks, kb), :].astype(BF16)
            v = v_ref[pl.ds(ks, kb), :].astype(BF16)
            lp, ln = _sb_logits(q, k, scale)
            if masked:
                valid = (ks + col) < row
                ln = jnp.where(valid, ln, 0.0)
            w = jnp.exp(lp + (c + _dot2(_split2(ln), later)))
            if masked:
                w = jnp.where(valid, w, 0.0)
            acc = acc + jnp.dot(w.astype(BF16), v, preferred_element_type=F32)
            return c + jnp.sum(ln, axis=1, keepdims=True), acc

        carry = (jnp.zeros((tq, 1), F32), jnp.zeros((tq, HEAD_DIM), F32))
        for dgl in range(r - 1, -1, -1):
            carry = block(qi * r + dgl, carry, True)
        carry = lax.fori_loop(0, qi * r, lambda i, cr: block(qi * r - 1 - i, cr, False), carry)
        o_ref[...] = carry[1]
        tot_ref[...] = jnp.broadcast_to(carry[0], (tq, HEAD_DIM))
        at_last(refs)

    h = n_heads
    qblk = pl.BlockSpec((tq, HEAD_DIM), lambda hh, i: (i, hh))
    out = _sds((length, h * HEAD_DIM), F32)
    res = pl.pallas_call(
        body, name=name, grid=grid,
        in_specs=[qblk,
                  pl.BlockSpec((length, HEAD_DIM), lambda hh, i: (0, h + hh)),
                  pl.BlockSpec((length, HEAD_DIM), lambda hh, i: (0, 2 * h + hh))] + [_ANY] * nci,
        out_specs=(qblk, qblk) + ((_ANY,) * len(comm.out_shapes) if comm else ()),
        out_shape=(out, out) + (tuple(comm.out_shapes) if comm else ()),
        scratch_shapes=comm.scratch() if comm else [],
        compiler_params=_params(("arbitrary", "arbitrary") if comm else ("parallel", "arbitrary")),
    )(qkv, qkv, qkv, *(comm.ins if comm else ()))
    return (res[0], res[1]), tuple(res[2:])


def _sb_bwd(name, qkv, tot, do, n_heads, comm=None):
    length = qkv.shape[0]
    kb = _tile(length, SB_KEY_BLOCK, LANES)
    tq = _tile(length, SB_QUERY_ROWS, kb)
    r = tq // kb
    scale = 1.0 / math.sqrt(HEAD_DIM)
    grid = (n_heads, length // tq)
    nci = len(comm.ins) if comm else 0
    at_first, at_last = _comm_hooks(comm, 5, 3, grid)

    def body(*refs):
        q_ref, k_ref, v_ref, tot_ref, do_ref = refs[:5]
        dq_ref, dk_ref, dv_ref = refs[5 + nci:8 + nci]
        at_first(refs)
        qi = pl.program_id(1)

        @pl.when(qi == 0)
        def _():
            dk_ref[...] = jnp.zeros_like(dk_ref)
            dv_ref[...] = jnp.zeros_like(dv_ref)

        q = q_ref[...].astype(BF16)
        dob = do_ref[...].astype(BF16)
        tot = tot_ref[:, 0:1]
        row = qi * tq + lax.broadcasted_iota(jnp.int32, (tq, kb), 0)
        col = lax.broadcasted_iota(jnp.int32, (tq, kb), 1)
        jj = lax.broadcasted_iota(jnp.int32, (kb, kb), 0)
        ss = lax.broadcasted_iota(jnp.int32, (kb, kb), 1)
        later = (jj > ss).astype(BF16)
        before = (jj < ss).astype(BF16)

        def block(kbi, carry, masked):
            pl_, pe, dq = carry
            ks = pl.multiple_of(kbi * kb, kb)
            k = k_ref[pl.ds(ks, kb), :].astype(BF16)
            v = v_ref[pl.ds(ks, kb), :].astype(BF16)
            lp, ln = _sb_logits(q, k, scale)
            if masked:
                valid = (ks + col) < row
                ln = jnp.where(valid, ln, 0.0)
            tb = jnp.sum(ln, axis=1, keepdims=True)
            w = jnp.exp(lp + ((tot - pl_ - tb) + _dot2(_split2(ln), later)))
            if masked:
                w = jnp.where(valid, w, 0.0)
            e = w * lax.dot_general(dob, v, _NT, preferred_element_type=F32)
            dv_ref[pl.ds(ks, kb), :] += lax.dot_general(w.astype(BF16), dob, _TN, preferred_element_type=F32)
            beta = jnp.exp(lp)
            dz = e * (1.0 - beta) - beta * (pe + _dot2(_split2(e), before))
            if masked:
                dz = jnp.where(valid, dz, 0.0)
            dzb = (dz * scale).astype(BF16)
            dq = dq + jnp.dot(dzb, k, preferred_element_type=F32)
            dk_ref[pl.ds(ks, kb), :] += lax.dot_general(dzb, q, _TN, preferred_element_type=F32)
            return pl_ + tb, pe + jnp.sum(e, axis=1, keepdims=True), dq

        carry = (jnp.zeros((tq, 1), F32), jnp.zeros((tq, 1), F32), jnp.zeros((tq, HEAD_DIM), F32))
        carry = lax.fori_loop(0, qi * r, lambda i, cr: block(i, cr, False), carry)
        for dgl in range(r):
            carry = block(qi * r + dgl, carry, True)
        dq_ref[...] = carry[2]
        at_last(refs)

    h = n_heads
    qblk = pl.BlockSpec((tq, HEAD_DIM), lambda hh, i: (i, hh))
    full = pl.BlockSpec((length, HEAD_DIM), lambda hh, i: (0, hh))
    out = _sds((length, h * HEAD_DIM), F32)
    res = pl.pallas_call(
        body, name=name, grid=grid,
        in_specs=[qblk,
                  pl.BlockSpec((length, HEAD_DIM), lambda hh, i: (0, h + hh)),
                  pl.BlockSpec((length, HEAD_DIM), lambda hh, i: (0, 2 * h + hh)),
                  qblk, qblk] + [_ANY] * nci,
        out_specs=(qblk, full, full) + ((_ANY,) * len(comm.out_shapes) if comm else ()),
        out_shape=(out, out, out) + (tuple(comm.out_shapes) if comm else ()),
        scratch_shapes=comm.scratch() if comm else [],
        compiler_params=_params(("arbitrary", "arbitrary") if comm else ("parallel", "arbitrary")),
    )(qkv, qkv, qkv, tot, do, *(comm.ins if comm else ()))
    return tuple(res[:3]), tuple(res[3:])


def _head_rms(xh):
    r = lax.rsqrt(jnp.mean(xh * xh, axis=-1, keepdims=True) + EPS)
    return xh * r, r


def _mem_fwd(name, qsrc, qcol0, kv, gq, gk):
    length = qsrc.shape[0]
    mm, mw2 = kv.shape
    mw = mw2 // 2
    nh = mw // HEAD_DIM
    tq = _tile(length, 512, 8)
    inv = 1.0 / math.sqrt(HEAD_DIM)
    assert qcol0 % mw == 0

    def body(q_ref, kv_ref, gq_ref, gk_ref, o_ref):
        for hh in range(nh):
            sl = slice(hh * HEAD_DIM, (hh + 1) * HEAD_DIM)
            qn = _head_rms(q_ref[:, sl])[0] * gq_ref[...]
            kn = _head_rms(kv_ref[:, sl])[0] * gk_ref[...]
            vh = kv_ref[:, mw + hh * HEAD_DIM:mw + (hh + 1) * HEAD_DIM].astype(BF16)
            s = lax.dot_general(qn.astype(BF16), kn.astype(BF16), _NT, preferred_element_type=F32) * inv
            p = jnp.exp(s - jnp.max(s, axis=-1, keepdims=True))
            p = p / jnp.sum(p, axis=-1, keepdims=True)
            o_ref[:, sl] = jnp.dot(p.astype(BF16), vh, preferred_element_type=F32)

    vec = pl.BlockSpec((1, HEAD_DIM), lambda i: (0, 0))
    return pl.pallas_call(
        body, name=name, grid=(length // tq,),
        in_specs=[pl.BlockSpec((tq, mw), lambda i: (i, qcol0 // mw)),
                  pl.BlockSpec((mm, mw2), lambda i: (0, 0)), vec, vec],
        out_specs=pl.BlockSpec((tq, mw), lambda i: (i, 0)),
        out_shape=_sds((length, mw), F32),
        compiler_params=_params(("parallel",)),
    )(qsrc, kv, gq, gk)


def _mem_bwd(name, qsrc, qcol0, kv, gq, gk, dsrc, docol0):
    length = qsrc.shape[0]
    mm, mw2 = kv.shape
    mw = mw2 // 2
    nh = mw // HEAD_DIM
    tq = _tile(length, 512, 8)
    nsteps = length // tq
    inv = 1.0 / math.sqrt(HEAD_DIM)

    def body(q_ref, kv_ref, gq_ref, gk_ref, do_ref, dq_ref, dkv_ref, dgq_ref, dgk_ref):
        i = pl.program_id(0)

        @pl.when(i == 0)
        def _():
            dkv_ref[...] = jnp.zeros_like(dkv_ref)
            dgq_ref[...] = jnp.zeros_like(dgq_ref)

        gqv = gq_ref[...]
        gkv = gk_ref[...]
        for hh in range(nh):
            sl = slice(hh * HEAD_DIM, (hh + 1) * HEAD_DIM)
            slv = slice(mw + hh * HEAD_DIM, mw + (hh + 1) * HEAD_DIM)
            qhat, rq = _head_rms(q_ref[:, sl])
            qn = (qhat * gqv).astype(BF16)
            kn = (_head_rms(kv_ref[:, sl])[0] * gkv).astype(BF16)
            vh = kv_ref[:, slv].astype(BF16)
            dob = do_ref[:, sl].astype(BF16)
            s = lax.dot_general(qn, kn, _NT, preferred_element_type=F32) * inv
            p = jnp.exp(s - jnp.max(s, axis=-1, keepdims=True))
            p = p / jnp.sum(p, axis=-1, keepdims=True)
            dp = lax.dot_general(dob, vh, _NT, preferred_element_type=F32)
            ds = (p * (dp - jnp.sum(dp * p, axis=-1, keepdims=True)) * inv).astype(BF16)
            dqn = jnp.dot(ds, kn, preferred_element_type=F32)
            dkv_ref[:, sl] += lax.dot_general(ds, qn, _TN, preferred_element_type=F32)
            dkv_ref[:, slv] += lax.dot_general(p.astype(BF16), dob, _TN, preferred_element_type=F32)
            dgq_ref[...] += jnp.sum(dqn * qhat, axis=0, keepdims=True)
            dqh = dqn * gqv
            dq_ref[:, sl] = rq * (dqh - qhat * jnp.mean(dqh * qhat, axis=-1, keepdims=True))

        @pl.when(i == nsteps - 1)
        def _():
            dgk = jnp.zeros((1, HEAD_DIM), F32)
            for hh in range(nh):
                sl = slice(hh * HEAD_DIM, (hh + 1) * HEAD_DIM)
                khat, rk = _head_rms(kv_ref[:, sl])
                dkn = dkv_ref[:, sl]
                dgk = dgk + jnp.sum(dkn * khat, axis=0, keepdims=True)
                dkh = dkn * gkv
                dkv_ref[:, sl] = rk * (dkh - khat * jnp.mean(dkh * khat, axis=-1, keepdims=True))
            dgk_ref[...] = dgk

    vec = pl.BlockSpec((1, HEAD_DIM), lambda i: (0, 0))
    kvs = pl.BlockSpec((mm, mw2), lambda i: (0, 0))
    blk = pl.BlockSpec((tq, mw), lambda i: (i, 0))
    return pl.pallas_call(
        body, name=name, grid=(nsteps,),
        in_specs=[pl.BlockSpec((tq, mw), lambda i: (i, qcol0 // mw)), kvs, vec, vec,
                  pl.BlockSpec((tq, mw), lambda i: (i, docol0 // mw))],
        out_specs=(blk, kvs, vec, vec),
        out_shape=(_sds((length, mw), F32), _sds((mm, mw2), F32), _sds((1, HEAD_DIM), F32), _sds((1, HEAD_DIM), F32)),
        compiler_params=_params(("arbitrary",)),
    )(qsrc, kv, gq, gk, dsrc)


S5_TILE_GROUPS = LANES // S5_GROUP
S5_TILE_STATES = S5_TILE_GROUPS * S5_STATE
S5_SCAN_ROWS = 64


def _dotf(a, b, dims=_DIMS["nn"]):
    a_hi, a_lo = _split2(a)
    b_hi, b_lo = _split2(b)
    return (lax.dot_general(a_hi, b_hi, dims, preferred_element_type=F32)
            + lax.dot_general(a_hi, b_lo, dims, preferred_element_type=F32)
            + lax.dot_general(a_lo, b_hi, dims, preferred_element_type=F32))


def _s5_prep(log_dt, a_re, a_im, b_re, b_im, c_re, c_im):
    g, n = a_re.shape
    nt = g // S5_TILE_GROUPS
    dt = jnp.exp(log_dt)[:, None]
    mag = jnp.exp(a_re * dt)
    ab_re = mag * jnp.cos(a_im * dt)
    ab_im = mag * jnp.sin(a_im * dt)
    den = a_re * a_re + a_im * a_im
    num_re = ab_re - 1.0
    co_re = (num_re * a_re + ab_im * a_im) / den
    co_im = (ab_im * a_re - num_re * a_im) / den
    bb_re = co_re[..., None] * b_re - co_im[..., None] * b_im
    bb_im = co_re[..., None] * b_im + co_im[..., None] * b_re
    eye = jnp.eye(S5_TILE_GROUPS, dtype=F32)

    def blk_b(bb):
        t = bb.reshape(nt, S5_TILE_GROUPS, n, S5_GROUP).transpose(0, 1, 3, 2)
        return jnp.einsum("jgcn,gh->jgchn", t, eye).reshape(nt, LANES, S5_TILE_STATES)

    def blk_c(cc):
        t = cc.reshape(nt, S5_TILE_GROUPS, S5_GROUP, n).transpose(0, 1, 3, 2)
        return jnp.einsum("jgnc,gh->jgnhc", t, eye).reshape(nt, S5_TILE_STATES, LANES)

    return (ab_re.reshape(1, g * n), ab_im.reshape(1, g * n), blk_b(bb_re), blk_b(bb_im), blk_c(c_re), blk_c(c_im))


def _s5_bu(name, usrc, bblk_re, bblk_im):
    length = usrc.shape[0]
    nt = bblk_re.shape[0]
    tm = _tile(length, 512, 8)

    def body(u_ref, br_ref, bi_ref, or_ref, oi_ref):
        u = u_ref[...]
        or_ref[...] = _dotf(u, br_ref[...])
        oi_ref[...] = _dotf(u, bi_ref[...])

    bspec = pl.BlockSpec((None, LANES, S5_TILE_STATES), lambda i, j: (j, 0, 0))
    ospec = pl.BlockSpec((tm, S5_TILE_STATES), lambda i, j: (i, j))
    out = _sds((length, nt * S5_TILE_STATES), F32)
    return pl.pallas_call(
        body, name=name, grid=(length // tm, nt),
        in_specs=[pl.BlockSpec((tm, LANES), lambda i, j: (i, j)), bspec, bspec],
        out_specs=(ospec, ospec), out_shape=(out, out),
        compiler_params=_params(("parallel", "parallel")),
    )(usrc, bblk_re, bblk_im)


def _scan_rows(hr, hi, ar, ai, reverse):
    t = hr.shape[0]
    rows = lax.broadcasted_iota(jnp.int32, hr.shape, 0)
    d = 1
    while d < t:
        if reverse:
            sr = jnp.where(rows < t - d, pltpu.roll(hr, t - d, 0), 0.0)
            si = jnp.where(rows < t - d, pltpu.roll(hi, t - d, 0), 0.0)
        else:
            sr = jnp.where(rows >= d, pltpu.roll(hr, d, 0), 0.0)
            si = jnp.where(rows >= d, pltpu.roll(hi, d, 0), 0.0)
        hr, hi = hr + ar * sr - ai * si, hi + ar * si + ai * sr
        ar, ai = ar * ar - ai * ai, 2.0 * ar * ai
        d *= 2
    return hr, hi


def _s5_scan(name, x_re, x_im, a_re, a_im, reverse=False, h=None, bu=None):
    length, width = x_re.shape
    t = _tile(length, S5_SCAN_ROWS, 8)
    nsteps = length // t
    with_sum = h is not None

    def body(*refs):
        if with_sum:
            (xr_ref, xi_ref, ar_ref, ai_ref, hr_ref, hi_ref, br_ref, bi_ref,
             or_ref, oi_ref, sr_ref, si_ref, cr, ci, pr, pi, accr, acci) = refs
        else:
            xr_ref, xi_ref, ar_ref, ai_ref, or_ref, oi_ref, cr, ci, pr, pi = refs
        step = pl.program_id(0)
        ar = ar_ref[...]
        ai = ai_ref[...]
        edge = 0 if reverse else t - 1
        last = t - 1 if reverse else 0

        @pl.when(step == 0)
        def _():
            cr[...] = jnp.zeros_like(cr)
            ci[...] = jnp.zeros_like(ci)
            rows = lax.broadcasted_iota(jnp.int32, (t, width), 0)
            seed_r = jnp.where(rows == last, ar, 0.0)
            seed_i = jnp.where(rows == last, ai, 0.0)
            p_r, p_i = _scan_rows(seed_r, seed_i, ar, ai, reverse)
            pr[...] = p_r
            pi[...] = p_i
            if with_sum:
                accr[...] = jnp.zeros_like(accr)
                acci[...] = jnp.zeros_like(acci)

        hr, hi = _scan_rows(xr_ref[...], xi_ref[...], ar, ai, reverse)
        c_r = cr[...]
        c_i = ci[...]
        p_r = pr[...]
        p_i = pi[...]
        hr = hr + p_r * c_r - p_i * c_i
        hi = hi + p_r * c_i + p_i * c_r
        or_ref[...] = hr
        oi_ref[...] = hi
        cr[...] = hr[edge:edge + 1, :]
        ci[...] = hi[edge:edge + 1, :]
        if with_sum:
            wr = hr_ref[...] - br_ref[...]
            wi = hi_ref[...] - bi_ref[...]
            accr[...] += (wr * hr + wi * hi).reshape(t // 8, 8, width).sum(axis=0)
            acci[...] += (wr * hi - wi * hr).reshape(t // 8, 8, width).sum(axis=0)

            @pl.when(step == nsteps - 1)
            def _():
                sr_ref[...] = jnp.sum(accr[...], axis=0, keepdims=True)
                si_ref[...] = jnp.sum(acci[...], axis=0, keepdims=True)

    if reverse:
        blk = pl.BlockSpec((t, width), lambda s: (nsteps - 1 - s, 0))
    else:
        blk = pl.BlockSpec((t, width), lambda s: (s, 0))
    vec = pl.BlockSpec((1, width), lambda s: (0, 0))
    full = _sds((length, width), F32)
    row = _sds((1, width), F32)
    scratch = [pltpu.VMEM((1, width), F32), pltpu.VMEM((1, width), F32),
               pltpu.VMEM((t, width), F32), pltpu.VMEM((t, width), F32)]
    if with_sum:
        return pl.pallas_call(
            body, name=name, grid=(nsteps,),
            in_specs=[blk, blk, vec, vec, blk, blk, blk, blk],
            out_specs=(blk, blk, vec, vec), out_shape=(full, full, row, row),
            scratch_shapes=scratch + [pltpu.VMEM((8, width), F32), pltpu.VMEM((8, width), F32)],
            compiler_params=_params(("arbitrary",)),
        )(x_re, x_im, a_re, a_im, h[0], h[1], bu[0], bu[1])
    return pl.pallas_call(
        body, name=name, grid=(nsteps,),
        in_specs=[blk, blk, vec, vec], out_specs=(blk, blk), out_shape=(full, full),
        scratch_shapes=scratch,
        compiler_params=_params(("arbitrary",)),
    )(x_re, x_im, a_re, a_im)


_GELU_C = math.sqrt(2.0 / math.pi)


def _gelu(y):
    return 0.5 * y * (1.0 + jnp.tanh(_GELU_C * (y + 0.044715 * y * y * y)))


def _gelu_grad(y):
    th = jnp.tanh(_GELU_C * (y + 0.044715 * y * y * y))
    return 0.5 * (1.0 + th) + 0.5 * y * (1.0 - th * th) * _GELU_C * (1.0 + 3 * 0.044715 * y * y)


def _s5_out(name, h_re, h_im, cblk_re, cblk_im, usrc, dskip):
    length = h_re.shape[0]
    nt = cblk_re.shape[0]
    tm = _tile(length, 512, 8)

    def body(hr_ref, hi_ref, cr_ref, ci_ref, u_ref, d_ref, y_ref, y2_ref):
        y = _dotf(hr_ref[...], cr_ref[...]) - _dotf(hi_ref[...], ci_ref[...]) + d_ref[...] * u_ref[...]
        y_ref[...] = y
        y2_ref[...] = _gelu(y)

    hspec = pl.BlockSpec((tm, S5_TILE_STATES), lambda i, j: (i, j))
    cspec = pl.BlockSpec((None, S5_TILE_STATES, LANES), lambda i, j: (j, 0, 0))
    uspec = pl.BlockSpec((tm, LANES), lambda i, j: (i, j))
    out = _sds((length, nt * LANES), F32)
    return pl.pallas_call(
        body, name=name, grid=(length // tm, nt),
        in_specs=[hspec, hspec, cspec, cspec, uspec, pl.BlockSpec((1, LANES), lambda i, j: (0, j))],
        out_specs=(uspec, uspec), out_shape=(out, out),
        compiler_params=_params(("parallel", "parallel")),
    )(h_re, h_im, cblk_re, cblk_im, usrc, dskip)


def _s5_fwd(tag, usrc, prep, dskip, wglu):
    ab_re, ab_im, bb_re, bb_im, cb_re, cb_im = prep
    bu = _s5_bu(f"{tag}_bu", usrc, bb_re, bb_im)
    hs = _s5_scan(f"{tag}_scan", bu[0], bu[1], ab_re, ab_im)
    y, y2 = _s5_out(f"{tag}_out", hs[0], hs[1], cb_re, cb_im, usrc, dskip)

    def glu_epilogue(acc, ex, outs):
        outs[0][...] = acc
        outs[1][...] = ex[0][...] * jax.nn.sigmoid(acc)

    gl, tok = _mm_nn(f"{tag}_glu", y2, wglu, extras=(y2,), epilogue=glu_epilogue, n_out=2)
    return tok, (bu, hs, y, y2, gl)


def _s5_bwd(tag, usrc, prep, dskip, wglu, saved, dsrc):
    ab_re, ab_im, bb_re, bb_im, cb_re, cb_im = prep
    bu, hs, y, y2, gl = saved
    length = usrc.shape[0]
    tw = y.shape[1]
    nt = bb_re.shape[0]

    tme = _tile(length, 512, 8)

    def gate_body(dt_ref, y2_ref, gl_ref, dgl_ref, dy2_ref):
        s = jax.nn.sigmoid(gl_ref[...])
        dt = dt_ref[...]
        dgl_ref[...] = (dt * y2_ref[...] * s * (1.0 - s)).astype(BF16)
        dy2_ref[...] = dt * s

    espec = pl.BlockSpec((tme, tw), lambda i: (i, 0))
    dgl, dy2a = pl.pallas_call(
        gate_body, name=f"{tag}_dgate", grid=(length // tme,),
        in_specs=[espec, espec, espec], out_specs=(espec, espec),
        out_shape=(_sds((length, tw), BF16), _sds((length, tw), F32)),
        compiler_params=_params(("parallel",)),
    )(dsrc, y2, gl)

    dwglu = _mm_tn(f"{tag}_dwglu", y2, dgl)

    def dy_epilogue(acc, ex, outs):
        outs[0][...] = (ex[0][...] + acc) * _gelu_grad(ex[1][...])

    dy = _mm_nt(f"{tag}_dy", dgl, wglu, extras=(dy2a, y), epilogue=dy_epilogue)

    tmh = _tile(length, 512, 8)

    def dh_body(dy_ref, cr_ref, ci_ref, gr_ref, gi_ref):
        dyv = dy_ref[...]
        gr_ref[...] = _dotf(dyv, cr_ref[...], _NT)
        gi_ref[...] = -_dotf(dyv, ci_ref[...], _NT)

    hspec = pl.BlockSpec((tmh, S5_TILE_STATES), lambda i, j: (i, j))
    cspec = pl.BlockSpec((None, S5_TILE_STATES, LANES), lambda i, j: (j, 0, 0))
    uspec = pl.BlockSpec((tmh, LANES), lambda i, j: (i, j))
    wide = _sds((length, nt * S5_TILE_STATES), F32)
    g_re, g_im = pl.pallas_call(
        dh_body, name=f"{tag}_dh", grid=(length // tmh, nt),
        in_specs=[uspec, cspec, cspec], out_specs=(hspec, hspec), out_shape=(wide, wide),
        compiler_params=_params(("parallel", "parallel")),
    )(dy, cb_re, cb_im)

    lam_re, lam_im, s_re, s_im = _s5_scan(f"{tag}_rscan", g_re, g_im, ab_re, -ab_im, reverse=True, h=hs, bu=bu)
    den = ab_re * ab_re + ab_im * ab_im
    da_re = (ab_re * s_re - ab_im * s_im) / den
    da_im = (ab_re * s_im + ab_im * s_re) / den

    def du_body(lr_ref, li_ref, br_ref, bi_ref, dy_ref, d_ref, du_ref):
        du_ref[...] = (_dotf(lr_ref[...], br_ref[...], _NT) + _dotf(li_ref[...], bi_ref[...], _NT)
                       + dy_ref[...] * d_ref[...])

    bspec = pl.BlockSpec((None, LANES, S5_TILE_STATES), lambda i, j: (j, 0, 0))
    dvec = pl.BlockSpec((1, LANES), lambda i, j: (0, j))
    du = pl.pallas_call(
        du_body, name=f"{tag}_du", grid=(length // tmh, nt),
        in_specs=[hspec, hspec, bspec, bspec, uspec, dvec], out_specs=uspec,
        out_shape=_sds((length, tw), F32),
        compiler_params=_params(("parallel", "parallel")),
    )(lam_re, lam_im, bb_re, bb_im, dy, dskip)

    tkm = _tile(length, 512, 8)
    nk = length // tkm

    def dpar_body(u_ref, dy_ref, hr_ref, hi_ref, lr_ref, li_ref, dbr_ref, dbi_ref, dcr_ref, dci_ref, dd_ref):
        k = pl.program_id(1)

        @pl.when(k == 0)
        def _():
            for ref in (dbr_ref, dbi_ref, dcr_ref, dci_ref, dd_ref):
                ref[...] = jnp.zeros_like(ref)

        u = u_ref[...]
        dyv = dy_ref[...]
        dbr_ref[...] += _dotf(u, lr_ref[...], _TN)
        dbi_ref[...] += _dotf(u, li_ref[...], _TN)
        dcr_ref[...] += _dotf(hr_ref[...], dyv, _TN)
        dci_ref[...] -= _dotf(hi_ref[...], dyv, _TN)
        dd_ref[...] += jnp.sum(dyv * u, axis=0, keepdims=True)

    kspec_u = pl.BlockSpec((tkm, LANES), lambda j, k: (k, j))
    kspec_h = pl.BlockSpec((tkm, S5_TILE_STATES), lambda j, k: (k, j))
    ob = pl.BlockSpec((None, LANES, S5_TILE_STATES), lambda j, k: (j, 0, 0))
    oc = pl.BlockSpec((None, S5_TILE_STATES, LANES), lambda j, k: (j, 0, 0))
    dbr, dbi, dcr, dci, dd = pl.pallas_call(
        dpar_body, name=f"{tag}_dpar", grid=(nt, nk),
        in_specs=[kspec_u, kspec_u, kspec_h, kspec_h, kspec_h, kspec_h],
        out_specs=(ob, ob, oc, oc, pl.BlockSpec((1, LANES), lambda j, k: (0, j))),
        out_shape=(_sds(bb_re.shape, F32), _sds(bb_re.shape, F32), _sds(cb_re.shape, F32), _sds(cb_re.shape, F32),
                   _sds((1, tw), F32)),
        compiler_params=_params(("parallel", "arbitrary")),
    )(usrc, dy, hs[0], hs[1], lam_re, lam_im)
    return du, (da_re, da_im, dbr, dbi, dcr, dci), dd, dwglu


def _loss_head(name, y, target):
    m, d = y.shape
    tm = _tile(m, 256, 8)
    nsteps = m // tm

    def body(y_ref, t_ref, loss_ref, dy_ref, dyb_ref, acc_ref):
        i = pl.program_id(0)
        diff = y_ref[...] - t_ref[...]
        dy = diff * (1.0 / d)
        dy_ref[...] = dy
        dyb_ref[...] = dy.astype(BF16)
        sq = (diff * diff).reshape(tm // 8, 8, d).sum(axis=0)

        @pl.when(i == 0)
        def _():
            acc_ref[...] = sq

        @pl.when(i > 0)
        def _():
            acc_ref[...] += sq

        @pl.when(i == nsteps - 1)
        def _():
            loss_ref[...] = jnp.full(loss_ref.shape, jnp.sum(acc_ref[...]) * (0.5 / d), F32)

    row = pl.BlockSpec((tm, d), lambda i: (i, 0))
    return pl.pallas_call(
        body, name=name, grid=(nsteps,),
        in_specs=[row, row], out_specs=(pl.BlockSpec((8, LANES), lambda i: (0, 0)), row, row),
        out_shape=(_sds((8, LANES), F32), _sds((m, d), F32), _sds((m, d), BF16)),
        scratch_shapes=[pltpu.VMEM((8, d), F32)],
        compiler_params=_params(("arbitrary",)),
    )(y, target)


def _adamw(name, w, g, m, v):
    shape = w.shape
    total = math.prod(shape)
    if shape[-1] % LANES and total % LANES == 0:
        view = (total // LANES, LANES)
    else:
        view = (total // shape[-1], shape[-1])
    rows, cols = view
    tr = rows
    if rows * cols * 4 > (1 << 20) and rows % 16 == 0:
        tr = _row_tile(rows, cols)
    c1 = 1.0 / (1.0 - ADAM_B1 ** ADAM_STEP)
    c2 = 1.0 / (1.0 - ADAM_B2 ** ADAM_STEP)

    def body(w_ref, g_ref, m_ref, v_ref, d_ref, nm_ref, nv_ref):
        gv = g_ref[...]
        nm = ADAM_B1 * m_ref[...] + (1.0 - ADAM_B1) * gv
        nv = ADAM_B2 * v_ref[...] + (1.0 - ADAM_B2) * gv * gv
        nm_ref[...] = nm
        nv_ref[...] = nv
        d_ref[...] = -ADAM_LR * ((nm * c1) / (jnp.sqrt(nv * c2) + ADAM_EPS) + ADAM_WD * w_ref[...])

    spec = pl.BlockSpec((tr, cols), lambda i: (i, 0))
    out = _sds(view, F32)
    res = pl.pallas_call(
        body, name=name, grid=(rows // tr,),
        in_specs=[spec] * 4, out_specs=(spec,) * 3, out_shape=(out,) * 3,
        compiler_params=_params(("parallel",)),
    )(w.reshape(view), g.reshape(view), m.reshape(view), v.reshape(view))
    return tuple(r.reshape(shape) for r in res)


def _sum_leading(name, x):
    k, rows, cols = x.shape
    tr = _row_tile(rows, cols)

    def body(x_ref, o_ref):
        acc = x_ref[0].astype(F32)
        for j in range(1, k):
            acc = acc + x_ref[j].astype(F32)
        o_ref[...] = acc

    return pl.pallas_call(
        body, name=name, grid=(rows // tr,),
        in_specs=[pl.BlockSpec((k, tr, cols), lambda i: (0, i, 0))],
        out_specs=pl.BlockSpec((tr, cols), lambda i: (i, 0)), out_shape=_sds((rows, cols), F32),
        compiler_params=_params(("parallel",)),
    )(x)


def _sum_chips_into_half(name, slabs):
    s, hf, r, c = slabs.shape
    rows = hf * r
    tr = _row_tile(rows, c)
    core = lax.axis_index("c").astype(jnp.int32).reshape(1)

    def body(core_ref, x_ref, o_ref):
        acc = x_ref[0].astype(F32)
        for j in range(1, s):
            acc = acc + x_ref[j].astype(F32)
        o_ref[...] = acc

    out = pl.pallas_call(
        body, name=name,
        grid_spec=pltpu.PrefetchScalarGridSpec(
            num_scalar_prefetch=1, grid=(rows // tr,),
            in_specs=[pl.BlockSpec((s, tr, c), lambda i, core_ref: (0, i, 0))],
            out_specs=pl.BlockSpec((None, tr, c), lambda i, core_ref: (core_ref[0], i, 0))),
        out_shape=_sds((2, rows, c), F32), compiler_params=_params(("parallel",)),
    )(core, slabs.reshape(s, rows, c))
    return out.reshape(2 * hf, r, c)


def _add_own_layer(name, g0, g1, recv):
    s, r, c = g0.shape
    rows = s * r
    tr = _row_tile(rows, c)
    core = lax.axis_index("c").astype(jnp.int32).reshape(1)

    def body(core_ref, g0_ref, g1_ref, r_ref, o_ref):
        own = jnp.where(core_ref[0] == 0, g0_ref[...], g1_ref[...])
        o_ref[...] = (own + r_ref[...]).astype(BF16)

    flat = pl.BlockSpec((tr, c), lambda i, core_ref: (i, 0))
    out = pl.pallas_call(
        body, name=name,
        grid_spec=pltpu.PrefetchScalarGridSpec(
            num_scalar_prefetch=1, grid=(rows // tr,),
            in_specs=[pl.BlockSpec((tr, c), lambda i, core_ref: (i * (1 - core_ref[0]), 0)),
                      pl.BlockSpec((tr, c), lambda i, core_ref: (i * core_ref[0], 0)), flat],
            out_specs=flat),
        out_shape=_sds((rows, c), BF16), compiler_params=_params(("parallel",)),
    )(core, g0.reshape(rows, c), g1.reshape(rows, c), recv.reshape(rows, c))
    return out.reshape(1, s, r, c)


_ANY = pl.BlockSpec(memory_space=pl.ANY)


def _place():
    x, y, c = lax.axis_index("x"), lax.axis_index("y"), lax.axis_index("c")
    return x, y, c, [(1 - x, y), (x, 1 - y), (1 - x, 1 - y)]


def _remote(src, dst, send_sems, recv_sems, k, to):
    return pltpu.make_async_remote_copy(src_ref=src, dst_ref=dst, send_sem=send_sems.at[k], recv_sem=recv_sems.at[k],
                                        device_id=to, device_id_type=MESH)


def _comm_call(name, body, ins, out_shapes, n_remote, n_local, aliases=None):
    scratch = [pltpu.SemaphoreType.DMA((n_remote,)), pltpu.SemaphoreType.DMA((n_remote,))]
    if n_local:
        scratch.append(pltpu.SemaphoreType.DMA((n_local,)))
    return pl.pallas_call(
        body, name=name, in_specs=[_ANY] * len(ins), out_specs=[_ANY] * len(out_shapes), out_shape=out_shapes,
        scratch_shapes=scratch, input_output_aliases=aliases or {},
    )(*ins)


class _Comm(NamedTuple):
    start: object
    finish: object
    ins: tuple
    out_shapes: tuple
    n_remote: int
    n_local: int

    def scratch(self):
        sems = [pltpu.SemaphoreType.DMA((self.n_remote,)), pltpu.SemaphoreType.DMA((self.n_remote,))]
        return sems + ([pltpu.SemaphoreType.DMA((self.n_local,))] if self.n_local else [])


def _run_comm(name, comm):
    def body(*refs):
        comm.start(refs)
        comm.finish(refs)

    return pl.pallas_call(
        body, name=name, in_specs=[_ANY] * len(comm.ins), out_specs=[_ANY] * len(comm.out_shapes),
        out_shape=list(comm.out_shapes), scratch_shapes=comm.scratch(),
    )(*comm.ins)


def _gather_comm(shards, owned):
    nt = len(shards)

    def copies(refs):
        ins, outs, send_sems, recv_sems = refs[:nt], refs[nt:2 * nt], refs[2 * nt], refs[2 * nt + 1]
        x, y, c, chips = _place()
        me = 2 * x + y
        sibling = (x, y, 1 - c)
        own, first, arrive, passed, forwarded = [], [], [], [], []
        for t, (l0, l1) in enumerate(owned):
            src_mine = ins[t].at[pl.ds(l0 + c * (l1 - l0), 1)]
            src_theirs = ins[t].at[pl.ds(l1 - c * (l1 - l0), 1)]
            mine, theirs = pl.ds(c, 1), pl.ds(1 - c, 1)
            for k, layer in enumerate((l0, l1)):
                own.append(_remote(ins[t].at[pl.ds(layer, 1)], outs[t].at[me, pl.ds(k, 1)], send_sems, recv_sems,
                                   8 * t + 6 + k, sibling))
            for j, chip in enumerate(chips):
                got, fwd = outs[t].at[2 * chip[0] + chip[1], mine], outs[t].at[2 * chip[0] + chip[1], theirs]
                first.append(_remote(src_mine, outs[t].at[me, mine], send_sems, recv_sems, 8 * t + j, (*chip, c)))
                arrive.append(_remote(src_mine, got, send_sems, recv_sems, 8 * t + j, (x, y, c)))
                passed.append(_remote(got, got, send_sems, recv_sems, 8 * t + 3 + j, sibling))
                forwarded.append(_remote(src_theirs, fwd, send_sems, recv_sems, 8 * t + 3 + j, (x, y, c)))
        return own, first, arrive, passed, forwarded

    def start(refs):
        own, first, _, _, _ = copies(refs)
        for cp in own + first:
            cp.start()

    def finish(refs):
        own, first, arrive, passed, forwarded = copies(refs)
        for landed, onward in zip(arrive, passed):
            landed.wait_recv()
            onward.start()
        for cp in own + forwarded:
            cp.wait_recv()
        for cp in own + first + passed:
            cp.wait_send()

    shapes = tuple(_sds((N_CHIPS, 2) + s.shape[1:], s.dtype) for s in shards)
    return _Comm(start, finish, tuple(shards), shapes, 8 * nt, 0)


def _other_layers_to_sibling(name, pairs):
    nt = len(pairs)

    def body(*refs):
        ins, outs = refs[:2 * nt], refs[2 * nt:3 * nt]
        send_sems, recv_sems = refs[3 * nt:]
        x, y, c, _ = _place()
        for t in range(nt):
            for core in (0, 1):
                @pl.when(c == core)
                def _(t=t, core=core):
                    _remote(ins[2 * t + 1 - core], outs[t], send_sems, recv_sems, t, (x, y, 1 - c)).start()
        for t in range(nt):
            _remote(ins[2 * t], outs[t], send_sems, recv_sems, t, (x, y, 1 - c)).wait()

    flat = [g for pair in pairs for g in pair]
    return _comm_call(name, body, flat, [_sds(g0.shape, g0.dtype) for g0, _ in pairs], nt, 0)


def _scatter_comm(parts):
    nt = len(parts)

    def copies(refs):
        ins, outs = refs[:nt], refs[nt:2 * nt]
        send_sems, recv_sems, local_sems = refs[2 * nt:2 * nt + 3]
        x, y, c, chips = _place()
        me = 2 * x + y
        local, sends, arrive = [], [], []
        for t in range(nt):
            layers = pl.ds(0, ins[t].shape[0])
            local.append(pltpu.make_async_copy(ins[t].at[layers, me], outs[t].at[me], local_sems.at[t]))
            for j, chip in enumerate(chips):
                there = 2 * chip[0] + chip[1]
                sends.append(_remote(ins[t].at[layers, there], outs[t].at[me], send_sems, recv_sems, 3 * t + j, (*chip, c)))
                arrive.append(_remote(ins[t].at[layers, me], outs[t].at[there], send_sems, recv_sems, 3 * t + j, (x, y, c)))
        return local, sends, arrive

    def start(refs):
        local, sends, _ = copies(refs)
        for cp in local + sends:
            cp.start()

    def finish(refs):
        local, sends, arrive = copies(refs)
        for cp in arrive:
            cp.wait_recv()
        for cp in sends:
            cp.wait_send()
        for cp in local:
            cp.wait()

    shapes = tuple(_sds((p.shape[1], p.shape[0]) + p.shape[2:], p.dtype) for p in parts)
    return _Comm(start, finish, tuple(parts), shapes, 3 * nt, nt)


def _join_halves(name, bufs):
    nt = len(bufs)

    def body(*refs):
        outs = refs[nt:2 * nt]
        send_sems, recv_sems = refs[2 * nt:]
        x, y, c, _ = _place()
        sends = []
        for t in range(nt):
            half = outs[t].shape[0] // 2
            mine = pl.ds(c * half, half)
            sends.append(_remote(outs[t].at[mine], outs[t].at[mine], send_sems, recv_sems, t, (x, y, 1 - c)))
            sends[-1].start()
        for t in range(nt):
            half = outs[t].shape[0] // 2
            theirs = pl.ds((1 - c) * half, half)
            _remote(outs[t].at[theirs], outs[t].at[theirs], send_sems, recv_sems, t, (x, y, c)).wait_recv()
        for cp in sends:
            cp.wait_send()

    return _comm_call(name, body, bufs, [_sds(b.shape, b.dtype) for b in bufs], nt, 0,
                      aliases={t: t for t in range(nt)})


def _gather_all(name, block):
    rows, cols = block.shape

    def body(x_ref, out_ref, send_sems, recv_sems, local_sem):
        x, y, c, chips = _place()
        me, sibling = (x, y, c), (x, y, 1 - c)

        def at(px, py, pc):
            return out_ref.at[4 * px + 2 * py + pc]

        def copy(k, blk, to, src=None):
            return _remote(at(*blk) if src is None else src, at(*blk), send_sems, recv_sems, k, to)

        mine = pltpu.make_async_copy(x_ref, at(*me), local_sem)
        mine.start()
        first = [copy(0, me, sibling, src=x_ref)]
        first += [copy(1 + j, me, (*chip, c), src=x_ref) for j, chip in enumerate(chips)]
        for cp in first:
            cp.start()
        passed = [copy(4 + j, (*chip, c), sibling) for j, chip in enumerate(chips)]
        for j, chip in enumerate(chips):
            copy(1 + j, (*chip, c), me).wait_recv()
            passed[j].start()
        copy(0, sibling, me).wait_recv()
        for j, chip in enumerate(chips):
            copy(4 + j, (*chip, 1 - c), me).wait_recv()
        for cp in first + passed:
            cp.wait_send()
        mine.wait()

    return pl.pallas_call(
        body, name=name,
        in_specs=[pl.BlockSpec(memory_space=pltpu.VMEM)], out_specs=pl.BlockSpec(memory_space=pltpu.VMEM),
        out_shape=_sds((8, rows, cols), block.dtype),
        scratch_shapes=[pltpu.SemaphoreType.DMA((7,)), pltpu.SemaphoreType.DMA((7,)), pltpu.SemaphoreType.DMA],
        compiler_params=pltpu.CompilerParams(vmem_limit_bytes=V7X_VMEM_LIMIT),
    )(block)


WEIGHTS = ("ffn1_norm", "ffn1_w_gu", "ffn1_w_down", "mix_norm", "mem_norm", "w_mem_kv", "xq_norm", "xk_norm", "w_out",
           "ffn2_norm", "ffn2_w_gu", "ffn2_w_down", "sb_w_in", "s5_w_in", "s5_log_dt", "s5_a_re", "s5_a_im", "s5_b_re",
           "s5_b_im", "s5_c_re", "s5_c_im", "s5_d", "s5_w_glu")
BIG = ("ffn1_w_gu", "ffn1_w_down", "w_mem_kv", "w_out", "ffn2_w_gu", "ffn2_w_down", "sb_w_in", "s5_w_in", "s5_w_glu")
COLUMN_SHARDED = ("ffn1_w_gu", "ffn2_w_gu", "sb_w_in")
EVERY_LAYER = ("ffn1_w_gu", "ffn1_w_down", "w_mem_kv", "w_out", "ffn2_w_gu", "ffn2_w_down")
SMALL = tuple(n for n in WEIGHTS if n not in BIG)
GROUPS = (EVERY_LAYER + ("sb_w_in",), EVERY_LAYER + ("s5_w_in", "s5_w_glu"))


def _owned(name, group):
    return (group, group + 2) if name in EVERY_LAYER else (0, 1)


def _layer_weights(full, i):
    return {name: _W(full[name], i // 2, name in COLUMN_SHARDED) for name in GROUPS[i % 2]}


def _row(v):
    return v.reshape(1, -1)


def _layer_fwd(i, x, mem, w, p, s5prep, comm=None):
    tag = f"l{i}"
    j = i // 2
    comm_out = ()
    x1, sv_ffn1 = _ffn_fwd(f"{tag}_ffn1", x, _row(p["ffn1_norm"][i]), w["ffn1_w_gu"], w["ffn1_w_down"])
    h = _rmsnorm_fwd(f"{tag}_mixnorm", x1, _row(p["mix_norm"][i]))
    if i % 2 == 0:
        proj = _mm_nn(f"{tag}_inproj", h, w["sb_w_in"])
        n_heads = (proj.shape[1] * 3 // 10) // HEAD_DIM
        (tok, mix_saved), comm_out = _sb_fwd(f"{tag}_sb", proj, n_heads, comm)
        tok_w = n_heads * HEAD_DIM
        qcol0 = 3 * tok_w
    else:
        proj = _mm_nn(f"{tag}_inproj", h, w["s5_w_in"])
        tok, mix_saved = _s5_fwd(f"{tag}_s5", proj, s5prep[j], _row(p["s5_d"][j]), w["s5_w_glu"])
        tok_w = tok.shape[1]
        qcol0 = tok_w
    mem_h = _rmsnorm_fwd(f"{tag}_memnorm", mem, _row(p["mem_norm"][i]))
    kv = _mm_nn(f"{tag}_memkv", mem_h, w["w_mem_kv"])
    gq, gk = _row(p["xq_norm"][i]), _row(p["xk_norm"][i])
    cross = _mem_fwd(f"{tag}_mem", proj, qcol0, kv, gq, gk)
    cat = jnp.concatenate([tok, cross], axis=1).astype(BF16)
    x2 = _mm_nn(f"{tag}_outproj", cat, w["w_out"], extras=(x1,), epilogue=_add_residual)
    x3, sv_ffn2 = _ffn_fwd(f"{tag}_ffn2", x2, _row(p["ffn2_norm"][i]), w["ffn2_w_gu"], w["ffn2_w_down"])
    saved = dict(ffn1=sv_ffn1, x1=x1, h=h, proj=proj, mix=mix_saved, mem_h=mem_h, kv=kv, cat=cat, ffn2=sv_ffn2,
                 tok_w=tok_w, qcol0=qcol0)
    return x3, saved, comm_out


def _layer_bwd(i, sv, mem, w, p, s5prep, dx3, dx3b, comm=None):
    tag = f"l{i}b"
    j = i // 2
    g = {}
    comm_out = ()
    dx2, dx2b, g["ffn2_norm"], g["ffn2_w_gu"], g["ffn2_w_down"] = _ffn_bwd(
        f"{tag}_ffn2", sv["ffn2"], _row(p["ffn2_norm"][i]), w["ffn2_w_gu"], w["ffn2_w_down"], dx3, dx3b)
    dcat = _mm_nt(f"{tag}_dcat", dx2b, w["w_out"])
    g["w_out"] = _mm_tn(f"{tag}_dwout", sv["cat"], dx2b)
    gq, gk = _row(p["xq_norm"][i]), _row(p["xk_norm"][i])
    tok_w, qcol0 = sv["tok_w"], sv["qcol0"]
    dqm, dkv, g["xq_norm"], g["xk_norm"] = _mem_bwd(f"{tag}_mem", sv["proj"], qcol0, sv["kv"], gq, gk, dcat, tok_w)
    dkvb = dkv.astype(BF16)
    g["w_mem_kv"] = _mm_tn(f"{tag}_dwkv", sv["mem_h"], dkvb)
    dmem_h = _mm_nt(f"{tag}_dmemh", dkvb, w["w_mem_kv"])
    g["mem_norm"] = _rmsnorm_bwd(f"{tag}_dmemnorm", mem, dmem_h, _row(p["mem_norm"][i]))
    if i % 2 == 0:
        (dq, dk, dv), comm_out = _sb_bwd(f"{tag}_sb", sv["proj"], sv["mix"], dcat, tok_w // HEAD_DIM, comm)
        dproj = jnp.concatenate([dq, dk, dv, dqm], axis=1).astype(BF16)
        g["sb_w_in"] = _mm_tn(f"{tag}_dwin", sv["h"], dproj, shards=N_CHIPS)
        dh = _mm_nt(f"{tag}_dh", dproj, w["sb_w_in"])
    else:
        du, g["s5_prep"], g["s5_d"], g["s5_w_glu"] = _s5_bwd(
            f"{tag}_s5", sv["proj"], s5prep[j], _row(p["s5_d"][j]), w["s5_w_glu"], sv["mix"], dcat)
        dproj = jnp.concatenate([du, dqm], axis=1).astype(BF16)
        g["s5_w_in"] = _mm_tn(f"{tag}_dwin", sv["h"], dproj)
        dh = _mm_nt(f"{tag}_dh", dproj, w["s5_w_in"])
    dx1, dx1b, g["mix_norm"] = _rmsnorm_bwd(f"{tag}_dmixnorm", sv["x1"], dh, _row(p["mix_norm"][i]), dx2)
    dx, dxb, g["ffn1_norm"], g["ffn1_w_gu"], g["ffn1_w_down"] = _ffn_bwd(
        f"{tag}_ffn1", sv["ffn1"], _row(p["ffn1_norm"][i]), w["ffn1_w_gu"], w["ffn1_w_down"], dx1, dx1b)
    return dx, dxb, g, comm_out


def _group_grads(per_layer, group):
    out = []
    for name in GROUPS[group]:
        per = [per_layer[i][name] for i in (group, group + 2)]
        if name not in COLUMN_SHARDED:
            per = [gl.reshape((N_CHIPS, gl.shape[0] // N_CHIPS, gl.shape[1])) for gl in per]
        out.append(tuple(per))
    return out


def _reduce_group(group, gs, beside=None):
    names = GROUPS[group]
    tag = f"reduce{group}"
    from_sibling = _other_layers_to_sibling(f"{tag}_to_sibling", gs)
    parts = [_add_own_layer(f"{tag}_add_cores_{n}", g0, g1, r) for n, (g0, g1), r in zip(names, gs, from_sibling)]
    comm = _scatter_comm(parts)
    slabs = beside(comm) if beside else _run_comm(f"{tag}_scatter_chips", comm)
    bufs = [_sum_chips_into_half(f"{tag}_add_chips_{n}", sl) for n, sl in zip(names, slabs)]
    return dict(zip(names, _join_halves(f"{tag}_join_halves", bufs)))


def _local_step(x, mem, target, shards, p):
    depth = p["ffn1_norm"].shape[0]
    n_s5 = depth // 2
    s5_names = ("s5_log_dt", "s5_a_re", "s5_a_im", "s5_b_re", "s5_b_im", "s5_c_re", "s5_c_im")
    s5prep, s5vjp = [], []
    for j in range(n_s5):
        out, vjp = jax.vjp(_s5_prep, *[p[n][j] for n in s5_names])
        s5prep.append(out)
        s5vjp.append(vjp)

    def gather(group):
        return _gather_comm([shards[n] for n in GROUPS[group]], [_owned(n, group) for n in GROUPS[group]])

    full = [dict(zip(GROUPS[0], _run_comm("gather0_weights", gather(0)))), None]
    saved = []
    for i in range(depth):
        x, sv, got = _layer_fwd(i, x, mem, _layer_weights(full[i % 2], i), p, s5prep, gather(1) if i == 0 else None)
        if i == 0:
            full[1] = dict(zip(GROUPS[1], got))
        saved.append(sv)
    loss, dx, dxb = _loss_head("loss", x, target)
    per_layer = [None] * depth
    reduced = [None, None]
    for i in reversed(range(depth)):
        if i == 0:
            def beside(comm):
                nonlocal dx, dxb
                dx, dxb, per_layer[0], slabs = _layer_bwd(0, saved[0], mem, _layer_weights(full[0], 0), p, s5prep,
                                                          dx, dxb, comm)
                return slabs

            reduced[1] = _reduce_group(1, _group_grads(per_layer, 1), beside)
        else:
            dx, dxb, per_layer[i], _ = _layer_bwd(i, saved[i], mem, _layer_weights(full[i % 2], i), p, s5prep, dx, dxb)
    reduced[0] = _reduce_group(0, _group_grads(per_layer, 0))
    grads = {}
    for name in ("ffn1_norm", "mix_norm", "mem_norm", "xq_norm", "xk_norm", "ffn2_norm"):
        grads[name] = jnp.concatenate([per_layer[i][name] for i in range(depth)], axis=0)
    grads["s5_d"] = jnp.concatenate([per_layer[i]["s5_d"] for i in range(1, depth, 2)], axis=0)
    s5g = [s5vjp[j](tuple(per_layer[2 * j + 1]["s5_prep"])) for j in range(n_s5)]
    for k, name in enumerate(s5_names):
        grads[name] = jnp.stack([s5g[j][k] for j in range(n_s5)], axis=0)
    for name in BIG:
        if name in EVERY_LAYER:
            grads[name] = jnp.stack([reduced[i % 2][name][i // 2] for i in range(depth)], axis=0)
        else:
            grads[name] = reduced[0 if name in GROUPS[0] else 1][name]
    return loss, dx, grads


def _reduce_small(grads, shapes):
    flat = jnp.concatenate([grads[n].reshape(-1) for n in SMALL])
    total = flat.shape[0]
    rows = -(-total // (512 * LANES)) * 512
    block = jnp.pad(flat, (0, rows * LANES - total)).reshape(rows, LANES)
    summed = _sum_leading("reduce_small_sum", _gather_all("reduce_small_gather", block)).reshape(-1)
    out, off = {}, 0
    for n in SMALL:
        size = math.prod(shapes[n])
        out[n] = summed[off:off + size].reshape(shapes[n])
        off += size
    return out


def kernel(x, mem, ffn1_norm, ffn1_w_gu, ffn1_w_down, mix_norm, mem_norm, w_mem_kv, xq_norm, xk_norm, w_out, ffn2_norm, ffn2_w_gu, ffn2_w_down, sb_w_in, s5_w_in, s5_log_dt, s5_a_re, s5_a_im, s5_b_re, s5_b_im, s5_c_re, s5_c_im, s5_d, s5_w_glu, loss_target, m_ffn1_norm, m_ffn1_w_gu, m_ffn1_w_down, m_mix_norm, m_mem_norm, m_w_mem_kv, m_xq_norm, m_xk_norm, m_w_out, m_ffn2_norm, m_ffn2_w_gu, m_ffn2_w_down, m_sb_w_in, m_s5_w_in, m_s5_log_dt, m_s5_a_re, m_s5_a_im, m_s5_b_re, m_s5_b_im, m_s5_c_re, m_s5_c_im, m_s5_d, m_s5_w_glu, v_ffn1_norm, v_ffn1_w_gu, v_ffn1_w_down, v_mix_norm, v_mem_norm, v_w_mem_kv, v_xq_norm, v_xk_norm, v_w_out, v_ffn2_norm, v_ffn2_w_gu, v_ffn2_w_down, v_sb_w_in, v_s5_w_in, v_s5_log_dt, v_s5_a_re, v_s5_a_im, v_s5_b_re, v_s5_b_im, v_s5_c_re, v_s5_c_im, v_s5_d, v_s5_w_glu):
    given = dict(locals())
    wts = {n: given[n] for n in WEIGHTS}
    chip = 2 * lax.axis_index("x") + lax.axis_index("y")

    d_sh = s5_d.shape
    d_rows = -(-math.prod(d_sh) // (8 * LANES)) * 8
    d_block = jnp.pad(s5_d.reshape(-1), (0, d_rows * LANES - math.prod(d_sh))).reshape(d_rows, LANES)
    d_all = _gather_all("gather_s5_d", d_block).reshape(8, -1)[:, :math.prod(d_sh)]
    d_full = jnp.concatenate([d_all[2 * s].reshape(d_sh) for s in range(N_CHIPS)], axis=1)

    p = {n: wts[n] for n in SMALL}
    p["s5_d"] = d_full
    loss, grad_x, grads = _local_step(x[0], mem[0], loss_target[0], {n: wts[n].astype(BF16) for n in BIG}, p)

    small_shapes = {n: wts[n].shape for n in SMALL}
    small_shapes["s5_d"] = d_full.shape
    gsum = {n: grads[n] for n in BIG}
    gsmall = _reduce_small(grads, small_shapes)
    gsmall["s5_d"] = lax.dynamic_slice_in_dim(gsmall["s5_d"], chip * d_sh[1], d_sh[1], axis=1)
    gsum.update(gsmall)

    deltas, new_m, new_v = {}, {}, {}
    for n in WEIGHTS:
        deltas[n], new_m[n], new_v[n] = _adamw(f"adamw_{n}", wts[n], gsum[n], given["m_" + n], given["v_" + n])
    total_loss = lax.psum(loss[0, 0], ("x", "y", "c"))
    return (total_loss, grad_x[None], *[gsum[n] for n in WEIGHTS], *[deltas[n] for n in WEIGHTS],
            *[new_m[n] for n in WEIGHTS], *[new_v[n] for n in WEIGHTS])
```
